```python
import math
import jax, jax.numpy as jnp
from jax import lax
import numpy as np

D_MODEL = 1024
BATCH = 32
SEQ = 2048
DEPTH = 1

D_MIX = D_MODEL
DIFF_WIDTH = D_MIX // 2
DIFF_HEADS = 4
DIFF_QK_DIM = 64
DIFF_V_DIM = DIFF_WIDTH // DIFF_HEADS
NSA_WIDTH = D_MIX - DIFF_WIDTH
NSA_HEADS = 8
NSA_HEAD_DIM = NSA_WIDTH // NSA_HEADS
NSA_KV_HEADS = 2
NSA_GROUP = NSA_HEADS // NSA_KV_HEADS
CMP_BLOCK = 32
CMP_STRIDE = 16
CMP_HIDDEN = 256
SEL_BLOCK = 64
SEL_TOPK = 8
N_LOCAL_BLOCKS = 2
FORCE_BONUS = 1000.0
WINDOW = 512
Q_BLOCK = 128
SEL_Q_BLOCK = 32
RMS_EPS = 1e-6
NEG_BIG = -1e30

COL_DIFF_Q = DIFF_HEADS * 2 * DIFF_QK_DIM
COL_DIFF_K = DIFF_HEADS * 2 * DIFF_QK_DIM
COL_DIFF_V = DIFF_WIDTH
COL_DIFF_Z = DIFF_WIDTH
COL_NSA_Q = NSA_WIDTH
COL_NSA_KV = 6 * NSA_KV_HEADS * NSA_HEAD_DIM
COL_NSA_Z = NSA_WIDTH
COL_NSA_GATE = 3 * NSA_HEADS
IN_COLS = COL_DIFF_Q + COL_DIFF_K + COL_DIFF_V + COL_DIFF_Z + COL_NSA_Q + COL_NSA_KV + COL_NSA_Z + COL_NSA_GATE
IN_SPLITS = [int(v) for v in np.cumsum([COL_DIFF_Q, COL_DIFF_K, COL_DIFF_V, COL_DIFF_Z, COL_NSA_Q, COL_NSA_KV, COL_NSA_Z])]

kernel_name = 'hymba_diffattn_nsa_block'


def rms_norm(x, g):
    xf = x.astype(jnp.float32)
    r = lax.rsqrt(jnp.mean(xf * xf, axis=-1, keepdims=True) + RMS_EPS)
    return (xf * r).astype(x.dtype) * g


def alibi_slopes(n):
    return jnp.asarray([2.0 ** (-8.0 * (h + 1) / n) for h in range(n)], dtype=jnp.float32)


def diff_attention(q, k, v, lam, slopes):
    B, H, _, T, _ = q.shape
    dv = v.shape[-1]
    scale = DIFF_QK_DIM ** -0.5
    kpos = jnp.arange(T)

    def block(start):
        qb = lax.dynamic_slice_in_dim(q, start, Q_BLOCK, axis=3)
        s = jnp.einsum('bhcqd,bhckd->bhcqk', qb, k).astype(jnp.float32) * scale
        qpos = start + jnp.arange(Q_BLOCK)
        dist = (qpos[:, None] - kpos[None, :]).astype(jnp.float32)
        s = s - slopes[None, :, None, None, None] * dist
        s = jnp.where(dist >= 0, s, -jnp.inf)
        p = jax.nn.softmax(s, axis=-1)
        p = p[:, :, 0] - lam * p[:, :, 1]
        return jnp.einsum('bhqk,bhkd->bhqd', p.astype(v.dtype), v)

    out = lax.map(block, jnp.arange(T // Q_BLOCK) * Q_BLOCK)
    return out.transpose(1, 2, 0, 3, 4).reshape(B, H, T, dv)


def compress(kv, pos, w1, b1, w2):
    B, KVH, T, d = kv.shape
    n_cmp = (T - CMP_BLOCK) // CMP_STRIDE + 1
    idx = jnp.arange(n_cmp)[:, None] * CMP_STRIDE + jnp.arange(CMP_BLOCK)[None, :]
    blocks = kv[:, :, idx] + pos
    flat = blocks.reshape(B, KVH, n_cmp, CMP_BLOCK * d)
    return jax.nn.silu(flat @ w1 + b1) @ w2


def compressed_attention(q, kc, vc):
    T = q.shape[3]
    n_cmp = kc.shape[2]
    scale = NSA_HEAD_DIM ** -0.5
    s = jnp.einsum('bhgtd,bhcd->bhgtc', q, kc).astype(jnp.float32) * scale
    cend = jnp.arange(n_cmp) * CMP_STRIDE + CMP_BLOCK - 1
    valid = cend[None, :] <= jnp.arange(T)[:, None]
    p = jax.nn.softmax(jnp.where(valid, s, NEG_BIG), axis=-1)
    p = jnp.where(valid, p, 0.0)
    o = jnp.einsum('bhgtc,bhcd->bhgtd', p.astype(vc.dtype), vc)
    return o, p


def select_blocks(p_cmp):
    T, n_cmp = p_cmp.shape[3], p_cmp.shape[4]
    n_sb = T // SEL_BLOCK
    cstart = jnp.arange(n_cmp) * CMP_STRIDE
    sstart = jnp.arange(n_sb) * SEL_BLOCK
    overlap = ((cstart[:, None] < sstart[None, :] + SEL_BLOCK)
               & (cstart[:, None] + CMP_BLOCK > sstart[None, :])).astype(jnp.float32)
    imp = jnp.einsum('bhgtc,cs->bhts', p_cmp, overlap)
    blk_t = jnp.arange(T)[:, None] // SEL_BLOCK
    j = jnp.arange(n_sb)[None, :]
    valid = j <= blk_t
    forced = valid & ((j == 0) | (j >= blk_t - (N_LOCAL_BLOCKS - 1)))
    score = jnp.where(valid, imp, -1.0) + jnp.where(forced, FORCE_BONUS, 0.0)
    _, idx = lax.top_k(score, min(SEL_TOPK, n_sb))
    return idx


def selected_attention(q, k, v, sel_idx, slopes):
    B, KVH, G, T, d = q.shape
    n = sel_idx.shape[-1]
    scale = NSA_HEAD_DIM ** -0.5
    kb = k.reshape(B, KVH, T // SEL_BLOCK, SEL_BLOCK, d)
    vb = v.reshape(B, KVH, T // SEL_BLOCK, SEL_BLOCK, d)
    bi = jnp.arange(B)[:, None, None, None]
    hi = jnp.arange(KVH)[None, :, None, None]
    offs = jnp.arange(SEL_BLOCK)
    n_keys = n * SEL_BLOCK

    def chunk(start):
        qc = lax.dynamic_slice_in_dim(q, start, SEL_Q_BLOCK, axis=3)
        ic = lax.dynamic_slice_in_dim(sel_idx, start, SEL_Q_BLOCK, axis=2)
        kg = kb[bi, hi, ic].reshape(B, KVH, SEL_Q_BLOCK, n_keys, d)
        vg = vb[bi, hi, ic].reshape(B, KVH, SEL_Q_BLOCK, n_keys, d)
        kpos = (ic[..., None] * SEL_BLOCK + offs).reshape(B, KVH, SEL_Q_BLOCK, n_keys)
        qpos = start + jnp.arange(SEL_Q_BLOCK)
        dist = (qpos[None, None, :, None] - kpos).astype(jnp.float32)[:, :, None]
        s = jnp.einsum('bhgqd,bhqkd->bhgqk', qc, kg).astype(jnp.float32) * scale
        s = s - slopes[None, :, :, None, None] * dist
        s = jnp.where(dist >= 0, s, -jnp.inf)
        p = jax.nn.softmax(s, axis=-1)
        return jnp.einsum('bhgqk,bhqkd->bhgqd', p.astype(vg.dtype), vg)

    out = lax.map(chunk, jnp.arange(T // SEL_Q_BLOCK) * SEL_Q_BLOCK)
    return out.transpose(1, 2, 3, 0, 4, 5).reshape(B, KVH, G, T, d)


def window_attention(q, k, v, slopes):
    B, KVH, G, T, d = q.shape
    scale = NSA_HEAD_DIM ** -0.5
    span = WINDOW + Q_BLOCK
    kp = jnp.pad(k, ((0, 0), (0, 0), (WINDOW, 0), (0, 0)))
    vp = jnp.pad(v, ((0, 0), (0, 0), (WINDOW, 0), (0, 0)))

    def block(start):
        qb = lax.dynamic_slice_in_dim(q, start, Q_BLOCK, axis=3)
        kbk = lax.dynamic_slice_in_dim(kp, start, span, axis=2)
        vbk = lax.dynamic_slice_in_dim(vp, start, span, axis=2)
        qpos = start + jnp.arange(Q_BLOCK)
        kpos = start - WINDOW + jnp.arange(span)
        dist = (qpos[:, None] - kpos[None, :]).astype(jnp.float32)
        valid = (dist >= 0) & (dist < WINDOW) & (kpos[None, :] >= 0)
        s = jnp.einsum('bhgqd,bhkd->bhgqk', qb, kbk).astype(jnp.float32) * scale
        s = s - slopes[None, :, :, None, None] * dist
        s = jnp.where(valid, s, -jnp.inf)
        p = jax.nn.softmax(s, axis=-1)
        return jnp.einsum('bhgqk,bhkd->bhgqd', p.astype(vbk.dtype), vbk)

    out = lax.map(block, jnp.arange(T // Q_BLOCK) * Q_BLOCK)
    return out.transpose(1, 2, 3, 0, 4, 5).reshape(B, KVH, G, T, d)


def setup_inputs(seed: int = 0) -> dict:
    key = jax.random.key(seed)
    ks = jax.random.split(key, 20)
    f32 = jnp.float32
    nrm = lambda k, shape, s: jax.random.normal(k, shape, f32) * s
    return {
        'x': nrm(ks[0], (BATCH, SEQ, D_MODEL), 1.0),
        'norm_g': 1.0 + nrm(ks[1], (DEPTH, D_MODEL), 0.02),
        'w_in': nrm(ks[2], (DEPTH, D_MODEL, IN_COLS), D_MODEL ** -0.5),
        'diff_q_norm_g': 1.0 + nrm(ks[3], (DEPTH, DIFF_QK_DIM), 0.02),
        'diff_k_norm_g': 1.0 + nrm(ks[4], (DEPTH, DIFF_QK_DIM), 0.02),
        'diff_lambda_q1': nrm(ks[5], (DEPTH, DIFF_QK_DIM), 0.1),
        'diff_lambda_k1': nrm(ks[6], (DEPTH, DIFF_QK_DIM), 0.1),
        'diff_lambda_q2': nrm(ks[7], (DEPTH, DIFF_QK_DIM), 0.1),
        'diff_lambda_k2': nrm(ks[8], (DEPTH, DIFF_QK_DIM), 0.1),
        'diff_subln_g': 1.0 + nrm(ks[9], (DEPTH, DIFF_V_DIM), 0.02),
        'nsa_q_norm_g': 1.0 + nrm(ks[10], (DEPTH, NSA_HEAD_DIM), 0.02),
        'nsa_k_norm_g': 1.0 + nrm(ks[11], (DEPTH, 3, NSA_HEAD_DIM), 0.02),
        'cmp_pos': nrm(ks[12], (DEPTH, 2, CMP_BLOCK, NSA_HEAD_DIM), 0.1),
        'cmp_w1': nrm(ks[13], (DEPTH, 2, CMP_BLOCK * NSA_HEAD_DIM, CMP_HIDDEN), (CMP_BLOCK * NSA_HEAD_DIM) ** -0.5),
        'cmp_b1': nrm(ks[14], (DEPTH, 2, CMP_HIDDEN), 0.01),
        'cmp_w2': nrm(ks[15], (DEPTH, 2, CMP_HIDDEN, NSA_HEAD_DIM), CMP_HIDDEN ** -0.5),
        'w_out': nrm(ks[16], (DEPTH, D_MIX, D_MODEL), D_MIX ** -0.5),
    }


def reference(x, norm_g, w_in, diff_q_norm_g, diff_k_norm_g, diff_lambda_q1, diff_lambda_k1,
              diff_lambda_q2, diff_lambda_k2, diff_subln_g, nsa_q_norm_g, nsa_k_norm_g,
              cmp_pos, cmp_w1, cmp_b1, cmp_w2, w_out):
    B, T, _ = x.shape
    f32 = jnp.float32
    diff_slopes = alibi_slopes(DIFF_HEADS)
    nsa_slopes = alibi_slopes(NSA_HEADS).reshape(NSA_KV_HEADS, NSA_GROUP)
    for layer in range(DEPTH):
        lam_init = 0.8 - 0.6 * math.exp(-0.3 * layer)
        h = rms_norm(x, norm_g[layer])
        proj = h @ w_in[layer]
        dq, dk, dv, dz, nq, nkv, nz, ng = jnp.split(proj, IN_SPLITS, axis=-1)

        dq = rms_norm(dq.reshape(B, T, DIFF_HEADS, 2, DIFF_QK_DIM).transpose(0, 2, 3, 1, 4), diff_q_norm_g[layer])
        dk = rms_norm(dk.reshape(B, T, DIFF_HEADS, 2, DIFF_QK_DIM).transpose(0, 2, 3, 1, 4), diff_k_norm_g[layer])
        dv = dv.reshape(B, T, DIFF_HEADS, DIFF_V_DIM).transpose(0, 2, 1, 3)
        lam = (jnp.exp(jnp.sum(diff_lambda_q1[layer].astype(f32) * diff_lambda_k1[layer].astype(f32)))
               - jnp.exp(jnp.sum(diff_lambda_q2[layer].astype(f32) * diff_lambda_k2[layer].astype(f32)))
               + lam_init)
        o_diff = diff_attention(dq, dk, dv, lam, diff_slopes)
        o_diff = rms_norm(o_diff, diff_subln_g[layer]) * (1.0 - lam_init)
        o_diff = o_diff.transpose(0, 2, 1, 3).reshape(B, T, DIFF_WIDTH) * jax.nn.silu(dz)

        nq = rms_norm(nq.reshape(B, T, NSA_KV_HEADS, NSA_GROUP, NSA_HEAD_DIM).transpose(0, 2, 3, 1, 4),
                      nsa_q_norm_g[layer])
        nkv = nkv.reshape(B, T, 6, NSA_KV_HEADS, NSA_HEAD_DIM).transpose(2, 0, 3, 1, 4)
        kc = compress(nkv[0], cmp_pos[layer, 0], cmp_w1[layer, 0], cmp_b1[layer, 0], cmp_w2[layer, 0])
        vc = compress(nkv[1], cmp_pos[layer, 1], cmp_w1[layer, 1], cmp_b1[layer, 1], cmp_w2[layer, 1])
        kc = rms_norm(kc, nsa_k_norm_g[layer, 0])
        k_slc = rms_norm(nkv[2], nsa_k_norm_g[layer, 1])
        k_win = rms_norm(nkv[4], nsa_k_norm_g[layer, 2])
        o_cmp, p_cmp = compressed_attention(nq, kc, vc)
        sel_idx = select_blocks(p_cmp)
        o_slc = selected_attention(nq, k_slc, nkv[3], sel_idx, nsa_slopes)
        o_win = window_attention(nq, k_win, nkv[5], nsa_slopes)
        g = jax.nn.sigmoid(ng.astype(f32)).astype(x.dtype)
        g = g.reshape(B, T, 3, NSA_KV_HEADS, NSA_GROUP).transpose(2, 0, 3, 4, 1)[..., None]
        o_nsa = g[0] * o_cmp + g[1] * o_slc + g[2] * o_win
        o_nsa = o_nsa.transpose(0, 3, 1, 2, 4).reshape(B, T, NSA_WIDTH) * jax.nn.silu(nz)

        x = x + jnp.concatenate([o_diff, o_nsa], axis=-1) @ w_out[layer]
    return x
```

```python
import functools
import math

import numpy as np
import jax
import jax.numpy as jnp
from jax import lax
from jax.experimental import pallas as pl
from jax.experimental.pallas import tpu as pltpu

F32 = jnp.float32
BF16 = jnp.bfloat16

D_MODEL = 1024
N_DIFF_HEADS = 4
DIFF_QK = 64
DIFF_V = 128
DIFF_W = 512
N_NSA_HEADS = 8
NSA_D = 64
N_KV = 2
N_GRP = 4
NSA_W = 512
CMP_BLOCK = 32
CMP_STRIDE = 16
CMP_HIDDEN = 256
SEL_BLOCK = 64
SEL_TOPK = 8
N_LOCAL = 2
FORCE_BONUS = 1000.0
WINDOW = 512
EPS = 1e-6
NEG_BIG = -1e30
LAM_INIT = 0.8 - 0.6 * math.exp(-0.3 * 0)

IN_MAIN = 3840
N_GATE = 24
IN_PAD = 3968
LANES = 128
SEL_MASK_BIAS = -32768.0

TM_PROJ = 512
TQ = 256
TK = 256
VMEM_LIMIT = 56 * 1024 * 1024


def _dot(a, b):
    return jnp.dot(a, b, preferred_element_type=F32)


def _dot_nt(a, b):
    return lax.dot_general(a, b, (((1,), (1,)), ((), ())), preferred_element_type=F32)


def _sigmoid(y):
    return 1.0 / (1.0 + jnp.exp(-y))


def _seg_mean_sq(y, e):
    n = y.shape[1]
    y2 = (y * y).astype(BF16)
    if n == LANES:
        return _dot(y2, e[:LANES, :LANES])
    return jnp.concatenate([_dot(y2[:, c:c + 256], e) for c in range(0, n, 256)], axis=1)


def _seg_norm(y, e, gain):
    return y * lax.rsqrt(_seg_mean_sq(y, e) + EPS) * gain


def _proj_kernel(x_ref, g_ref, w_ref, e_ref, gq_ref, gk_ref, gnq_ref, gks_ref, gkw_ref, fs_ref, fw_ref,
                 dq_ref, dk_ref, dv_ref, dz_ref, nq_ref, kc_ref, vc_ref, ks_ref, vs_ref, kw_ref, vw_ref,
                 nz_ref, gate_ref):
    x = x_ref[...]
    ms = jnp.mean(x * x, axis=-1, keepdims=True)
    h = (x * lax.rsqrt(ms + EPS) * g_ref[...]).astype(BF16)
    e = e_ref[...]

    def proj(a, b):
        return _dot(h, w_ref[:, a:b])

    dq_ref[...] = _seg_norm(proj(0, 512), e, gq_ref[...]).astype(BF16)
    dk_ref[...] = _seg_norm(proj(512, 1024), e, gk_ref[...]).astype(BF16)
    dv_ref[...] = proj(1024, 1536).astype(BF16)
    y = proj(1536, 2048)
    dz_ref[...] = (y * _sigmoid(y)).astype(BF16)
    nq_ref[...] = _seg_norm(proj(2048, 2560), e, gnq_ref[...]).astype(BF16)
    kc_ref[...] = proj(2560, 2688).astype(BF16)
    vc_ref[...] = proj(2688, 2816).astype(BF16)

    low = lax.broadcasted_iota(jnp.int32, (1, LANES), 1) < 64
    kp = _seg_norm(proj(2816, 2944), e, gks_ref[...]).astype(BF16)
    fs = fs_ref[...]
    ks_ref[0] = jnp.where(low, kp, fs)
    ks_ref[1] = jnp.where(low, fs, kp)
    vs_ref[...] = proj(2944, 3072).astype(BF16)
    kp = _seg_norm(proj(3072, 3200), e, gkw_ref[...]).astype(BF16)
    fw = fw_ref[...]
    kw_ref[0] = jnp.where(low, kp, fw)
    kw_ref[1] = jnp.where(low, fw, kp)
    vw_ref[...] = proj(3200, 3328).astype(BF16)
    y = proj(3328, 3840)
    nz_ref[...] = (y * _sigmoid(y)).astype(BF16)
    gate_ref[...] = _sigmoid(proj(3840, 3968))


def _cmp_kernel(hk_ref, hv_ref, pa_ref, pb_ref, w1a_ref, w1b_ref, b1_ref, w2_ref, e_ref, gkc_ref,
                kc_ref, vc_ref):
    n_rows = hk_ref.shape[1]
    row_ok = lax.broadcasted_iota(jnp.int32, (n_rows, 1), 0) < (n_rows - 1)
    low = lax.broadcasted_iota(jnp.int32, (1, LANES), 1) < 64

    def mlp(h_ref, idx):
        hf = h_ref[0].astype(F32)
        ha = (hf + pa_ref[idx]).astype(BF16)
        hb = (hf + pb_ref[idx]).astype(BF16)
        a = _dot(ha, w1a_ref[idx])
        b = _dot(hb, w1b_ref[idx])
        hid = a + pltpu.roll(b, n_rows - 1, 0) + b1_ref[idx]
        hid = hid * _sigmoid(hid)
        return _dot(hid.astype(BF16), w2_ref[idx])

    yk = mlp(hk_ref, 0)
    yk = _seg_norm(yk, e_ref[...], gkc_ref[...])
    yk = jnp.where(row_ok, yk, 0.0).astype(BF16)
    zero = jnp.zeros_like(yk)
    kc_ref[0, 0] = jnp.where(low, yk, zero)
    kc_ref[0, 1] = jnp.where(low, zero, yk)
    yv = mlp(hv_ref, 1)
    vc_ref[0] = jnp.where(row_ok, yv, 0.0).astype(BF16)


def _diff_kernel(slope_ref, q_ref, k_ref, v_ref, z_ref, lam_ref, gsub_ref, o_ref, m_ref, l_ref, acc_ref):
    hd = pl.program_id(1)
    i = pl.program_id(2)
    slope = slope_ref[hd]
    lane = lax.broadcasted_iota(jnp.int32, (1, LANES), 1)
    q = q_ref[...]
    zero = jnp.zeros_like(q)
    qs = jnp.concatenate([jnp.where(lane < 64, q, zero), jnp.where(lane >= 64, q, zero)], axis=0)

    m_ref[...] = jnp.full(m_ref.shape, NEG_BIG, F32)
    l_ref[...] = jnp.zeros(l_ref.shape, F32)
    acc_ref[...] = jnp.zeros(acc_ref.shape, F32)
    col = lax.broadcasted_iota(jnp.int32, (1, TK), 1)

    def block(j, diagonal):
        k0 = pl.multiple_of(j * TK, TK)
        kb = k_ref[pl.ds(k0, TK), :]
        vb = v_ref[pl.ds(k0, TK), :]
        s = _dot_nt(qs, kb)
        s = s + slope * (k0 + col).astype(F32)
        if diagonal:
            row = lax.broadcasted_iota(jnp.int32, (TQ, 1), 0)
            keep = (col <= row)[None]
            s = jnp.where(keep, s.reshape(2, TQ, TK), -jnp.inf).reshape(2 * TQ, TK)
        m_prev = m_ref[...]
        m_new = jnp.maximum(m_prev, jnp.max(s, axis=-1, keepdims=True))
        alpha = jnp.exp(m_prev - m_new)
        p = jnp.exp(s - m_new)
        l_ref[...] = alpha * l_ref[...] + jnp.sum(p, axis=-1, keepdims=True)
        acc_ref[...] = alpha * acc_ref[...] + _dot(p.astype(BF16), vb)
        m_ref[...] = m_new

    def body(j, c):
        block(j, False)
        return c

    lax.fori_loop(0, i, body, 0)
    block(i, True)

    lam_v = lam_ref[...]
    lam = (jnp.exp(jnp.sum(lam_v[0:1] * lam_v[1:2], axis=-1, keepdims=True))
           - jnp.exp(jnp.sum(lam_v[2:3] * lam_v[3:4], axis=-1, keepdims=True)) + LAM_INIT)
    o = acc_ref[...] / l_ref[...]
    o = o[:TQ] - lam * o[TQ:]
    ms = jnp.mean(o * o, axis=-1, keepdims=True)
    o = o * lax.rsqrt(ms + EPS) * gsub_ref[...] * (1.0 - LAM_INIT)
    o_ref[...] = (o * z_ref[...].astype(F32)).astype(BF16)


def _nsa_kernel(q_ref, kc_ref, vc_ref, ks_ref, vs_ref, kw_ref, vw_ref, gate_ref, nz_ref, qf_ref, ov_ref,
                o_ref, m_ref, l_ref, acc_ref):
    i = pl.program_id(1)
    kvh = pl.program_id(2)
    q0 = i * TQ
    lane = lax.broadcasted_iota(jnp.int32, (1, LANES), 1)
    half = (lane >> 6) == kvh
    qf = qf_ref[0]
    q_parts = []
    for g in range(N_GRP):
        qp = q_ref[:, g * LANES:(g + 1) * LANES]
        q_parts.append(jnp.where(half, qp, qf[g:g + 1, :]))
    qb = jnp.concatenate(q_parts, axis=0)
    row = lax.broadcasted_iota(jnp.int32, (TQ, 1), 0)
    tpos = q0 + row

    s = _dot_nt(qb, kc_ref[0, 0])
    valid = (lane * CMP_STRIDE + (CMP_BLOCK - 1)) <= tpos
    s3 = jnp.where(valid[None], s.reshape(N_GRP, TQ, LANES), NEG_BIG)
    mx = jnp.max(s3, axis=-1, keepdims=True)
    ex = jnp.where(valid[None], jnp.exp(s3 - mx), 0.0)
    den = jnp.sum(ex, axis=-1, keepdims=True)
    p3 = ex * jnp.where(den > 0.0, 1.0 / den, 0.0)
    o_cmp = _dot(p3.reshape(N_GRP * TQ, LANES).astype(BF16), vc_ref[0])
    psum = p3[0] + p3[1] + p3[2] + p3[3]

    ov = ov_ref[0]
    p_hi = psum.astype(BF16)
    p_lo = (psum - p_hi.astype(F32)).astype(BF16)
    imp = _dot(p_hi, ov) + _dot(p_lo, ov)
    bidx = lane - (96 - 64 * kvh)
    in_seg = (bidx >= 0) & (bidx < 32)
    blk_t = tpos // SEL_BLOCK
    valid_b = in_seg & (bidx <= blk_t)
    forced = valid_b & ((bidx == 0) | (bidx >= blk_t - (N_LOCAL - 1)))
    score = jnp.where(valid_b, imp, -1.0) + jnp.where(forced, FORCE_BONUS, 0.0)
    score = jnp.where(in_seg, score, -jnp.inf)
    lane_f = lane.astype(F32)
    chosen = jnp.zeros(score.shape, F32)
    for _ in range(SEL_TOPK):
        best = jnp.max(score, axis=-1, keepdims=True)
        first = jnp.min(jnp.where(score == best, lane_f, 1e9), axis=-1, keepdims=True)
        pick = lane_f == first
        chosen = jnp.where(pick, 1.0, chosen)
        score = jnp.where(pick, -jnp.inf, score)
    sel_bias = jnp.where(in_seg & (chosen == 0.0), SEL_MASK_BIAS, 0.0).astype(BF16)
    qsel = jnp.concatenate([jnp.where(in_seg, sel_bias, qp_) for qp_ in q_parts], axis=0)

    col = lax.broadcasted_iota(jnp.int32, (1, TK), 1)

    def reset():
        m_ref[...] = jnp.full(m_ref.shape, NEG_BIG, F32)
        l_ref[...] = jnp.zeros(l_ref.shape, F32)
        acc_ref[...] = jnp.zeros(acc_ref.shape, F32)

    def online(s, vb):
        m_prev = m_ref[...]
        m_new = jnp.maximum(m_prev, jnp.max(s, axis=-1, keepdims=True))
        alpha = jnp.exp(m_prev - m_new)
        p = jnp.exp(s - m_new)
        l_ref[...] = alpha * l_ref[...] + jnp.sum(p, axis=-1, keepdims=True)
        acc_ref[...] = alpha * acc_ref[...] + _dot(p.astype(BF16), vb)
        m_ref[...] = m_new

    def mask3(s, keep):
        return jnp.where(keep[None], s.reshape(N_GRP, TQ, TK), -jnp.inf).reshape(N_GRP * TQ, TK)

    reset()

    def sel_block(j, diagonal):
        k0 = pl.multiple_of(j * TK, TK)
        s = _dot_nt(qsel, ks_ref[kvh, pl.ds(k0, TK), :])
        if diagonal:
            s = mask3(s, col <= row)
        online(s, vs_ref[pl.ds(k0, TK), :])

    def sel_body(j, c):
        sel_block(j, False)
        return c

    lax.fori_loop(0, i, sel_body, 0)
    sel_block(i, True)
    o_slc = acc_ref[...] / l_ref[...]

    reset()

    def win_body(j, c):
        k0 = pl.multiple_of(j * TK, TK)
        s = _dot_nt(qb, kw_ref[kvh, pl.ds(k0, TK), :])
        dist = tpos - (k0 + col)
        s = mask3(s, (dist >= 0) & (dist < WINDOW))
        online(s, vw_ref[pl.ds(k0, TK), :])
        return c

    lax.fori_loop(jnp.maximum(i - WINDOW // TK, 0), i + 1, win_body, 0)
    o_win = acc_ref[...] / l_ref[...]

    gate = gate_ref[...]

    def gate_col(br):
        cols = []
        for g in range(N_GRP):
            c0 = gate[:, br * 8 + g:br * 8 + g + 1]
            c1 = gate[:, br * 8 + 4 + g:br * 8 + 4 + g + 1]
            cols.append(jnp.where(kvh == 0, c0, c1))
        return jnp.concatenate(cols, axis=0)

    o = gate_col(0) * o_cmp + gate_col(1) * o_slc + gate_col(2) * o_win
    for g in range(N_GRP):
        og = o[g * TQ:(g + 1) * TQ]
        val = jnp.where(half, og, 0.0) * nz_ref[:, g * LANES:(g + 1) * LANES].astype(F32)
        sl = slice(g * LANES, (g + 1) * LANES)

        @pl.when(kvh == 0)
        def _():
            o_ref[:, sl] = val.astype(BF16)

        @pl.when(kvh != 0)
        def _():
            o_ref[:, sl] = (o_ref[:, sl].astype(F32) + val).astype(BF16)


def _out_kernel(x_ref, od_ref, on_ref, w_ref, o_ref):
    acc = _dot(od_ref[...], w_ref[:DIFF_W, :]) + _dot(on_ref[...], w_ref[DIFF_W:, :])
    o_ref[...] = x_ref[...] + acc


def _params(sem):
    return pltpu.CompilerParams(dimension_semantics=sem, vmem_limit_bytes=VMEM_LIMIT)


def _head_perm():
    new = np.arange(NSA_W).reshape(N_GRP, N_KV, NSA_D)
    g, kvh, d = np.meshgrid(np.arange(N_GRP), np.arange(N_KV), np.arange(NSA_D), indexing="ij")
    old = kvh * (N_GRP * NSA_D) + g * NSA_D + d
    perm = np.zeros(NSA_W, np.int64)
    perm[new.reshape(-1)] = old.reshape(-1)
    return perm


def _key_features(seq, with_blocks):
    t = np.arange(seq)
    f = np.zeros((seq, 64), np.float32)
    f[:, 0] = t // 256
    f[:, 1] = t % 256
    if with_blocks:
        f[t, 32 + t // SEL_BLOCK] = 1.0
    return jnp.asarray(np.concatenate([f, f], axis=1), dtype=BF16)


def _query_features():
    f = np.zeros((N_KV, N_GRP, LANES), np.float32)
    for kvh in range(N_KV):
        base = 64 if kvh == 0 else 0
        for g in range(N_GRP):
            slope = 2.0 ** (-(kvh * N_GRP + g + 1))
            f[kvh, g, base + 0] = slope * 256.0
            f[kvh, g, base + 1] = slope
    return jnp.asarray(f, dtype=BF16)


def _overlap(seq):
    n_cmp = (seq - CMP_BLOCK) // CMP_STRIDE + 1
    n_sb = seq // SEL_BLOCK
    cs = np.arange(n_cmp) * CMP_STRIDE
    ss = np.arange(n_sb) * SEL_BLOCK
    ovl = ((cs[:, None] < ss[None, :] + SEL_BLOCK) & (cs[:, None] + CMP_BLOCK > ss[None, :])).astype(np.float32)
    out = np.zeros((N_KV, LANES, LANES), np.float32)
    out[0, :n_cmp, 96:96 + n_sb] = ovl
    out[1, :n_cmp, 32:32 + n_sb] = ovl
    return jnp.asarray(out, dtype=BF16)


def kernel(x, norm_g, w_in, diff_q_norm_g, diff_k_norm_g, diff_lambda_q1, diff_lambda_k1, diff_lambda_q2,
           diff_lambda_k2, diff_subln_g, nsa_q_norm_g, nsa_k_norm_g, cmp_pos, cmp_w1, cmp_b1, cmp_w2, w_out):
    B, T, D = x.shape
    BT = B * T
    assert D == D_MODEL and T % TQ == 0 and T == 2048 and BT % TM_PROJ == 0 and T % TM_PROJ == 0
    n_half = T // CMP_STRIDE
    nq = T // TQ
    perm = _head_perm()

    w = w_in[0]
    col = np.arange(IN_MAIN)
    col[2048:2560] = 2048 + perm
    col[3328:3840] = 3328 + perm
    w_p = jnp.concatenate([w[:, col], w[:, IN_MAIN:], jnp.zeros((D, IN_PAD - IN_MAIN - N_GATE), w.dtype)],
                          axis=1).astype(BF16)
    w_o = jnp.concatenate([w_out[0][:DIFF_W], w_out[0][DIFF_W:][perm]], axis=0).astype(BF16)
    seg = np.arange(256) // 64
    e_mat = jnp.asarray((seg[:, None] == seg[None, :]).astype(np.float32) / 64.0, dtype=BF16)
    gq = (jnp.tile(diff_q_norm_g[0], 8) * DIFF_QK ** -0.5)[None]
    gk = jnp.tile(diff_k_norm_g[0], 8)[None]
    gnq = (jnp.tile(nsa_q_norm_g[0], 8) * NSA_D ** -0.5)[None]
    gkc = jnp.tile(nsa_k_norm_g[0, 0], 2)[None]
    gks = jnp.tile(nsa_k_norm_g[0, 1], 2)[None]
    gkw = jnp.tile(nsa_k_norm_g[0, 2], 2)[None]
    feat_s = _key_features(T, True)
    feat_w = _key_features(T, False)

    x2 = x.reshape(BT, D)
    n_rt = BT // TM_PROJ
    rt_per_seq = T // TM_PROJ
    row_spec = lambda n: pl.BlockSpec((TM_PROJ, n), lambda i: (i, 0))
    full = lambda shp: pl.BlockSpec(shp, lambda i: (0,) * len(shp))
    feat_spec = pl.BlockSpec((TM_PROJ, LANES), lambda i: (i % rt_per_seq, 0))
    pair_spec = pl.BlockSpec((2, TM_PROJ, LANES), lambda i: (0, i, 0))
    bt = lambda n, dt=BF16: jax.ShapeDtypeStruct((BT, n), dt)
    outs = pl.pallas_call(
        _proj_kernel,
        grid=(n_rt,),
        in_specs=[row_spec(D), full((1, D)), full((D, IN_PAD)), full((256, 256)),
                  full((1, 512)), full((1, 512)), full((1, 512)), full((1, LANES)), full((1, LANES)),
                  feat_spec, feat_spec],
        out_specs=[row_spec(512), row_spec(512), row_spec(512), row_spec(512), row_spec(512),
                   row_spec(LANES), row_spec(LANES), pair_spec, row_spec(LANES), pair_spec, row_spec(LANES),
                   row_spec(512), row_spec(LANES)],
        out_shape=[bt(512), bt(512), bt(512), bt(512), bt(512), bt(LANES), bt(LANES),
                   jax.ShapeDtypeStruct((2, BT, LANES), BF16), bt(LANES),
                   jax.ShapeDtypeStruct((2, BT, LANES), BF16), bt(LANES), bt(512), bt(LANES, F32)],
        compiler_params=_params(("parallel",)),
    )(x2, norm_g[0][None], w_p, e_mat, gq, gk, gnq, gks, gkw, feat_s, feat_w)
    dq, dk, dv, dz, nqp, kcr, vcr, ks, vs, kw, vw, nz, gates = outs

    def pair_w1(w1):
        wh = w1.reshape(2, 16, NSA_D, CMP_HIDDEN)
        z = jnp.zeros((2, 16, N_KV, NSA_D, N_KV, CMP_HIDDEN), w1.dtype)
        z = z.at[:, :, 0, :, 0, :].set(wh).at[:, :, 1, :, 1, :].set(wh)
        z = z.reshape(2, 16 * N_KV * NSA_D, N_KV * CMP_HIDDEN).astype(BF16)
        return z[0], z[1]

    def pair_w2(w2):
        z = jnp.zeros((N_KV, CMP_HIDDEN, N_KV, NSA_D), w2.dtype)
        z = z.at[0, :, 0, :].set(w2).at[1, :, 1, :].set(w2)
        return z.reshape(N_KV * CMP_HIDDEN, N_KV * NSA_D).astype(BF16)

    def pair_pos(p):
        ph = p.reshape(2, 16, 1, NSA_D)
        ph = jnp.broadcast_to(ph, (2, 16, N_KV, NSA_D)).reshape(2, 1, 16 * N_KV * NSA_D)
        return ph[0], ph[1]

    w1k = pair_w1(cmp_w1[0, 0]); w1v = pair_w1(cmp_w1[0, 1])
    w1a = jnp.stack([w1k[0], w1v[0]]); w1b = jnp.stack([w1k[1], w1v[1]])
    w2p = jnp.stack([pair_w2(cmp_w2[0, 0]), pair_w2(cmp_w2[0, 1])])
    pk = pair_pos(cmp_pos[0, 0]); pv = pair_pos(cmp_pos[0, 1])
    pa = jnp.stack([pk[0], pv[0]]); pb = jnp.stack([pk[1], pv[1]])
    b1p = jnp.stack([jnp.tile(cmp_b1[0, 0], 2)[None], jnp.tile(cmp_b1[0, 1], 2)[None]])
    hk = kcr.reshape(B, n_half, CMP_STRIDE * LANES)
    hv = vcr.reshape(B, n_half, CMP_STRIDE * LANES)
    hspec = pl.BlockSpec((1, n_half, CMP_STRIDE * LANES), lambda b: (b, 0, 0))
    kc, vc = pl.pallas_call(
        _cmp_kernel,
        grid=(B,),
        in_specs=[hspec, hspec, full((2, 1, 2048)), full((2, 1, 2048)), full((2, 2048, 512)),
                  full((2, 2048, 512)), full((2, 1, 512)), full((2, 512, LANES)), full((256, 256)),
                  full((1, LANES))],
        out_specs=[pl.BlockSpec((1, 2, n_half, LANES), lambda b: (b, 0, 0, 0)),
                   pl.BlockSpec((1, n_half, LANES), lambda b: (b, 0, 0))],
        out_shape=[jax.ShapeDtypeStruct((B, 2, n_half, LANES), BF16),
                   jax.ShapeDtypeStruct((B, n_half, LANES), BF16)],
        compiler_params=_params(("parallel",)),
    )(hk, hv, pa, pb, w1a, w1b, b1p, w2p, e_mat, gkc)

    slopes = jnp.asarray([2.0 ** (-8.0 * (h + 1) / N_DIFF_HEADS) for h in range(N_DIFF_HEADS)], F32)
    lam_v = jnp.concatenate([diff_lambda_q1, diff_lambda_k1, diff_lambda_q2, diff_lambda_k2], axis=0).astype(F32)
    qd_spec = pl.BlockSpec((TQ, LANES), lambda b, h, i: (b * nq + i, h))
    kd_spec = pl.BlockSpec((T, LANES), lambda b, h, i: (b, h))
    o_diff = pl.pallas_call(
        _diff_kernel,
        grid=(B, N_DIFF_HEADS, nq),
        in_specs=[pl.BlockSpec(memory_space=pltpu.SMEM), qd_spec, kd_spec, kd_spec, qd_spec,
                  pl.BlockSpec((4, DIFF_QK), lambda b, h, i: (0, 0)),
                  pl.BlockSpec((1, DIFF_V), lambda b, h, i: (0, 0))],
        out_specs=qd_spec,
        out_shape=bt(DIFF_W),
        scratch_shapes=[pltpu.VMEM((2 * TQ, 1), F32), pltpu.VMEM((2 * TQ, 1), F32),
                        pltpu.VMEM((2 * TQ, DIFF_V), F32)],
        compiler_params=_params(("parallel", "parallel", "arbitrary")),
    )(slopes, dq, dk, dv, dz, lam_v, diff_subln_g[0][None])

    qn_spec = pl.BlockSpec((TQ, 512), lambda b, i, h: (b * nq + i, 0))
    kpair_spec = pl.BlockSpec((2, T, LANES), lambda b, i, h: (0, b, 0))
    vpair_spec = pl.BlockSpec((T, LANES), lambda b, i, h: (b, 0))
    o_nsa = pl.pallas_call(
        _nsa_kernel,
        grid=(B, nq, N_KV),
        in_specs=[qn_spec,
                  pl.BlockSpec((1, 1, n_half, LANES), lambda b, i, h: (b, h, 0, 0)),
                  pl.BlockSpec((1, n_half, LANES), lambda b, i, h: (b, 0, 0)),
                  kpair_spec, vpair_spec, kpair_spec, vpair_spec,
                  pl.BlockSpec((TQ, LANES), lambda b, i, h: (b * nq + i, 0)),
                  qn_spec,
                  pl.BlockSpec((1, N_GRP, LANES), lambda b, i, h: (h, 0, 0)),
                  pl.BlockSpec((1, LANES, LANES), lambda b, i, h: (h, 0, 0))],
        out_specs=qn_spec,
        out_shape=bt(NSA_W),
        scratch_shapes=[pltpu.VMEM((N_GRP * TQ, 1), F32), pltpu.VMEM((N_GRP * TQ, 1), F32),
                        pltpu.VMEM((N_GRP * TQ, LANES), F32)],
        compiler_params=_params(("parallel", "arbitrary", "arbitrary")),
    )(nqp, kc, vc, ks, vs, kw, vw, gates, nz, _query_features(), _overlap(T))

    out = pl.pallas_call(
        _out_kernel,
        grid=(n_rt,),
        in_specs=[row_spec(D), row_spec(DIFF_W), row_spec(NSA_W), full((D, D))],
        out_specs=row_spec(D),
        out_shape=jax.ShapeDtypeStruct((BT, D), x.dtype),
        compiler_params=_params(("parallel",)),
    )(x2, o_diff, o_nsa, w_o)
    return out.reshape(B, T, D)
```

```python
import math

import numpy as np
import jax
import jax.numpy as jnp
from jax import lax
from jax.experimental import pallas as pl
from jax.experimental.pallas import tpu as pltpu

F32 = jnp.float32
BF16 = jnp.bfloat16

D_MODEL = 1024
N_DIFF_HEADS = 4
DIFF_QK = 64
DIFF_V = 128
DIFF_W = 512
N_NSA_HEADS = 8
NSA_D = 64
N_KV = 2
N_GRP = 4
NSA_W = 512
CMP_BLOCK = 32
CMP_STRIDE = 16
CMP_HIDDEN = 256
SEL_BLOCK = 64
SEL_TOPK = 8
N_LOCAL = 2
FORCE_BONUS = 1000.0
WINDOW = 512
EPS = 1e-6
NEG_BIG = -1e30
LAM_INIT = 0.8 - 0.6 * math.exp(-0.3 * 0)
LOG2E = math.log2(math.e)

IN_MAIN = 3840
N_GATE = 24
IN_PAD = 3968
LANES = 128
SUBLANES = 8
SEL_MASK_BIAS = -32768.0

TM_PROJ = 512
TQ = 256
TK = 256
VMEM_LIMIT = 56 * 1024 * 1024
DIFF_ACC_ROWS = DIFF_V + SUBLANES


def _dot(a, b):
    return jnp.dot(a, b, preferred_element_type=F32)


def _sigmoid(y):
    return 1.0 / (1.0 + jnp.exp(-y))


def _seg_mean_sq(y, e):
    n = y.shape[1]
    y2 = (y * y).astype(BF16)
    if n == LANES:
        return _dot(y2, e[:LANES, :LANES])
    return jnp.concatenate([_dot(y2[:, c:c + 256], e) for c in range(0, n, 256)], axis=1)


def _seg_norm(y, e, gain):
    return y * lax.rsqrt(_seg_mean_sq(y, e) + EPS) * gain


def _transpose_bf16(a):
    return a.astype(F32).T.astype(BF16)


def _mask_cols(st, keep, width):
    n = st.shape[1]
    return jnp.concatenate([jnp.where(keep, st[:, a:a + width], -jnp.inf) for a in range(0, n, width)], axis=1)


def _online_softmax_step(st, vt_blk, m_ref, acc_ref, slot=None):
    m_prev = m_ref[...]
    m_new = jnp.maximum(m_prev, jnp.max(st, axis=0, keepdims=True))
    alpha = jnp.exp2(m_prev - m_new)
    p = jnp.exp2(st - m_new).astype(BF16)
    pv = _dot(vt_blk, p)
    if slot is None:
        acc_ref[...] = alpha * acc_ref[...] + pv
    else:
        acc_ref[slot] = alpha * acc_ref[slot] + pv
    m_ref[...] = m_new


def _proj_kernel(x_ref, g_ref, w_ref, e_ref, gq_ref, gk_ref, gnq_ref, gks_ref, gkw_ref, fs_ref, fw_ref,
                 dq_ref, dk_ref, dv_ref, dz_ref, nq_ref, kc_ref, vc_ref, ks_ref, vs_ref, kw_ref, vw_ref,
                 nz_ref, gate_ref):
    x = x_ref[...]
    ms = jnp.mean(x * x, axis=-1, keepdims=True)
    h = (x * lax.rsqrt(ms + EPS) * g_ref[...]).astype(BF16)
    e = e_ref[...]

    def proj(a, b):
        return _dot(h, w_ref[:, a:b])

    dq_ref[...] = _seg_norm(proj(0, 512), e, gq_ref[...]).astype(BF16)
    dk_ref[...] = _seg_norm(proj(512, 1024), e, gk_ref[...]).astype(BF16)
    dv_ref[...] = proj(1024, 1536).astype(BF16)
    y = proj(1536, 2048)
    dz_ref[...] = (y * _sigmoid(y)).astype(BF16)
    nq_ref[...] = _seg_norm(proj(2048, 2560), e, gnq_ref[...]).astype(BF16)
    kc_ref[...] = proj(2560, 2688).astype(BF16)
    vc_ref[...] = proj(2688, 2816).astype(BF16)

    low = lax.broadcasted_iota(jnp.int32, (1, LANES), 1) < 64
    kp = _seg_norm(proj(2816, 2944), e, gks_ref[...]).astype(BF16)
    fs = fs_ref[...]
    ks_ref[0] = jnp.where(low, kp, fs)
    ks_ref[1] = jnp.where(low, fs, kp)
    vs_ref[...] = proj(2944, 3072).astype(BF16)
    kp = _seg_norm(proj(3072, 3200), e, gkw_ref[...]).astype(BF16)
    fw = fw_ref[...]
    kw_ref[0] = jnp.where(low, kp, fw)
    kw_ref[1] = jnp.where(low, fw, kp)
    vw_ref[...] = proj(3200, 3328).astype(BF16)
    y = proj(3328, 3840)
    nz_ref[...] = (y * _sigmoid(y)).astype(BF16)
    gate_ref[...] = _sigmoid(proj(3840, 3968))


def _cmp_kernel(hk_ref, hv_ref, pa_ref, pb_ref, w1a_ref, w1b_ref, b1_ref, w2_ref, e_ref, gkc_ref,
                kc_ref, vct_ref):
    n_rows = hk_ref.shape[1]
    row_ok = lax.broadcasted_iota(jnp.int32, (n_rows, 1), 0) < (n_rows - 1)
    low = lax.broadcasted_iota(jnp.int32, (1, LANES), 1) < 64

    def mlp(h_ref, idx):
        hf = h_ref[0].astype(F32)
        ha = (hf + pa_ref[idx]).astype(BF16)
        hb = (hf + pb_ref[idx]).astype(BF16)
        a = _dot(ha, w1a_ref[idx])
        b = _dot(hb, w1b_ref[idx])
        hid = a + pltpu.roll(b, n_rows - 1, 0) + b1_ref[idx]
        hid = hid * _sigmoid(hid)
        return _dot(hid.astype(BF16), w2_ref[idx])

    yk = mlp(hk_ref, 0)
    yk = _seg_norm(yk, e_ref[...], gkc_ref[...])
    yk = jnp.where(row_ok, yk, 0.0).astype(BF16)
    zero = jnp.zeros_like(yk)
    kc_ref[0, 0] = jnp.where(low, yk, zero)
    kc_ref[0, 1] = jnp.where(low, zero, yk)
    yv = mlp(hv_ref, 1)
    vct_ref[0] = jnp.where(row_ok, yv, 0.0).T.astype(BF16)


def _diff_kernel(slope_ref, q_ref, k_ref, v_ref, z_ref, lam_ref, gsub_ref, o_ref, vt_ref, m_ref, acc_ref):
    hd = pl.program_id(1)
    i = pl.program_id(2)
    seq = k_ref.shape[0]

    @pl.when(i == 0)
    def _():
        vt_ref[0:DIFF_V, :] = _transpose_bf16(v_ref[...])
        ones_row = lax.broadcasted_iota(jnp.int32, (SUBLANES, seq), 0) == 0
        vt_ref[DIFF_V:DIFF_ACC_ROWS, :] = jnp.where(ones_row, 1.0, 0.0).astype(BF16)

    slope = slope_ref[hd]
    qt = _transpose_bf16(q_ref[...])
    rows = lax.broadcasted_iota(jnp.int32, (LANES, TQ), 0)
    zero = jnp.zeros_like(qt)
    qt2 = jnp.concatenate([jnp.where(rows < 64, qt, zero), jnp.where(rows >= 64, qt, zero)], axis=1)

    m_ref[...] = jnp.full(m_ref.shape, NEG_BIG, F32)
    acc_ref[...] = jnp.zeros(acc_ref.shape, F32)
    key_row = lax.broadcasted_iota(jnp.int32, (TK, LANES), 0)

    def block(j, diagonal):
        k0 = pl.multiple_of(j * TK, TK)
        st = _dot(k_ref[pl.ds(k0, TK), :], qt2)
        bias = slope * (k0 + key_row).astype(F32)
        st = jnp.concatenate([st[:, a:a + LANES] + bias for a in range(0, 2 * TQ, LANES)], axis=1)
        if diagonal:
            keep = (lax.broadcasted_iota(jnp.int32, (TK, TQ), 0)
                    <= lax.broadcasted_iota(jnp.int32, (TK, TQ), 1))
            st = _mask_cols(st, keep, TQ)
        _online_softmax_step(st, vt_ref[:, pl.ds(k0, TK)], m_ref, acc_ref)

    def body(j, c):
        block(j, False)
        return c

    lax.fori_loop(0, i, body, 0)
    block(i, True)

    lam_v = lam_ref[...]
    lam = (jnp.exp(jnp.sum(lam_v[0:1] * lam_v[1:2], axis=-1, keepdims=True))
           - jnp.exp(jnp.sum(lam_v[2:3] * lam_v[3:4], axis=-1, keepdims=True)) + LAM_INIT)
    acc = acc_ref[...]
    o2 = acc[0:DIFF_V] / acc[DIFF_V:DIFF_V + 1]
    o = (o2[:, :TQ] - lam * o2[:, TQ:]).T
    ms = jnp.mean(o * o, axis=-1, keepdims=True)
    o = o * lax.rsqrt(ms + EPS) * gsub_ref[...] * (1.0 - LAM_INIT)
    o_ref[...] = (o * z_ref[...].astype(F32)).astype(BF16)


def _nsa_kernel(q_ref, kc_ref, vct_ref, ks_ref, vs_ref, kw_ref, vw_ref, gate_ref, nz_ref, qf_ref, ovt_ref,
                o_ref, vst_ref, vwt_ref, m_ref, acc_ref):
    i = pl.program_id(1)
    kvh = pl.program_id(2)
    q0 = i * TQ
    n_all = N_GRP * TQ
    rows = lax.broadcasted_iota(jnp.int32, (LANES, TQ), 0)
    half = (rows >> 6) == kvh
    ones_row = rows == jnp.where(kvh == 0, 64, 0)

    @pl.when((i == 0) & (kvh == 0))
    def _():
        vst_ref[...] = _transpose_bf16(vs_ref[...])
        vwt_ref[...] = _transpose_bf16(vw_ref[...])

    def value_rows(vt_blk):
        return jnp.where(half, vt_blk, jnp.where(ones_row, 1.0, 0.0).astype(BF16))

    qts = []
    for g in range(N_GRP):
        qt = _transpose_bf16(q_ref[:, g * LANES:(g + 1) * LANES])
        feat = qf_ref[0, g]
        qts.append(jnp.where(half, qt, jnp.concatenate([feat] * (TQ // LANES), axis=1)))
    qb = jnp.concatenate(qts, axis=1)
    qpos = q0 + lax.broadcasted_iota(jnp.int32, (1, TQ), 1)

    st = _dot(kc_ref[0, 0], qb)
    crow = lax.broadcasted_iota(jnp.int32, (LANES, TQ), 0)
    valid = (crow * CMP_STRIDE + (CMP_BLOCK - 1)) <= qpos
    p_parts = []
    for g in range(N_GRP):
        sg = jnp.where(valid, st[:, g * TQ:(g + 1) * TQ], NEG_BIG)
        mx = jnp.max(sg, axis=0, keepdims=True)
        ex = jnp.where(valid, jnp.exp2(sg - mx), 0.0)
        den = jnp.sum(ex, axis=0, keepdims=True)
        p_parts.append(ex * jnp.where(den > 0.0, 1.0 / den, 0.0))
    pt = jnp.concatenate(p_parts, axis=1)
    o_cmp = _dot(vct_ref[0], pt.astype(BF16))
    psum = p_parts[0] + p_parts[1] + p_parts[2] + p_parts[3]

    ovt = ovt_ref[0]
    p_hi = psum.astype(BF16)
    p_lo = (psum - p_hi.astype(F32)).astype(BF16)
    imp = _dot(ovt, p_hi) + _dot(ovt, p_lo)
    bidx = crow - (96 - 64 * kvh)
    in_seg = (bidx >= 0) & (bidx < 32)
    blk_t = qpos >> 6
    valid_b = in_seg & (bidx <= blk_t)
    forced = valid_b & ((bidx == 0) | (bidx >= blk_t - (N_LOCAL - 1)))
    score = jnp.where(valid_b, imp, -1.0) + jnp.where(forced, FORCE_BONUS, 0.0)
    score = jnp.where(in_seg, score, -jnp.inf)
    row_f = crow.astype(F32)
    chosen = jnp.zeros(score.shape, F32)
    for _ in range(SEL_TOPK):
        best = jnp.max(score, axis=0, keepdims=True)
        first = jnp.min(jnp.where(score == best, row_f, 1e9), axis=0, keepdims=True)
        pick = row_f == first
        chosen = jnp.where(pick, 1.0, chosen)
        score = jnp.where(pick, -jnp.inf, score)
    sel_bias = jnp.where(in_seg & (chosen == 0.0), SEL_MASK_BIAS, 0.0).astype(BF16)
    qsel = jnp.concatenate([jnp.where(in_seg, sel_bias, qt_) for qt_ in qts], axis=1)

    krow = lax.broadcasted_iota(jnp.int32, (TK, TQ), 0)
    qcol = lax.broadcasted_iota(jnp.int32, (TK, TQ), 1)

    def reset(slot):
        m_ref[...] = jnp.full(m_ref.shape, NEG_BIG, F32)
        acc_ref[slot] = jnp.zeros((LANES, n_all), F32)

    reset(0)

    def sel_block(j, diagonal):
        k0 = pl.multiple_of(j * TK, TK)
        st = _dot(ks_ref[kvh, pl.ds(k0, TK), :], qsel)
        if diagonal:
            st = _mask_cols(st, krow <= qcol, TQ)
        _online_softmax_step(st, value_rows(vst_ref[:, pl.ds(k0, TK)]), m_ref, acc_ref, 0)

    def sel_body(j, c):
        sel_block(j, False)
        return c

    lax.fori_loop(0, i, sel_body, 0)
    sel_block(i, True)

    reset(1)

    def win_block(j, keep):
        k0 = pl.multiple_of(j * TK, TK)
        st = _dot(kw_ref[kvh, pl.ds(k0, TK), :], qb)
        if keep is not None:
            st = _mask_cols(st, keep, TQ)
        _online_softmax_step(st, value_rows(vwt_ref[:, pl.ds(k0, TK)]), m_ref, acc_ref, 1)

    @pl.when(i >= 2)
    def _():
        win_block(i - 2, krow > qcol)

    @pl.when(i >= 1)
    def _():
        win_block(i - 1, None)

    win_block(i, krow <= qcol)

    gt = gate_ref[...].T

    def gate_row(br, g):
        r0 = br * 8 + g
        r1 = br * 8 + 4 + g
        return jnp.where(kvh == 0, gt[r0:r0 + 1], gt[r1:r1 + 1])

    def norm_row(a):
        return jnp.where(kvh == 0, a[64:65], a[0:1])

    for g in range(N_GRP):
        cs = slice(g * TQ, (g + 1) * TQ)
        a_s = acc_ref[0, :, cs]
        a_w = acc_ref[1, :, cs]
        ot = (gate_row(0, g) * o_cmp[:, cs]
              + (gate_row(1, g) / norm_row(a_s)) * a_s
              + (gate_row(2, g) / norm_row(a_w)) * a_w)
        og = jnp.where(half, ot, 0.0).T
        val = og * nz_ref[:, g * LANES:(g + 1) * LANES].astype(F32)
        sl = slice(g * LANES, (g + 1) * LANES)

        @pl.when(kvh == 0)
        def _():
            o_ref[:, sl] = val.astype(BF16)

        @pl.when(kvh != 0)
        def _():
            o_ref[:, sl] = (o_ref[:, sl].astype(F32) + val).astype(BF16)


def _out_kernel(x_ref, od_ref, on_ref, w_ref, o_ref):
    acc = _dot(od_ref[...], w_ref[:DIFF_W, :]) + _dot(on_ref[...], w_ref[DIFF_W:, :])
    o_ref[...] = x_ref[...] + acc


def _params(sem):
    return pltpu.CompilerParams(dimension_semantics=sem, vmem_limit_bytes=VMEM_LIMIT)


def _head_perm():
    new = np.arange(NSA_W).reshape(N_GRP, N_KV, NSA_D)
    g, kvh, d = np.meshgrid(np.arange(N_GRP), np.arange(N_KV), np.arange(NSA_D), indexing="ij")
    old = kvh * (N_GRP * NSA_D) + g * NSA_D + d
    perm = np.zeros(NSA_W, np.int64)
    perm[new.reshape(-1)] = old.reshape(-1)
    return perm


def _bf16_pieces(v):
    out = []
    r = np.float32(v)
    for _ in range(3):
        p = np.asarray(r).astype(BF16).astype(np.float32)
        out.append(float(p))
        r = np.float32(r - p)
    return out


def _key_features(seq, with_blocks):
    t = np.arange(seq)
    f = np.zeros((seq, 64), np.float32)
    for r in range(3):
        f[:, 2 * r] = t // 256
        f[:, 2 * r + 1] = t % 256
    if with_blocks:
        f[t, 32 + t // SEL_BLOCK] = 1.0
    return jnp.asarray(np.concatenate([f, f], axis=1), dtype=BF16)


def _query_features():
    f = np.zeros((N_KV, N_GRP, LANES), np.float32)
    for kvh in range(N_KV):
        base = 64 if kvh == 0 else 0
        for g in range(N_GRP):
            slope = 2.0 ** (-(kvh * N_GRP + g + 1))
            for r, piece in enumerate(_bf16_pieces(slope * LOG2E)):
                f[kvh, g, base + 2 * r] = piece * 256.0
                f[kvh, g, base + 2 * r + 1] = piece
    return jnp.asarray(np.repeat(f[..., None], LANES, axis=-1), dtype=BF16)


def _overlap_t(seq):
    n_cmp = (seq - CMP_BLOCK) // CMP_STRIDE + 1
    n_sb = seq // SEL_BLOCK
    cs = np.arange(n_cmp) * CMP_STRIDE
    ss = np.arange(n_sb) * SEL_BLOCK
    ovl = ((cs[:, None] < ss[None, :] + SEL_BLOCK) & (cs[:, None] + CMP_BLOCK > ss[None, :])).astype(np.float32)
    out = np.zeros((N_KV, LANES, LANES), np.float32)
    out[0, 96:96 + n_sb, :n_cmp] = ovl.T
    out[1, 32:32 + n_sb, :n_cmp] = ovl.T
    return jnp.asarray(out, dtype=BF16)


def kernel(x, norm_g, w_in, diff_q_norm_g, diff_k_norm_g, diff_lambda_q1, diff_lambda_k1, diff_lambda_q2,
           diff_lambda_k2, diff_subln_g, nsa_q_norm_g, nsa_k_norm_g, cmp_pos, cmp_w1, cmp_b1, cmp_w2, w_out):
    B, T, D = x.shape
    BT = B * T
    assert D == D_MODEL and T % TQ == 0 and T == 2048 and BT % TM_PROJ == 0 and T % TM_PROJ == 0
    n_half = T // CMP_STRIDE
    nq = T // TQ
    perm = _head_perm()

    w = w_in[0]
    col = np.arange(IN_MAIN)
    col[2048:2560] = 2048 + perm
    col[3328:3840] = 3328 + perm
    w_p = jnp.concatenate([w[:, col], w[:, IN_MAIN:], jnp.zeros((D, IN_PAD - IN_MAIN - N_GATE), w.dtype)],
                          axis=1).astype(BF16)
    w_o = jnp.concatenate([w_out[0][:DIFF_W], w_out[0][DIFF_W:][perm]], axis=0).astype(BF16)
    seg = np.arange(256) // 64
    e_mat = jnp.asarray((seg[:, None] == seg[None, :]).astype(np.float32) / 64.0, dtype=BF16)
    gq = (jnp.tile(diff_q_norm_g[0], 8) * (DIFF_QK ** -0.5 * LOG2E))[None]
    gk = jnp.tile(diff_k_norm_g[0], 8)[None]
    gnq = (jnp.tile(nsa_q_norm_g[0], 8) * (NSA_D ** -0.5 * LOG2E))[None]
    gkc = jnp.tile(nsa_k_norm_g[0, 0], 2)[None]
    gks = jnp.tile(nsa_k_norm_g[0, 1], 2)[None]
    gkw = jnp.tile(nsa_k_norm_g[0, 2], 2)[None]
    feat_s = _key_features(T, True)
    feat_w = _key_features(T, False)

    x2 = x.reshape(BT, D)
    n_rt = BT // TM_PROJ
    rt_per_seq = T // TM_PROJ
    row_spec = lambda n: pl.BlockSpec((TM_PROJ, n), lambda i: (i, 0))
    full = lambda shp: pl.BlockSpec(shp, lambda i: (0,) * len(shp))
    feat_spec = pl.BlockSpec((TM_PROJ, LANES), lambda i: (i % rt_per_seq, 0))
    pair_spec = pl.BlockSpec((2, TM_PROJ, LANES), lambda i: (0, i, 0))
    bt = lambda n, dt=BF16: jax.ShapeDtypeStruct((BT, n), dt)
    outs = pl.pallas_call(
        _proj_kernel,
        grid=(n_rt,),
        in_specs=[row_spec(D), full((1, D)), full((D, IN_PAD)), full((256, 256)),
                  full((1, 512)), full((1, 512)), full((1, 512)), full((1, LANES)), full((1, LANES)),
                  feat_spec, feat_spec],
        out_specs=[row_spec(512), row_spec(512), row_spec(512), row_spec(512), row_spec(512),
                   row_spec(LANES), row_spec(LANES), pair_spec, row_spec(LANES), pair_spec, row_spec(LANES),
                   row_spec(512), row_spec(LANES)],
        out_shape=[bt(512), bt(512), bt(512), bt(512), bt(512), bt(LANES), bt(LANES),
                   jax.ShapeDtypeStruct((2, BT, LANES), BF16), bt(LANES),
                   jax.ShapeDtypeStruct((2, BT, LANES), BF16), bt(LANES), bt(512), bt(LANES, F32)],
        compiler_params=_params(("parallel",)),
    )(x2, norm_g[0][None], w_p, e_mat, gq, gk, gnq, gks, gkw, feat_s, feat_w)
    dq, dk, dv, dz, nqp, kcr, vcr, ks, vs, kw, vw, nz, gates = outs

    def pair_w1(w1):
        wh = w1.reshape(2, 16, NSA_D, CMP_HIDDEN)
        z = jnp.zeros((2, 16, N_KV, NSA_D, N_KV, CMP_HIDDEN), w1.dtype)
        z = z.at[:, :, 0, :, 0, :].set(wh).at[:, :, 1, :, 1, :].set(wh)
        z = z.reshape(2, 16 * N_KV * NSA_D, N_KV * CMP_HIDDEN).astype(BF16)
        return z[0], z[1]

    def pair_w2(w2):
        z = jnp.zeros((N_KV, CMP_HIDDEN, N_KV, NSA_D), w2.dtype)
        z = z.at[0, :, 0, :].set(w2).at[1, :, 1, :].set(w2)
        return z.reshape(N_KV * CMP_HIDDEN, N_KV * NSA_D).astype(BF16)

    def pair_pos(p):
        ph = p.reshape(2, 16, 1, NSA_D)
        ph = jnp.broadcast_to(ph, (2, 16, N_KV, NSA_D)).reshape(2, 1, 16 * N_KV * NSA_D)
        return ph[0], ph[1]

    w1k = pair_w1(cmp_w1[0, 0]); w1v = pair_w1(cmp_w1[0, 1])
    w1a = jnp.stack([w1k[0], w1v[0]]); w1b = jnp.stack([w1k[1], w1v[1]])
    w2p = jnp.stack([pair_w2(cmp_w2[0, 0]), pair_w2(cmp_w2[0, 1])])
    pk = pair_pos(cmp_pos[0, 0]); pv = pair_pos(cmp_pos[0, 1])
    pa = jnp.stack([pk[0], pv[0]]); pb = jnp.stack([pk[1], pv[1]])
    b1p = jnp.stack([jnp.tile(cmp_b1[0, 0], 2)[None], jnp.tile(cmp_b1[0, 1], 2)[None]])
    hk = kcr.reshape(B, n_half, CMP_STRIDE * LANES)
    hv = vcr.reshape(B, n_half, CMP_STRIDE * LANES)
    hspec = pl.BlockSpec((1, n_half, CMP_STRIDE * LANES), lambda b: (b, 0, 0))
    kc, vct = pl.pallas_call(
        _cmp_kernel,
        grid=(B,),
        in_specs=[hspec, hspec, full((2, 1, 2048)), full((2, 1, 2048)), full((2, 2048, 512)),
                  full((2, 2048, 512)), full((2, 1, 512)), full((2, 512, LANES)), full((256, 256)),
                  full((1, LANES))],
        out_specs=[pl.BlockSpec((1, 2, n_half, LANES), lambda b: (b, 0, 0, 0)),
                   pl.BlockSpec((1, LANES, n_half), lambda b: (b, 0, 0))],
        out_shape=[jax.ShapeDtypeStruct((B, 2, n_half, LANES), BF16),
                   jax.ShapeDtypeStruct((B, LANES, n_half), BF16)],
        compiler_params=_params(("parallel",)),
    )(hk, hv, pa, pb, w1a, w1b, b1p, w2p, e_mat, gkc)

    slopes = jnp.asarray([2.0 ** (-8.0 * (h + 1) / N_DIFF_HEADS) * LOG2E for h in range(N_DIFF_HEADS)], F32)
    lam_v = jnp.concatenate([diff_lambda_q1, diff_lambda_k1, diff_lambda_q2, diff_lambda_k2], axis=0).astype(F32)
    qd_spec = pl.BlockSpec((TQ, LANES), lambda b, h, i: (b * nq + i, h))
    kd_spec = pl.BlockSpec((T, LANES), lambda b, h, i: (b, h))
    o_diff = pl.pallas_call(
        _diff_kernel,
        grid=(B, N_DIFF_HEADS, nq),
        in_specs=[pl.BlockSpec(memory_space=pltpu.SMEM), qd_spec, kd_spec, kd_spec, qd_spec,
                  pl.BlockSpec((4, DIFF_QK), lambda b, h, i: (0, 0)),
                  pl.BlockSpec((1, DIFF_V), lambda b, h, i: (0, 0))],
        out_specs=qd_spec,
        out_shape=bt(DIFF_W),
        scratch_shapes=[pltpu.VMEM((DIFF_ACC_ROWS, T), BF16), pltpu.VMEM((1, 2 * TQ), F32),
                        pltpu.VMEM((DIFF_ACC_ROWS, 2 * TQ), F32)],
        compiler_params=_params(("arbitrary", "arbitrary", "arbitrary")),
    )(slopes, dq, dk, dv, dz, lam_v, diff_subln_g[0][None])

    qn_spec = pl.BlockSpec((TQ, 512), lambda b, i, h: (b * nq + i, 0))
    kpair_spec = pl.BlockSpec((2, T, LANES), lambda b, i, h: (0, b, 0))
    vpair_spec = pl.BlockSpec((T, LANES), lambda b, i, h: (b, 0))
    o_nsa = pl.pallas_call(
        _nsa_kernel,
        grid=(B, nq, N_KV),
        in_specs=[qn_spec,
                  pl.BlockSpec((1, 1, n_half, LANES), lambda b, i, h: (b, h, 0, 0)),
                  pl.BlockSpec((1, LANES, n_half), lambda b, i, h: (b, 0, 0)),
                  kpair_spec, vpair_spec, kpair_spec, vpair_spec,
                  pl.BlockSpec((TQ, LANES), lambda b, i, h: (b * nq + i, 0)),
                  qn_spec,
                  pl.BlockSpec((1, N_GRP, LANES, LANES), lambda b, i, h: (h, 0, 0, 0)),
                  pl.BlockSpec((1, LANES, LANES), lambda b, i, h: (h, 0, 0))],
        out_specs=qn_spec,
        out_shape=bt(NSA_W),
        scratch_shapes=[pltpu.VMEM((LANES, T), BF16), pltpu.VMEM((LANES, T), BF16),
                        pltpu.VMEM((1, N_GRP * TQ), F32), pltpu.VMEM((2, LANES, N_GRP * TQ), F32)],
        compiler_params=_params(("arbitrary", "arbitrary", "arbitrary")),
    )(nqp, kc, vct, ks, vs, kw, vw, gates, nz, _query_features(), _overlap_t(T))

    out = pl.pallas_call(
        _out_kernel,
        grid=(n_rt,),
        in_specs=[row_spec(D), row_spec(DIFF_W), row_spec(NSA_W), full((D, D))],
        out_specs=row_spec(D),
        out_shape=jax.ShapeDtypeStruct((BT, D), x.dtype),
        compiler_params=_params(("parallel",)),
    )(x2, o_diff, o_nsa, w_o)
    return out.reshape(B, T, D)
```

```python
import math

import numpy as np
import jax
import jax.numpy as jnp
from jax import lax
from jax.experimental import pallas as pl
from jax.experimental.pallas import tpu as pltpu

F32 = jnp.float32
BF16 = jnp.bfloat16

D_MODEL = 1024
N_DIFF_HEADS = 4
DIFF_QK = 64
DIFF_V = 128
DIFF_W = 512
N_NSA_HEADS = 8
NSA_D = 64
N_KV = 2
N_GRP = 4
NSA_W = 512
CMP_BLOCK = 32
CMP_STRIDE = 16
CMP_HIDDEN = 256
SEL_BLOCK = 64
SEL_TOPK = 8
N_LOCAL = 2
FORCE_BONUS = 1000.0
WINDOW = 512
EPS = 1e-6
NEG_BIG = -1e30
LAM_INIT = 0.8 - 0.6 * math.exp(-0.3 * 0)
LOG2E = math.log2(math.e)

IN_MAIN = 3840
N_GATE = 24
IN_PAD = 3968
LANES = 128
SUBLANES = 8
SEL_MASK_BIAS = -32768.0

TM_PROJ = 512
TQ = 256
TK = 256
VMEM_LIMIT = 56 * 1024 * 1024
DIFF_ACC_ROWS = DIFF_V + SUBLANES
DIFF_HEADS_PER_STEP = 2


def _dot(a, b):
    return jnp.dot(a, b, preferred_element_type=F32)


def _sigmoid(y):
    return 1.0 / (1.0 + jnp.exp(-y))


def _seg_mean_sq(y, e):
    n = y.shape[1]
    y2 = (y * y).astype(BF16)
    if n == LANES:
        return _dot(y2, e[:LANES, :LANES])
    return jnp.concatenate([_dot(y2[:, c:c + 256], e) for c in range(0, n, 256)], axis=1)


def _seg_norm(y, e, gain):
    return y * lax.rsqrt(_seg_mean_sq(y, e) + EPS) * gain


def _transpose_bf16(a):
    return a.astype(F32).T.astype(BF16)


def _mask_cols(st, keep, width):
    n = st.shape[1]
    return jnp.concatenate([jnp.where(keep, st[:, a:a + width], -jnp.inf) for a in range(0, n, width)], axis=1)


def _online_softmax_step(st, vt_blk, m_ref, acc_ref, valid=None):
    m_prev = m_ref[...]
    cmax = jnp.max(st, axis=0, keepdims=True)
    if valid is not None:
        cmax = jnp.where(valid, cmax, -jnp.inf)
    m_new = jnp.maximum(m_prev, cmax)
    alpha = jnp.exp2(m_prev - m_new)
    sub = m_new if valid is None else jnp.where(valid, m_new, jnp.inf)
    p = jnp.exp2(st - sub).astype(BF16)
    acc_ref[...] = alpha * acc_ref[...] + _dot(vt_blk, p)
    m_ref[...] = m_new


def _pipelined_causal_sweep(i, qk, consume, st_a, st_b):
    qk(0, st_a)

    def body(jj, c):
        j = 2 * jj
        qk(j + 1, st_b)
        consume(j, st_a, False)
        qk(j + 2, st_a)
        consume(j + 1, st_b, False)
        return c

    lax.fori_loop(0, i // 2, body, 0)

    @pl.when(i % 2 == 0)
    def _():
        consume(i, st_a, True)

    @pl.when(i % 2 == 1)
    def _():
        qk(i, st_b)
        consume(i - 1, st_a, False)
        consume(i, st_b, True)


def _proj_kernel(x_ref, g_ref, w_ref, e_ref, gq_ref, gk_ref, gnq_ref, gks_ref, gkw_ref, fs_ref, fw_ref,
                 dq_ref, dk_ref, dv_ref, dz_ref, nq_ref, kc_ref, vc_ref, ks_ref, vs_ref, kw_ref, vw_ref,
                 nz_ref, gate_ref):
    x = x_ref[...]
    ms = jnp.mean(x * x, axis=-1, keepdims=True)
    h = (x * lax.rsqrt(ms + EPS) * g_ref[...]).astype(BF16)
    e = e_ref[...]

    def proj(a, b):
        return _dot(h, w_ref[:, a:b])

    dq_ref[...] = _seg_norm(proj(0, 512), e, gq_ref[...]).astype(BF16)
    dk_ref[...] = _seg_norm(proj(512, 1024), e, gk_ref[...]).astype(BF16)
    dv_ref[...] = proj(1024, 1536).astype(BF16)
    y = proj(1536, 2048)
    dz_ref[...] = (y * _sigmoid(y)).astype(BF16)
    nq_ref[...] = _seg_norm(proj(2048, 2560), e, gnq_ref[...]).astype(BF16)
    kc_ref[...] = proj(2560, 2688).astype(BF16)
    vc_ref[...] = proj(2688, 2816).astype(BF16)

    low = lax.broadcasted_iota(jnp.int32, (1, LANES), 1) < 64
    kp = _seg_norm(proj(2816, 2944), e, gks_ref[...]).astype(BF16)
    fs = fs_ref[...]
    ks_ref[0] = jnp.where(low, kp, fs)
    ks_ref[1] = jnp.where(low, fs, kp)
    vs_ref[...] = proj(2944, 3072).astype(BF16)
    kp = _seg_norm(proj(3072, 3200), e, gkw_ref[...]).astype(BF16)
    fw = fw_ref[...]
    kw_ref[0] = jnp.where(low, kp, fw)
    kw_ref[1] = jnp.where(low, fw, kp)
    vw_ref[...] = proj(3200, 3328).astype(BF16)
    y = proj(3328, 3840)
    nz_ref[...] = (y * _sigmoid(y)).astype(BF16)
    gate_ref[...] = _sigmoid(proj(3840, 3968))


def _cmp_kernel(hk_ref, hv_ref, pa_ref, pb_ref, w1a_ref, w1b_ref, b1_ref, w2_ref, e_ref, gkc_ref,
                kc_ref, vct_ref):
    n_rows = hk_ref.shape[1]
    row_ok = lax.broadcasted_iota(jnp.int32, (n_rows, 1), 0) < (n_rows - 1)
    low = lax.broadcasted_iota(jnp.int32, (1, LANES), 1) < 64

    def mlp(h_ref, idx):
        hf = h_ref[0].astype(F32)
        ha = (hf + pa_ref[idx]).astype(BF16)
        hb = (hf + pb_ref[idx]).astype(BF16)
        a = _dot(ha, w1a_ref[idx])
        b = _dot(hb, w1b_ref[idx])
        hid = a + pltpu.roll(b, n_rows - 1, 0) + b1_ref[idx]
        hid = hid * _sigmoid(hid)
        return _dot(hid.astype(BF16), w2_ref[idx])

    yk = mlp(hk_ref, 0)
    yk = _seg_norm(yk, e_ref[...], gkc_ref[...])
    yk = jnp.where(row_ok, yk, 0.0).astype(BF16)
    zero = jnp.zeros_like(yk)
    kc_ref[0, 0] = jnp.where(low, yk, zero)
    kc_ref[0, 1] = jnp.where(low, zero, yk)
    yv = mlp(hv_ref, 1)
    vct_ref[0] = jnp.where(row_ok, yv, 0.0).T.astype(BF16)


def _diff_kernel(slope_ref, q_ref, k_ref, v_ref, z_ref, lam_ref, gsub_ref, o_ref,
                 vt_ref, st_a, st_b, m_ref, acc_ref):
    hp = pl.program_id(1)
    i = pl.program_id(2)
    seq = k_ref.shape[0]
    heads = range(DIFF_HEADS_PER_STEP)

    @pl.when(i == 0)
    def _():
        ones_row = lax.broadcasted_iota(jnp.int32, (SUBLANES, seq), 0) == 0
        for hh in heads:
            vt_ref[hh, 0:DIFF_V, :] = _transpose_bf16(v_ref[:, hh * DIFF_V:(hh + 1) * DIFF_V])
            vt_ref[hh, DIFF_V:DIFF_ACC_ROWS, :] = jnp.where(ones_row, 1.0, 0.0).astype(BF16)

    rows = lax.broadcasted_iota(jnp.int32, (LANES, TQ), 0)
    qt2 = []
    for hh in heads:
        qt = _transpose_bf16(q_ref[:, hh * LANES:(hh + 1) * LANES])
        zero = jnp.zeros_like(qt)
        qt2.append(jnp.concatenate([jnp.where(rows < 64, qt, zero), jnp.where(rows >= 64, qt, zero)], axis=1))

    m_ref[...] = jnp.full(m_ref.shape, NEG_BIG, F32)
    acc_ref[...] = jnp.zeros(acc_ref.shape, F32)
    key_row = lax.broadcasted_iota(jnp.int32, (TK, LANES), 0)
    keep = (lax.broadcasted_iota(jnp.int32, (TK, TQ), 0) <= lax.broadcasted_iota(jnp.int32, (TK, TQ), 1))

    def qk(j, st_ref):
        k0 = pl.multiple_of(j * TK, TK)
        for hh in heads:
            st_ref[hh] = _dot(k_ref[pl.ds(k0, TK), hh * LANES:(hh + 1) * LANES], qt2[hh])

    def consume(j, st_ref, diagonal):
        k0 = pl.multiple_of(j * TK, TK)
        for hh in heads:
            bias = slope_ref[DIFF_HEADS_PER_STEP * hp + hh] * (k0 + key_row).astype(F32)
            st = st_ref[hh]
            st = jnp.concatenate([st[:, a:a + LANES] + bias for a in range(0, 2 * TQ, LANES)], axis=1)
            if diagonal:
                st = _mask_cols(st, keep, TQ)
            _online_softmax_step(st, vt_ref[hh, :, pl.ds(k0, TK)], m_ref.at[hh], acc_ref.at[hh])

    _pipelined_causal_sweep(i, qk, consume, st_a, st_b)

    lam_v = lam_ref[...]
    lam = (jnp.exp(jnp.sum(lam_v[0:1] * lam_v[1:2], axis=-1, keepdims=True))
           - jnp.exp(jnp.sum(lam_v[2:3] * lam_v[3:4], axis=-1, keepdims=True)) + LAM_INIT)
    for hh in heads:
        acc = acc_ref[hh]
        o2 = acc[0:DIFF_V] / acc[DIFF_V:DIFF_V + 1]
        o = (o2[:, :TQ] - lam * o2[:, TQ:]).T
        ms = jnp.mean(o * o, axis=-1, keepdims=True)
        o = o * lax.rsqrt(ms + EPS) * gsub_ref[...] * (1.0 - LAM_INIT)
        sl = slice(hh * DIFF_V, (hh + 1) * DIFF_V)
        o_ref[:, sl] = (o * z_ref[:, sl].astype(F32)).astype(BF16)


def _nsa_kernel(q_ref, kc_ref, vct_ref, ks_ref, vs_ref, kw_ref, vw_ref, gate_ref, nz_ref, qf_ref, ovt_ref,
                o_ref, vst_ref, vwt_ref, qsel_ref, st_a, st_b, m_ref, acc_ref):
    i = pl.program_id(1)
    kvh = pl.program_id(2)
    q0 = i * TQ
    n_all = N_GRP * TQ
    rows = lax.broadcasted_iota(jnp.int32, (LANES, TQ), 0)
    half = (rows >> 6) == kvh
    ones_row = rows == jnp.where(kvh == 0, 64, 0)

    @pl.when((i == 0) & (kvh == 0))
    def _():
        vst_ref[...] = _transpose_bf16(vs_ref[...])
        vwt_ref[...] = _transpose_bf16(vw_ref[...])

    def value_rows(vt_blk):
        return jnp.where(half, vt_blk, jnp.where(ones_row, 1.0, 0.0).astype(BF16))

    qts = []
    for g in range(N_GRP):
        qt = _transpose_bf16(q_ref[:, g * LANES:(g + 1) * LANES])
        feat = qf_ref[0, g]
        qts.append(jnp.where(half, qt, jnp.concatenate([feat] * (TQ // LANES), axis=1)))
    qb = jnp.concatenate(qts, axis=1)
    qpos = q0 + lax.broadcasted_iota(jnp.int32, (1, TQ), 1)

    st = _dot(kc_ref[0, 0], qb)
    crow = lax.broadcasted_iota(jnp.int32, (LANES, TQ), 0)
    valid = (crow * CMP_STRIDE + (CMP_BLOCK - 1)) <= qpos
    p_parts = []
    for g in range(N_GRP):
        sg = jnp.where(valid, st[:, g * TQ:(g + 1) * TQ], NEG_BIG)
        mx = jnp.max(sg, axis=0, keepdims=True)
        ex = jnp.where(valid, jnp.exp2(sg - mx), 0.0)
        den = jnp.sum(ex, axis=0, keepdims=True)
        p_parts.append(ex * jnp.where(den > 0.0, 1.0 / den, 0.0))
    pt = jnp.concatenate(p_parts, axis=1)
    o_cmp = _dot(vct_ref[0], pt.astype(BF16))
    psum = p_parts[0] + p_parts[1] + p_parts[2] + p_parts[3]

    ovt = ovt_ref[0]
    p_hi = psum.astype(BF16)
    p_lo = (psum - p_hi.astype(F32)).astype(BF16)
    imp = _dot(ovt, p_hi) + _dot(ovt, p_lo)
    bidx = crow - (96 - 64 * kvh)
    in_seg = (bidx >= 0) & (bidx < 32)
    blk_t = qpos >> 6
    valid_b = in_seg & (bidx <= blk_t)
    forced = valid_b & ((bidx == 0) | (bidx >= blk_t - (N_LOCAL - 1)))
    score = jnp.where(valid_b, imp, -1.0) + jnp.where(forced, FORCE_BONUS, 0.0)
    score = jnp.where(in_seg, score, -jnp.inf)
    row_f = crow.astype(F32)
    chosen = jnp.zeros(score.shape, F32)
    for _ in range(SEL_TOPK):
        best = jnp.max(score, axis=0, keepdims=True)
        first = jnp.min(jnp.where(score == best, row_f, 1e9), axis=0, keepdims=True)
        pick = row_f == first
        chosen = jnp.where(pick, 1.0, chosen)
        score = jnp.where(pick, -jnp.inf, score)
    sel_bias = jnp.where(in_seg & (chosen == 0.0), SEL_MASK_BIAS, 0.0).astype(BF16)
    qsel = jnp.concatenate([jnp.where(in_seg, sel_bias, qt_) for qt_ in qts], axis=1)

    krow = lax.broadcasted_iota(jnp.int32, (TK, TQ), 0)
    qcol = lax.broadcasted_iota(jnp.int32, (TK, TQ), 1)

    m_ref[...] = jnp.full(m_ref.shape, NEG_BIG, F32)
    acc_ref[...] = jnp.zeros(acc_ref.shape, F32)

    def win_block(j, keep, valid):
        k0 = pl.multiple_of(jnp.maximum(j, 0) * TK, TK)
        st = _dot(kw_ref[kvh, pl.ds(k0, TK), :], qb)
        if keep is not None:
            st = _mask_cols(st, keep, TQ)
        _online_softmax_step(st, value_rows(vwt_ref[:, pl.ds(k0, TK)]), m_ref.at[1], acc_ref.at[1], valid)

    win_block(i - 2, krow > qcol, i >= 2)
    win_block(i - 1, None, i >= 1)
    win_block(i, krow <= qcol, None)

    qsel_ref[...] = qsel

    def sel_qk(j, st_ref):
        k0 = pl.multiple_of(j * TK, TK)
        st_ref[...] = _dot(ks_ref[kvh, pl.ds(k0, TK), :], qsel_ref[...])

    def sel_consume(j, st_ref, diagonal):
        k0 = pl.multiple_of(j * TK, TK)
        st = st_ref[...]
        if diagonal:
            st = _mask_cols(st, krow <= qcol, TQ)
        _online_softmax_step(st, value_rows(vst_ref[:, pl.ds(k0, TK)]), m_ref.at[0], acc_ref.at[0])

    _pipelined_causal_sweep(i, sel_qk, sel_consume, st_a, st_b)

    gt = gate_ref[...].T

    def gate_row(br, g):
        r0 = br * 8 + g
        r1 = br * 8 + 4 + g
        return jnp.where(kvh == 0, gt[r0:r0 + 1], gt[r1:r1 + 1])

    def norm_row(a):
        return jnp.where(kvh == 0, a[64:65], a[0:1])

    for g in range(N_GRP):
        cs = slice(g * TQ, (g + 1) * TQ)
        a_s = acc_ref[0, :, cs]
        a_w = acc_ref[1, :, cs]
        ot = (gate_row(0, g) * o_cmp[:, cs]
              + (gate_row(1, g) / norm_row(a_s)) * a_s
              + (gate_row(2, g) / norm_row(a_w)) * a_w)
        og = jnp.where(half, ot, 0.0).T
        val = og * nz_ref[:, g * LANES:(g + 1) * LANES].astype(F32)
        sl = slice(g * LANES, (g + 1) * LANES)

        @pl.when(kvh == 0)
        def _():
            o_ref[:, sl] = val.astype(BF16)

        @pl.when(kvh != 0)
        def _():
            o_ref[:, sl] = (o_ref[:, sl].astype(F32) + val).astype(BF16)


def _out_kernel(x_ref, od_ref, on_ref, w_ref, o_ref):
    acc = _dot(od_ref[...], w_ref[:DIFF_W, :]) + _dot(on_ref[...], w_ref[DIFF_W:, :])
    o_ref[...] = x_ref[...] + acc


def _params(sem):
    return pltpu.CompilerParams(dimension_semantics=sem, vmem_limit_bytes=VMEM_LIMIT)


def _head_perm():
    new = np.arange(NSA_W).reshape(N_GRP, N_KV, NSA_D)
    g, kvh, d = np.meshgrid(np.arange(N_GRP), np.arange(N_KV), np.arange(NSA_D), indexing="ij")
    old = kvh * (N_GRP * NSA_D) + g * NSA_D + d
    perm = np.zeros(NSA_W, np.int64)
    perm[new.reshape(-1)] = old.reshape(-1)
    return perm


def _bf16_pieces(v):
    out = []
    r = np.float32(v)
    for _ in range(3):
        p = np.asarray(r).astype(BF16).astype(np.float32)
        out.append(float(p))
        r = np.float32(r - p)
    return out


def _key_features(seq, with_blocks):
    t = np.arange(seq)
    f = np.zeros((seq, 64), np.float32)
    for r in range(3):
        f[:, 2 * r] = t // 256
        f[:, 2 * r + 1] = t % 256
    if with_blocks:
        f[t, 32 + t // SEL_BLOCK] = 1.0
    return jnp.asarray(np.concatenate([f, f], axis=1), dtype=BF16)


def _query_features():
    f = np.zeros((N_KV, N_GRP, LANES), np.float32)
    for kvh in range(N_KV):
        base = 64 if kvh == 0 else 0
        for g in range(N_GRP):
            slope = 2.0 ** (-(kvh * N_GRP + g + 1))
            for r, piece in enumerate(_bf16_pieces(slope * LOG2E)):
                f[kvh, g, base + 2 * r] = piece * 256.0
                f[kvh, g, base + 2 * r + 1] = piece
    return jnp.asarray(np.repeat(f[..., None], LANES, axis=-1), dtype=BF16)


def _overlap_t(seq):
    n_cmp = (seq - CMP_BLOCK) // CMP_STRIDE + 1
    n_sb = seq // SEL_BLOCK
    cs = np.arange(n_cmp) * CMP_STRIDE
    ss = np.arange(n_sb) * SEL_BLOCK
    ovl = ((cs[:, None] < ss[None, :] + SEL_BLOCK) & (cs[:, None] + CMP_BLOCK > ss[None, :])).astype(np.float32)
    out = np.zeros((N_KV, LANES, LANES), np.float32)
    out[0, 96:96 + n_sb, :n_cmp] = ovl.T
    out[1, 32:32 + n_sb, :n_cmp] = ovl.T
    return jnp.asarray(out, dtype=BF16)


def kernel(x, norm_g, w_in, diff_q_norm_g, diff_k_norm_g, diff_lambda_q1, diff_lambda_k1, diff_lambda_q2,
           diff_lambda_k2, diff_subln_g, nsa_q_norm_g, nsa_k_norm_g, cmp_pos, cmp_w1, cmp_b1, cmp_w2, w_out):
    B, T, D = x.shape
    BT = B * T
    assert D == D_MODEL and T % TQ == 0 and T == 2048 and BT % TM_PROJ == 0 and T % TM_PROJ == 0
    n_half = T // CMP_STRIDE
    nq = T // TQ
    perm = _head_perm()

    w = w_in[0]
    col = np.arange(IN_MAIN)
    col[2048:2560] = 2048 + perm
    col[3328:3840] = 3328 + perm
    w_p = jnp.concatenate([w[:, col], w[:, IN_MAIN:], jnp.zeros((D, IN_PAD - IN_MAIN - N_GATE), w.dtype)],
                          axis=1).astype(BF16)
    w_o = jnp.concatenate([w_out[0][:DIFF_W], w_out[0][DIFF_W:][perm]], axis=0).astype(BF16)
    seg = np.arange(256) // 64
    e_mat = jnp.asarray((seg[:, None] == seg[None, :]).astype(np.float32) / 64.0, dtype=BF16)
    gq = (jnp.tile(diff_q_norm_g[0], 8) * (DIFF_QK ** -0.5 * LOG2E))[None]
    gk = jnp.tile(diff_k_norm_g[0], 8)[None]
    gnq = (jnp.tile(nsa_q_norm_g[0], 8) * (NSA_D ** -0.5 * LOG2E))[None]
    gkc = jnp.tile(nsa_k_norm_g[0, 0], 2)[None]
    gks = jnp.tile(nsa_k_norm_g[0, 1], 2)[None]
    gkw = jnp.tile(nsa_k_norm_g[0, 2], 2)[None]
    feat_s = _key_features(T, True)
    feat_w = _key_features(T, False)

    x2 = x.reshape(BT, D)
    n_rt = BT // TM_PROJ
    rt_per_seq = T // TM_PROJ
    row_spec = lambda n: pl.BlockSpec((TM_PROJ, n), lambda i: (i, 0))
    full = lambda shp: pl.BlockSpec(shp, lambda i: (0,) * len(shp))
    feat_spec = pl.BlockSpec((TM_PROJ, LANES), lambda i: (i % rt_per_seq, 0))
    pair_spec = pl.BlockSpec((2, TM_PROJ, LANES), lambda i: (0, i, 0))
    bt = lambda n, dt=BF16: jax.ShapeDtypeStruct((BT, n), dt)
    outs = pl.pallas_call(
        _proj_kernel,
        grid=(n_rt,),
        in_specs=[row_spec(D), full((1, D)), full((D, IN_PAD)), full((256, 256)),
                  full((1, 512)), full((1, 512)), full((1, 512)), full((1, LANES)), full((1, LANES)),
                  feat_spec, feat_spec],
        out_specs=[row_spec(512), row_spec(512), row_spec(512), row_spec(512), row_spec(512),
                   row_spec(LANES), row_spec(LANES), pair_spec, row_spec(LANES), pair_spec, row_spec(LANES),
                   row_spec(512), row_spec(LANES)],
        out_shape=[bt(512), bt(512), bt(512), bt(512), bt(512), bt(LANES), bt(LANES),
                   jax.ShapeDtypeStruct((2, BT, LANES), BF16), bt(LANES),
                   jax.ShapeDtypeStruct((2, BT, LANES), BF16), bt(LANES), bt(512), bt(LANES, F32)],
        compiler_params=_params(("parallel",)),
    )(x2, norm_g[0][None], w_p, e_mat, gq, gk, gnq, gks, gkw, feat_s, feat_w)
    dq, dk, dv, dz, nqp, kcr, vcr, ks, vs, kw, vw, nz, gates = outs

    def pair_w1(w1):
        wh = w1.reshape(2, 16, NSA_D, CMP_HIDDEN)
        z = jnp.zeros((2, 16, N_KV, NSA_D, N_KV, CMP_HIDDEN), w1.dtype)
        z = z.at[:, :, 0, :, 0, :].set(wh).at[:, :, 1, :, 1, :].set(wh)
        z = z.reshape(2, 16 * N_KV * NSA_D, N_KV * CMP_HIDDEN).astype(BF16)
        return z[0], z[1]

    def pair_w2(w2):
        z = jnp.zeros((N_KV, CMP_HIDDEN, N_KV, NSA_D), w2.dtype)
        z = z.at[0, :, 0, :].set(w2).at[1, :, 1, :].set(w2)
        return z.reshape(N_KV * CMP_HIDDEN, N_KV * NSA_D).astype(BF16)

    def pair_pos(p):
        ph = p.reshape(2, 16, 1, NSA_D)
        ph = jnp.broadcast_to(ph, (2, 16, N_KV, NSA_D)).reshape(2, 1, 16 * N_KV * NSA_D)
        return ph[0], ph[1]

    w1k = pair_w1(cmp_w1[0, 0]); w1v = pair_w1(cmp_w1[0, 1])
    w1a = jnp.stack([w1k[0], w1v[0]]); w1b = jnp.stack([w1k[1], w1v[1]])
    w2p = jnp.stack([pair_w2(cmp_w2[0, 0]), pair_w2(cmp_w2[0, 1])])
    pk = pair_pos(cmp_pos[0, 0]); pv = pair_pos(cmp_pos[0, 1])
    pa = jnp.stack([pk[0], pv[0]]); pb = jnp.stack([pk[1], pv[1]])
    b1p = jnp.stack([jnp.tile(cmp_b1[0, 0], 2)[None], jnp.tile(cmp_b1[0, 1], 2)[None]])
    hk = kcr.reshape(B, n_half, CMP_STRIDE * LANES)
    hv = vcr.reshape(B, n_half, CMP_STRIDE * LANES)
    hspec = pl.BlockSpec((1, n_half, CMP_STRIDE * LANES), lambda b: (b, 0, 0))
    kc, vct = pl.pallas_call(
        _cmp_kernel,
        grid=(B,),
        in_specs=[hspec, hspec, full((2, 1, 2048)), full((2, 1, 2048)), full((2, 2048, 512)),
                  full((2, 2048, 512)), full((2, 1, 512)), full((2, 512, LANES)), full((256, 256)),
                  full((1, LANES))],
        out_specs=[pl.BlockSpec((1, 2, n_half, LANES), lambda b: (b, 0, 0, 0)),
                   pl.BlockSpec((1, LANES, n_half), lambda b: (b, 0, 0))],
        out_shape=[jax.ShapeDtypeStruct((B, 2, n_half, LANES), BF16),
                   jax.ShapeDtypeStruct((B, LANES, n_half), BF16)],
        compiler_params=_params(("parallel",)),
    )(hk, hv, pa, pb, w1a, w1b, b1p, w2p, e_mat, gkc)

    slopes = jnp.asarray([2.0 ** (-8.0 * (h + 1) / N_DIFF_HEADS) * LOG2E for h in range(N_DIFF_HEADS)], F32)
    lam_v = jnp.concatenate([diff_lambda_q1, diff_lambda_k1, diff_lambda_q2, diff_lambda_k2], axis=0).astype(F32)
    hps = DIFF_HEADS_PER_STEP
    qd_spec = pl.BlockSpec((TQ, hps * LANES), lambda b, h, i: (b * nq + i, h))
    kd_spec = pl.BlockSpec((T, hps * LANES), lambda b, h, i: (b, h))
    st_scratch = pltpu.VMEM((hps, TK, 2 * TQ), F32)
    o_diff = pl.pallas_call(
        _diff_kernel,
        grid=(B, N_DIFF_HEADS // hps, nq),
        in_specs=[pl.BlockSpec(memory_space=pltpu.SMEM), qd_spec, kd_spec, kd_spec, qd_spec,
                  pl.BlockSpec((4, DIFF_QK), lambda b, h, i: (0, 0)),
                  pl.BlockSpec((1, DIFF_V), lambda b, h, i: (0, 0))],
        out_specs=qd_spec,
        out_shape=bt(DIFF_W),
        scratch_shapes=[pltpu.VMEM((hps, DIFF_ACC_ROWS, T), BF16), st_scratch, st_scratch,
                        pltpu.VMEM((hps, 1, 2 * TQ), F32), pltpu.VMEM((hps, DIFF_ACC_ROWS, 2 * TQ), F32)],
        compiler_params=_params(("arbitrary", "arbitrary", "arbitrary")),
    )(slopes, dq, dk, dv, dz, lam_v, diff_subln_g[0][None])

    qn_spec = pl.BlockSpec((TQ, 512), lambda b, i, h: (b * nq + i, 0))
    kpair_spec = pl.BlockSpec((2, T, LANES), lambda b, i, h: (0, b, 0))
    vpair_spec = pl.BlockSpec((T, LANES), lambda b, i, h: (b, 0))
    o_nsa = pl.pallas_call(
        _nsa_kernel,
        grid=(B, nq, N_KV),
        in_specs=[qn_spec,
                  pl.BlockSpec((1, 1, n_half, LANES), lambda b, i, h: (b, h, 0, 0)),
                  pl.BlockSpec((1, LANES, n_half), lambda b, i, h: (b, 0, 0)),
                  kpair_spec, vpair_spec, kpair_spec, vpair_spec,
                  pl.BlockSpec((TQ, LANES), lambda b, i, h: (b * nq + i, 0)),
                  qn_spec,
                  pl.BlockSpec((1, N_GRP, LANES, LANES), lambda b, i, h: (h, 0, 0, 0)),
                  pl.BlockSpec((1, LANES, LANES), lambda b, i, h: (h, 0, 0))],
        out_specs=qn_spec,
        out_shape=bt(NSA_W),
        scratch_shapes=[pltpu.VMEM((LANES, T), BF16), pltpu.VMEM((LANES, T), BF16),
                        pltpu.VMEM((LANES, N_GRP * TQ), BF16),
                        pltpu.VMEM((TK, N_GRP * TQ), F32), pltpu.VMEM((TK, N_GRP * TQ), F32),
                        pltpu.VMEM((2, 1, N_GRP * TQ), F32), pltpu.VMEM((2, LANES, N_GRP * TQ), F32)],
        compiler_params=_params(("arbitrary", "arbitrary", "arbitrary")),
    )(nqp, kc, vct, ks, vs, kw, vw, gates, nz, _query_features(), _overlap_t(T))

    out = pl.pallas_call(
        _out_kernel,
        grid=(n_rt,),
        in_specs=[row_spec(D), row_spec(DIFF_W), row_spec(NSA_W), full((D, D))],
        out_specs=row_spec(D),
        out_shape=jax.ShapeDtypeStruct((BT, D), x.dtype),
        compiler_params=_params(("parallel",)),
    )(x2, o_diff, o_nsa, w_o)
    return out.reshape(B, T, D)
```

```python
import math

import numpy as np
import jax
import jax.numpy as jnp
from jax import lax
from jax.experimental import pallas as pl
from jax.experimental.pallas import tpu as pltpu

F32 = jnp.float32
BF16 = jnp.bfloat16

D_MODEL = 1024
N_DIFF_HEADS = 4
DIFF_QK = 64
DIFF_V = 128
DIFF_W = 512
N_NSA_HEADS = 8
NSA_D = 64
N_KV = 2
N_GRP = 4
NSA_W = 512
CMP_BLOCK = 32
CMP_STRIDE = 16
CMP_HIDDEN = 256
SEL_BLOCK = 64
N_SEL_BLOCKS = 32
SEL_TOPK = 8
N_LOCAL = 2
FORCE_BONUS = 1000.0
WINDOW = 512
EPS = 1e-6
NEG_BIG = -1e30
LAM_INIT = 0.8 - 0.6 * math.exp(-0.3 * 0)
LOG2E = math.log2(math.e)

IN_MAIN = 3840
N_GATE = 24
IN_PAD = 3968
LANES = 128
SUBLANES = 8
BF16_ROWS = 16
SEL_MASK_BIAS = -32768.0

TM_PROJ = 512
TQ = 256
TK = 256
VMEM_LIMIT = 56 * 1024 * 1024
DIFF_ACC_ROWS = DIFF_V + SUBLANES
DIFF_VT_ROWS = DIFF_V + BF16_ROWS
DIFF_HEADS_PER_STEP = 2
NSA_ACC_ROWS = NSA_D + SUBLANES
NSA_VT_ROWS = NSA_D + BF16_ROWS


def _dot(a, b):
    return jnp.dot(a, b, preferred_element_type=F32)


def _sigmoid(y):
    return 1.0 / (1.0 + jnp.exp(-y))


def _seg_mean_sq(y, e):
    n = y.shape[1]
    y2 = (y * y).astype(BF16)
    if n == LANES:
        return _dot(y2, e[:LANES, :LANES])
    return jnp.concatenate([_dot(y2[:, c:c + 256], e) for c in range(0, n, 256)], axis=1)


def _seg_norm(y, e, gain):
    return y * lax.rsqrt(_seg_mean_sq(y, e) + EPS) * gain


def _transpose_bf16(a):
    return a.astype(F32).T.astype(BF16)


def _ones_rows(n_rows, width):
    return jnp.where(lax.broadcasted_iota(jnp.int32, (n_rows, width), 0) == 0, 1.0, 0.0).astype(BF16)


def _mask_cols(st, keep, width):
    n = st.shape[1]
    return jnp.concatenate([jnp.where(keep, st[:, a:a + width], -jnp.inf) for a in range(0, n, width)], axis=1)


def _online_softmax_step(st, vt_blk, m_ref, acc_ref):
    m_prev = m_ref[...]
    m_new = jnp.maximum(m_prev, jnp.max(st, axis=0, keepdims=True))
    alpha = jnp.exp2(m_prev - m_new)
    p = jnp.exp2(st - m_new).astype(BF16)
    pv = _dot(vt_blk, p)
    acc_ref[...] = alpha * acc_ref[...] + pv[:acc_ref.shape[0]]
    m_ref[...] = m_new


def _pipelined_sweep(n, qk, consume, st_a, st_b):
    qk(0, st_a)

    def body(jj, c):
        j = 2 * jj
        qk(j + 1, st_b)
        consume(j, st_a, False)
        qk(j + 2, st_a)
        consume(j + 1, st_b, False)
        return c

    lax.fori_loop(0, n // 2, body, 0)

    @pl.when(n % 2 == 0)
    def _():
        consume(n, st_a, True)

    @pl.when(n % 2 == 1)
    def _():
        qk(n, st_b)
        consume(n - 1, st_a, False)
        consume(n, st_b, True)


def _proj_kernel(x_ref, g_ref, w_ref, e_ref, gq_ref, gk_ref, gnq_ref, gks_ref, gkw_ref, fs_ref, fw_ref,
                 dq_ref, dk_ref, dv_ref, dz_ref, nq_ref, kc_ref, vc_ref, kk_ref, vs_ref, vw_ref,
                 nz_ref, gate_ref):
    x = x_ref[...]
    ms = jnp.mean(x * x, axis=-1, keepdims=True)
    h = (x * lax.rsqrt(ms + EPS) * g_ref[...]).astype(BF16)
    e = e_ref[...]

    def proj(a, b):
        return _dot(h, w_ref[:, a:b])

    dq_ref[...] = _seg_norm(proj(0, 512), e, gq_ref[...]).astype(BF16)
    dk_ref[...] = _seg_norm(proj(512, 1024), e, gk_ref[...]).astype(BF16)
    dv_ref[...] = proj(1024, 1536).astype(BF16)
    y = proj(1536, 2048)
    dz_ref[...] = (y * _sigmoid(y)).astype(BF16)
    nq_ref[...] = _seg_norm(proj(2048, 2560), e, gnq_ref[...]).astype(BF16)
    kc_ref[...] = proj(2560, 2688)
    vc_ref[...] = proj(2688, 2816)

    low = lax.broadcasted_iota(jnp.int32, (1, LANES), 1) < 64
    kp = _seg_norm(proj(2816, 2944), e, gks_ref[...]).astype(BF16)
    fs = fs_ref[...]
    kk_ref[0] = jnp.where(low, kp, fs)
    kk_ref[1] = jnp.where(low, fs, kp)
    vs_ref[...] = proj(2944, 3072).astype(BF16)
    kp = _seg_norm(proj(3072, 3200), e, gkw_ref[...]).astype(BF16)
    fw = fw_ref[...]
    kk_ref[2] = jnp.where(low, kp, fw)
    kk_ref[3] = jnp.where(low, fw, kp)
    vw_ref[...] = proj(3200, 3328).astype(BF16)
    y = proj(3328, 3840)
    nz_ref[...] = (y * _sigmoid(y)).astype(BF16)
    gate_ref[...] = _sigmoid(proj(3840, 3968))


def _cmp_kernel(hk_ref, hv_ref, pa_ref, pb_ref, w1a_ref, w1b_ref, b1_ref, w2_ref, e_ref, gkc_ref,
                kc_ref, vct_ref):
    n_rows = hk_ref.shape[0] // CMP_STRIDE
    row_ok = lax.broadcasted_iota(jnp.int32, (n_rows, 1), 0) < (n_rows - 1)
    low = lax.broadcasted_iota(jnp.int32, (1, LANES), 1) < 64

    def mlp(h_ref, idx):
        hf = jnp.concatenate([h_ref[pl.ds(t, n_rows, stride=CMP_STRIDE), :] for t in range(CMP_STRIDE)], axis=1)
        ha = (hf + pa_ref[idx]).astype(BF16)
        hb = (hf + pb_ref[idx]).astype(BF16)
        a = _dot(ha, w1a_ref[idx])
        b = _dot(hb, w1b_ref[idx])
        hid = a + pltpu.roll(b, n_rows - 1, 0) + b1_ref[idx]
        hid = hid * _sigmoid(hid)
        return _dot(hid.astype(BF16), w2_ref[idx])

    yk = mlp(hk_ref, 0)
    yk = _seg_norm(yk, e_ref[...], gkc_ref[...])
    yk = jnp.where(row_ok, yk, 0.0).astype(BF16)
    zero = jnp.zeros_like(yk)
    kc_ref[0, 0] = jnp.where(low, yk, zero)
    kc_ref[0, 1] = jnp.where(low, zero, yk)
    yvt = jnp.where(row_ok, mlp(hv_ref, 1), 0.0).T.astype(BF16)
    vct_ref[0, 0] = yvt[:NSA_D]
    vct_ref[0, 1] = yvt[NSA_D:]


def _diff_kernel(q_ref, k_ref, v_ref, z_ref, fk_ref, qf_ref, lam_ref, gsub_ref, o_ref,
                 vt_ref, qq_ref, st_a, st_b, m_ref, acc_ref):
    i = pl.program_id(2)
    seq = k_ref.shape[0]
    heads = range(DIFF_HEADS_PER_STEP)

    @pl.when(i == 0)
    def _():
        for hh in heads:
            vt_ref[hh, 0:DIFF_V, :] = _transpose_bf16(v_ref[:, hh * DIFF_V:(hh + 1) * DIFF_V])
            vt_ref[hh, DIFF_V:DIFF_VT_ROWS, :] = _ones_rows(BF16_ROWS, seq)

    rows = lax.broadcasted_iota(jnp.int32, (LANES, TQ), 0)
    for hh in heads:
        qt = _transpose_bf16(q_ref[:, hh * LANES:(hh + 1) * LANES])
        zero = jnp.zeros_like(qt)
        qq_ref[hh, 0:LANES, :] = jnp.concatenate(
            [jnp.where(rows < 64, qt, zero), jnp.where(rows >= 64, qt, zero)], axis=1)
        qq_ref[hh, LANES:2 * LANES, :] = jnp.concatenate([qf_ref[hh]] * (2 * TQ // LANES), axis=1)

    m_ref[...] = jnp.full(m_ref.shape, NEG_BIG, F32)
    acc_ref[...] = jnp.zeros(acc_ref.shape, F32)
    keep = (lax.broadcasted_iota(jnp.int32, (TK, TQ), 0) <= lax.broadcasted_iota(jnp.int32, (TK, TQ), 1))

    def qk(j, st_ref):
        k0 = pl.multiple_of(j * TK, TK)
        feat = fk_ref[pl.ds(k0, TK), :]
        for hh in heads:
            keys = jnp.concatenate([k_ref[pl.ds(k0, TK), hh * LANES:(hh + 1) * LANES], feat], axis=1)
            st_ref[hh] = _dot(keys, qq_ref[hh])

    def consume(j, st_ref, diagonal):
        k0 = pl.multiple_of(j * TK, TK)
        for hh in heads:
            st = st_ref[hh]
            if diagonal:
                st = _mask_cols(st, keep, TQ)
            _online_softmax_step(st, vt_ref[hh, :, pl.ds(k0, TK)], m_ref.at[hh], acc_ref.at[hh])

    _pipelined_sweep(i, qk, consume, st_a, st_b)

    lam_v = lam_ref[...]
    lam = (jnp.exp(jnp.sum(lam_v[0:1] * lam_v[1:2], axis=-1, keepdims=True))
           - jnp.exp(jnp.sum(lam_v[2:3] * lam_v[3:4], axis=-1, keepdims=True)) + LAM_INIT)
    for hh in heads:
        acc = acc_ref[hh]
        o2 = acc[0:DIFF_V] / acc[DIFF_V:DIFF_V + 1]
        o = (o2[:, :TQ] - lam * o2[:, TQ:]).T
        ms = jnp.mean(o * o, axis=-1, keepdims=True)
        o = o * lax.rsqrt(ms + EPS) * gsub_ref[...] * (1.0 - LAM_INIT)
        sl = slice(hh * DIFF_V, (hh + 1) * DIFF_V)
        o_ref[:, sl] = (o * z_ref[:, sl].astype(F32)).astype(BF16)


def _nsa_kernel(q_ref, kc_ref, vct_ref, kk_ref, vs_ref, vw_ref, gate_ref, nz_ref, qf_ref, ovt_ref,
                o_ref, vt_ref, qq_ref, mask_ref, st_a, st_b, m_ref, acc_ref):
    i = pl.program_id(1)
    seq = vs_ref.shape[0]
    q0 = i * TQ
    SEL, WIN = 0, 1
    NO_MASK, CAUSAL, UPPER = 0, 1, 2
    kv_heads = range(N_KV)
    krow = lax.broadcasted_iota(jnp.int32, (TK, TQ), 0)
    qcol = lax.broadcasted_iota(jnp.int32, (TK, TQ), 1)

    @pl.when(i == 0)
    def _():
        for br, v_ref in ((SEL, vs_ref), (WIN, vw_ref)):
            vt = _transpose_bf16(v_ref[...])
            for kvh in kv_heads:
                vt_ref[br, kvh, 0:NSA_D, :] = vt[kvh * NSA_D:(kvh + 1) * NSA_D]
                vt_ref[br, kvh, NSA_D:NSA_VT_ROWS, :] = _ones_rows(BF16_ROWS, seq)
        mask_ref[NO_MASK] = jnp.zeros((TK, TQ), F32)
        mask_ref[CAUSAL] = jnp.where(krow <= qcol, 0.0, -jnp.inf)
        mask_ref[UPPER] = jnp.where(krow > qcol, 0.0, -jnp.inf)

    rows = lax.broadcasted_iota(jnp.int32, (LANES, TQ), 0)
    qts = [_transpose_bf16(q_ref[:, g * LANES:(g + 1) * LANES]) for g in range(N_GRP)]
    qbs = []
    for kvh in kv_heads:
        half = (rows < 64) if kvh == 0 else (rows >= 64)
        qb = jnp.concatenate(
            [jnp.where(half, qts[g], jnp.concatenate([qf_ref[kvh, g]] * (TQ // LANES), axis=1))
             for g in range(N_GRP)], axis=1)
        qq_ref[WIN, kvh] = qb
        qq_ref[SEL, kvh] = qb
        qbs.append(qb)
    qpos = q0 + lax.broadcasted_iota(jnp.int32, (1, TQ), 1)

    m_ref[...] = jnp.full(m_ref.shape, NEG_BIG, F32)
    acc_ref[...] = jnp.zeros(acc_ref.shape, F32)

    valid = (rows * CMP_STRIDE + (CMP_BLOCK - 1)) <= qpos
    o_cmp, scores = [], []
    bidx = lax.broadcasted_iota(jnp.int32, (N_SEL_BLOCKS, TQ), 0)
    blk_t = qpos >> 6
    valid_b = bidx <= blk_t
    forced = valid_b & ((bidx == 0) | (bidx >= blk_t - (N_LOCAL - 1)))
    for kvh in kv_heads:
        st = _dot(kc_ref[0, kvh], qbs[kvh])
        p_parts = []
        for g in range(N_GRP):
            sg = jnp.where(valid, st[:, g * TQ:(g + 1) * TQ], NEG_BIG)
            mx = jnp.max(sg, axis=0, keepdims=True)
            ex = jnp.where(valid, jnp.exp2(sg - mx), 0.0)
            den = jnp.sum(ex, axis=0, keepdims=True)
            p_parts.append(ex * jnp.where(den > 0.0, 1.0 / den, 0.0))
        pt = jnp.concatenate(p_parts, axis=1)
        o_cmp.append(_dot(vct_ref[0, kvh], pt.astype(BF16)))
        psum = p_parts[0] + p_parts[1] + p_parts[2] + p_parts[3]
        ovt = ovt_ref[kvh]
        p_hi = psum.astype(BF16)
        p_lo = (psum - p_hi.astype(F32)).astype(BF16)
        seg0 = 96 - 64 * kvh
        imp = (_dot(ovt, p_hi) + _dot(ovt, p_lo))[seg0:seg0 + N_SEL_BLOCKS]
        scores.append(jnp.where(valid_b, imp, -1.0) + jnp.where(forced, FORCE_BONUS, 0.0))

    row_f = bidx.astype(F32)
    chosen = [jnp.zeros((N_SEL_BLOCKS, TQ), F32) for _ in kv_heads]
    for _ in range(SEL_TOPK):
        for kvh in kv_heads:
            best = jnp.max(scores[kvh], axis=0, keepdims=True)
            first = jnp.min(jnp.where(scores[kvh] == best, row_f, 1e9), axis=0, keepdims=True)
            pick = row_f == first
            chosen[kvh] = jnp.where(pick, 1.0, chosen[kvh])
            scores[kvh] = jnp.where(pick, -jnp.inf, scores[kvh])
    for kvh in kv_heads:
        sel_bias = jnp.where(chosen[kvh] == 0.0, SEL_MASK_BIAS, 0.0).astype(BF16)
        seg0 = 96 - 64 * kvh
        for g in range(N_GRP):
            qq_ref[SEL, kvh, seg0:seg0 + N_SEL_BLOCKS, g * TQ:(g + 1) * TQ] = sel_bias

    v0 = 2 - jnp.minimum(i, 2)

    def source(u):
        v = u + v0
        is_win = v < 3
        src = jnp.where(is_win, WIN, SEL)
        kb = jnp.where(is_win, i - 2 + v, v - 3)
        mt = jnp.where(is_win, jnp.where(v == 0, UPPER, jnp.where(v == 2, CAUSAL, NO_MASK)),
                       jnp.where(kb == i, CAUSAL, NO_MASK))
        return src, pl.multiple_of(kb * TK, TK), mt

    def qk(u, st_ref):
        src, k0, _ = source(u)
        for kvh in kv_heads:
            st_ref[kvh] = _dot(kk_ref[2 * src + kvh, pl.ds(k0, TK), :], qq_ref[src, kvh])

    def consume(u, st_ref, _last):
        src, k0, mt = source(u)
        mask = mask_ref[mt]
        for kvh in kv_heads:
            st = st_ref[kvh]
            st = jnp.concatenate([st[:, a:a + TQ] + mask for a in range(0, N_GRP * TQ, TQ)], axis=1)
            _online_softmax_step(st, vt_ref[src, kvh, :, pl.ds(k0, TK)], m_ref.at[src, kvh], acc_ref.at[src, kvh])

    _pipelined_sweep(3 + i - v0, qk, consume, st_a, st_b)

    gt = gate_ref[...].T
    for g in range(N_GRP):
        cs = slice(g * TQ, (g + 1) * TQ)
        ots = []
        for kvh in kv_heads:
            gr = [gt[br * 8 + kvh * 4 + g:br * 8 + kvh * 4 + g + 1] for br in range(3)]
            a_s = acc_ref[SEL, kvh, :, cs]
            a_w = acc_ref[WIN, kvh, :, cs]
            ots.append(gr[0] * o_cmp[kvh][:, cs]
                       + (gr[1] / a_s[NSA_D:NSA_D + 1]) * a_s[:NSA_D]
                       + (gr[2] / a_w[NSA_D:NSA_D + 1]) * a_w[:NSA_D])
        og = jnp.concatenate(ots, axis=0).T
        sl = slice(g * LANES, (g + 1) * LANES)
        o_ref[:, sl] = (og * nz_ref[:, sl].astype(F32)).astype(BF16)


def _out_kernel(x_ref, od_ref, on_ref, w_ref, o_ref):
    acc = _dot(od_ref[...], w_ref[:DIFF_W, :]) + _dot(on_ref[...], w_ref[DIFF_W:, :])
    o_ref[...] = x_ref[...] + acc


def _params(sem):
    return pltpu.CompilerParams(dimension_semantics=sem, vmem_limit_bytes=VMEM_LIMIT)


def _bf16_pieces(v):
    out = []
    r = np.float32(v)
    for _ in range(3):
        p = np.asarray(r).astype(BF16).astype(np.float32)
        out.append(float(p))
        r = np.float32(r - p)
    return out


def _slope_features(slope):
    f = []
    for piece in _bf16_pieces(slope * LOG2E):
        f += [piece * 256.0, piece]
    return f


def _key_features(seq, with_blocks):
    t = np.arange(seq)
    f = np.zeros((seq, 64), np.float32)
    for r in range(3):
        f[:, 2 * r] = t // 256
        f[:, 2 * r + 1] = t % 256
    if with_blocks:
        f[t, 32 + t // SEL_BLOCK] = 1.0
    return jnp.asarray(np.concatenate([f, f], axis=1), dtype=BF16)


def _nsa_query_features():
    f = np.zeros((N_KV, N_GRP, LANES), np.float32)
    for kvh in range(N_KV):
        base = 64 if kvh == 0 else 0
        for g in range(N_GRP):
            f[kvh, g, base:base + 6] = _slope_features(2.0 ** (-(kvh * N_GRP + g + 1)))
    return jnp.asarray(np.repeat(f[..., None], LANES, axis=-1), dtype=BF16)


def _diff_query_features():
    f = np.zeros((N_DIFF_HEADS, LANES), np.float32)
    for h in range(N_DIFF_HEADS):
        f[h, 0:6] = _slope_features(2.0 ** (-8.0 * (h + 1) / N_DIFF_HEADS))
    return jnp.asarray(np.repeat(f[..., None], LANES, axis=-1), dtype=BF16)


def _overlap_t(seq):
    n_cmp = (seq - CMP_BLOCK) // CMP_STRIDE + 1
    n_sb = seq // SEL_BLOCK
    cs = np.arange(n_cmp) * CMP_STRIDE
    ss = np.arange(n_sb) * SEL_BLOCK
    ovl = ((cs[:, None] < ss[None, :] + SEL_BLOCK) & (cs[:, None] + CMP_BLOCK > ss[None, :])).astype(np.float32)
    out = np.zeros((N_KV, LANES, LANES), np.float32)
    out[0, 96:96 + n_sb, :n_cmp] = ovl.T
    out[1, 32:32 + n_sb, :n_cmp] = ovl.T
    return jnp.asarray(out, dtype=BF16)


def _to_pair_order(a, axis):
    shp = a.shape
    a = a.reshape(shp[:axis] + (N_KV, N_GRP, NSA_D) + shp[axis + 1:])
    a = jnp.swapaxes(a, axis, axis + 1)
    return a.reshape(shp)


def kernel(x, norm_g, w_in, diff_q_norm_g, diff_k_norm_g, diff_lambda_q1, diff_lambda_k1, diff_lambda_q2,
           diff_lambda_k2, diff_subln_g, nsa_q_norm_g, nsa_k_norm_g, cmp_pos, cmp_w1, cmp_b1, cmp_w2, w_out):
    B, T, D = x.shape
    BT = B * T
    assert D == D_MODEL and TQ == TK and T == N_SEL_BLOCKS * SEL_BLOCK and T % TM_PROJ == 0
    n_half = T // CMP_STRIDE
    nq = T // TQ

    w = w_in[0]
    w_p = jnp.concatenate([w[:, :2048], _to_pair_order(w[:, 2048:2560], 1), w[:, 2560:3328],
                           _to_pair_order(w[:, 3328:3840], 1), w[:, IN_MAIN:],
                           jnp.zeros((D, IN_PAD - IN_MAIN - N_GATE), w.dtype)], axis=1).astype(BF16)
    w_o = jnp.concatenate([w_out[0][:DIFF_W], _to_pair_order(w_out[0][DIFF_W:], 0)], axis=0).astype(BF16)
    seg = np.arange(256) // 64
    e_mat = jnp.asarray((seg[:, None] == seg[None, :]).astype(np.float32) / 64.0, dtype=BF16)
    gq = (jnp.tile(diff_q_norm_g[0], 8) * (DIFF_QK ** -0.5 * LOG2E))[None]
    gk = jnp.tile(diff_k_norm_g[0], 8)[None]
    gnq = (jnp.tile(nsa_q_norm_g[0], 8) * (NSA_D ** -0.5 * LOG2E))[None]
    gkc = jnp.tile(nsa_k_norm_g[0, 0], 2)[None]
    gks = jnp.tile(nsa_k_norm_g[0, 1], 2)[None]
    gkw = jnp.tile(nsa_k_norm_g[0, 2], 2)[None]
    feat_s = _key_features(T, True)
    feat_w = _key_features(T, False)

    x2 = x.reshape(BT, D)
    n_rt = BT // TM_PROJ
    rt_per_seq = T // TM_PROJ
    row_spec = lambda n: pl.BlockSpec((TM_PROJ, n), lambda i: (i, 0))
    full = lambda shp: pl.BlockSpec(shp, lambda i: (0,) * len(shp))
    feat_spec = pl.BlockSpec((TM_PROJ, LANES), lambda i: (i % rt_per_seq, 0))
    bt = lambda n, dt=BF16: jax.ShapeDtypeStruct((BT, n), dt)
    outs = pl.pallas_call(
        _proj_kernel,
        grid=(n_rt,),
        in_specs=[row_spec(D), full((1, D)), full((D, IN_PAD)), full((256, 256)),
                  full((1, 512)), full((1, 512)), full((1, 512)), full((1, LANES)), full((1, LANES)),
                  feat_spec, feat_spec],
        out_specs=[row_spec(512), row_spec(512), row_spec(512), row_spec(512), row_spec(512),
                   row_spec(LANES), row_spec(LANES), pl.BlockSpec((4, TM_PROJ, LANES), lambda i: (0, i, 0)),
                   row_spec(LANES), row_spec(LANES), row_spec(512), row_spec(LANES)],
        out_shape=[bt(512), bt(512), bt(512), bt(512), bt(512), bt(LANES, F32), bt(LANES, F32),
                   jax.ShapeDtypeStruct((4, BT, LANES), BF16), bt(LANES), bt(LANES), bt(512), bt(LANES, F32)],
        compiler_params=_params(("parallel",)),
    )(x2, norm_g[0][None], w_p, e_mat, gq, gk, gnq, gks, gkw, feat_s, feat_w)
    dq, dk, dv, dz, nqp, kcr, vcr, kk, vs, vw, nz, gates = outs

    eye2 = jnp.eye(N_KV, dtype=F32)

    def pair_w1(w1):
        wh = w1.reshape(2, 16, 1, NSA_D, 1, CMP_HIDDEN) * eye2[None, None, :, None, :, None]
        return wh.reshape(2, 16 * N_KV * NSA_D, N_KV * CMP_HIDDEN).astype(BF16)

    def pair_w2(w2):
        z = w2[None, :, None, :] * eye2[:, None, :, None]
        return z.reshape(N_KV * CMP_HIDDEN, N_KV * NSA_D).astype(BF16)

    def pair_pos(p):
        ph = p.reshape(2, 16, 1, NSA_D)
        return jnp.broadcast_to(ph, (2, 16, N_KV, NSA_D)).reshape(2, 1, 16 * N_KV * NSA_D)

    w1k = pair_w1(cmp_w1[0, 0]); w1v = pair_w1(cmp_w1[0, 1])
    w1a = jnp.stack([w1k[0], w1v[0]]); w1b = jnp.stack([w1k[1], w1v[1]])
    w2p = jnp.stack([pair_w2(cmp_w2[0, 0]), pair_w2(cmp_w2[0, 1])])
    pk = pair_pos(cmp_pos[0, 0]); pv = pair_pos(cmp_pos[0, 1])
    pa = jnp.stack([pk[0], pv[0]]); pb = jnp.stack([pk[1], pv[1]])
    b1p = jnp.stack([jnp.tile(cmp_b1[0, 0], 2)[None], jnp.tile(cmp_b1[0, 1], 2)[None]])
    hspec = pl.BlockSpec((T, LANES), lambda b: (b, 0))
    kc, vct = pl.pallas_call(
        _cmp_kernel,
        grid=(B,),
        in_specs=[hspec, hspec, full((2, 1, 2048)), full((2, 1, 2048)), full((2, 2048, 512)),
                  full((2, 2048, 512)), full((2, 1, 512)), full((2, 512, LANES)), full((256, 256)),
                  full((1, LANES))],
        out_specs=[pl.BlockSpec((1, 2, n_half, LANES), lambda b: (b, 0, 0, 0)),
                   pl.BlockSpec((1, 2, NSA_D, n_half), lambda b: (b, 0, 0, 0))],
        out_shape=[jax.ShapeDtypeStruct((B, 2, n_half, LANES), BF16),
                   jax.ShapeDtypeStruct((B, 2, NSA_D, n_half), BF16)],
        compiler_params=_params(("parallel",)),
    )(kcr, vcr, pa, pb, w1a, w1b, b1p, w2p, e_mat, gkc)

    lam_v = jnp.concatenate([diff_lambda_q1, diff_lambda_k1, diff_lambda_q2, diff_lambda_k2], axis=0).astype(F32)
    hps = DIFF_HEADS_PER_STEP
    qd_spec = pl.BlockSpec((TQ, hps * LANES), lambda b, h, i: (b * nq + i, h))
    kd_spec = pl.BlockSpec((T, hps * LANES), lambda b, h, i: (b, h))
    st_scratch = pltpu.VMEM((hps, TK, 2 * TQ), F32)
    o_diff = pl.pallas_call(
        _diff_kernel,
        grid=(B, N_DIFF_HEADS // hps, nq),
        in_specs=[qd_spec, kd_spec, kd_spec, qd_spec,
                  pl.BlockSpec((T, LANES), lambda b, h, i: (0, 0)),
                  pl.BlockSpec((hps, LANES, LANES), lambda b, h, i: (h, 0, 0)),
                  pl.BlockSpec((4, DIFF_QK), lambda b, h, i: (0, 0)),
                  pl.BlockSpec((1, DIFF_V), lambda b, h, i: (0, 0))],
        out_specs=qd_spec,
        out_shape=bt(DIFF_W),
        scratch_shapes=[pltpu.VMEM((hps, DIFF_VT_ROWS, T), BF16), pltpu.VMEM((hps, 2 * LANES, 2 * TQ), BF16),
                        st_scratch, st_scratch,
                        pltpu.VMEM((hps, 1, 2 * TQ), F32), pltpu.VMEM((hps, DIFF_ACC_ROWS, 2 * TQ), F32)],
        compiler_params=_params(("arbitrary", "arbitrary", "arbitrary")),
    )(dq, dk, dv, dz, feat_w, _diff_query_features(), lam_v, diff_subln_g[0][None])

    n_all = N_GRP * TQ
    qn_spec = pl.BlockSpec((TQ, 512), lambda b, i: (b * nq + i, 0))
    vpair_spec = pl.BlockSpec((T, LANES), lambda b, i: (b, 0))
    st_nsa = pltpu.VMEM((N_KV, TK, n_all), F32)
    o_nsa = pl.pallas_call(
        _nsa_kernel,
        grid=(B, nq),
        in_specs=[qn_spec,
                  pl.BlockSpec((1, N_KV, n_half, LANES), lambda b, i: (b, 0, 0, 0)),
                  pl.BlockSpec((1, N_KV, NSA_D, n_half), lambda b, i: (b, 0, 0, 0)),
                  pl.BlockSpec((4, T, LANES), lambda b, i: (0, b, 0)),
                  vpair_spec, vpair_spec,
                  pl.BlockSpec((TQ, LANES), lambda b, i: (b * nq + i, 0)),
                  qn_spec,
                  pl.BlockSpec((N_KV, N_GRP, LANES, LANES), lambda b, i: (0, 0, 0, 0)),
                  pl.BlockSpec((N_KV, LANES, LANES), lambda b, i: (0, 0, 0))],
        out_specs=qn_spec,
        out_shape=bt(NSA_W),
        scratch_shapes=[pltpu.VMEM((2, N_KV, NSA_VT_ROWS, T), BF16),
                        pltpu.VMEM((2, N_KV, LANES, n_all), BF16),
                        pltpu.VMEM((3, TK, TQ), F32),
                        st_nsa, st_nsa,
                        pltpu.VMEM((2, N_KV, 1, n_all), F32), pltpu.VMEM((2, N_KV, NSA_ACC_ROWS, n_all), F32)],
        compiler_params=_params(("arbitrary", "arbitrary")),
    )(nqp, kc, vct, kk, vs, vw, gates, nz, _nsa_query_features(), _overlap_t(T))

    out = pl.pallas_call(
        _out_kernel,
        grid=(n_rt,),
        in_specs=[row_spec(D), row_spec(DIFF_W), row_spec(NSA_W), full((D, D))],
        out_specs=row_spec(D),
        out_shape=jax.ShapeDtypeStruct((BT, D), x.dtype),
        compiler_params=_params(("parallel",)),
    )(x2, o_diff, o_nsa, w_o)
    return out.reshape(B, T, D)
```

```python
import math

import numpy as np
import jax
import jax.numpy as jnp
from jax import lax
from jax.experimental import pallas as pl
from jax.experimental.pallas import tpu as pltpu

F32 = jnp.float32
BF16 = jnp.bfloat16

D_MODEL = 1024
N_DIFF_HEADS = 4
DIFF_QK = 64
DIFF_V = 128
DIFF_W = 512
N_NSA_HEADS = 8
NSA_D = 64
N_KV = 2
N_GRP = 4
NSA_W = 512
CMP_BLOCK = 32
CMP_STRIDE = 16
CMP_HIDDEN = 256
SEL_BLOCK = 64
N_SEL_BLOCKS = 32
SEL_TOPK = 8
N_LOCAL = 2
FORCE_BONUS = 1000.0
WINDOW = 512
EPS = 1e-6
NEG_BIG = -1e30
LAM_INIT = 0.8 - 0.6 * math.exp(-0.3 * 0)
LOG2E = math.log2(math.e)

IN_MAIN = 3840
N_GATE = 24
IN_PAD = 3968
LANES = 128
SUBLANES = 8
BF16_ROWS = 16
SEL_MASK_BIAS = -32768.0

TM_PROJ = 512
TQ = 256
TK = 256
VMEM_LIMIT = 56 * 1024 * 1024
DIFF_ACC_ROWS = DIFF_V + SUBLANES
DIFF_VT_ROWS = DIFF_V + BF16_ROWS
DIFF_HEADS_PER_STEP = 2
NSA_ACC_ROWS = NSA_D + SUBLANES
NSA_VT_ROWS = NSA_D + BF16_ROWS


def _dot(a, b):
    return jnp.dot(a, b, preferred_element_type=F32)


def _sigmoid(y):
    return 1.0 / (1.0 + jnp.exp(-y))


def _seg_mean_sq(y, e):
    n = y.shape[1]
    y2 = (y * y).astype(BF16)
    if n == LANES:
        return _dot(y2, e[:LANES, :LANES])
    return jnp.concatenate([_dot(y2[:, c:c + 256], e) for c in range(0, n, 256)], axis=1)


def _seg_norm(y, e, gain):
    return y * lax.rsqrt(_seg_mean_sq(y, e) + EPS) * gain


def _transpose_bf16(a):
    return a.astype(F32).T.astype(BF16)


def _ones_rows(n_rows, width):
    return jnp.where(lax.broadcasted_iota(jnp.int32, (n_rows, width), 0) == 0, 1.0, 0.0).astype(BF16)


def _mask_cols(st, keep, width):
    n = st.shape[1]
    return jnp.concatenate([jnp.where(keep, st[:, a:a + width], -jnp.inf) for a in range(0, n, width)], axis=1)


def _online_softmax_step(st, vt_blk, m_ref, acc_ref):
    m_prev = m_ref[...]
    m_new = jnp.maximum(m_prev, jnp.max(st, axis=0, keepdims=True))
    alpha = jnp.exp2(m_prev - m_new)
    p = jnp.exp2(st - m_new).astype(BF16)
    pv = _dot(vt_blk, p)
    acc_ref[...] = alpha * acc_ref[...] + pv[:acc_ref.shape[0]]
    m_ref[...] = m_new


def _pipelined_sweep(n, n_chunks, qk, consume, st_a, st_b, interleave=True, first_scores_done=False):
    chunks = range(n_chunks)

    def both(j_next, st_next, j_cur, st_cur):
        if interleave:
            for c in chunks:
                qk(j_next, st_next, c)
                consume(j_cur, st_cur, c)
        else:
            for c in chunks:
                qk(j_next, st_next, c)
            for c in chunks:
                consume(j_cur, st_cur, c)

    if not first_scores_done:
        for c in chunks:
            qk(0, st_a, c)

    def body(jj, carry):
        j = 2 * jj
        both(j + 1, st_b, j, st_a)
        both(j + 2, st_a, j + 1, st_b)
        return carry

    lax.fori_loop(0, n // 2, body, 0)

    @pl.when(n % 2 == 0)
    def _():
        for c in chunks:
            consume(n, st_a, c)

    @pl.when(n % 2 == 1)
    def _():
        both(n, st_b, n - 1, st_a)
        for c in chunks:
            consume(n, st_b, c)


def _proj_kernel(x_ref, g_ref, w_ref, e_ref, gq_ref, gk_ref, gnq_ref, gks_ref, gkw_ref, fs_ref, fw_ref,
                 dq_ref, dk_ref, dv_ref, dz_ref, nq_ref, kc_ref, vc_ref, kk_ref, vs_ref, vw_ref,
                 nz_ref, gate_ref):
    x = x_ref[...]
    ms = jnp.mean(x * x, axis=-1, keepdims=True)
    h = (x * lax.rsqrt(ms + EPS) * g_ref[...]).astype(BF16)
    e = e_ref[...]

    def proj(a, b):
        return _dot(h, w_ref[:, a:b])

    dq_ref[...] = _seg_norm(proj(0, 512), e, gq_ref[...]).astype(BF16)
    dk_ref[...] = _seg_norm(proj(512, 1024), e, gk_ref[...]).astype(BF16)
    dv_ref[...] = proj(1024, 1536).astype(BF16)
    y = proj(1536, 2048)
    dz_ref[...] = (y * _sigmoid(y)).astype(BF16)
    nq_ref[...] = _seg_norm(proj(2048, 2560), e, gnq_ref[...]).astype(BF16)
    kc_ref[...] = proj(2560, 2688)
    vc_ref[...] = proj(2688, 2816)

    low = lax.broadcasted_iota(jnp.int32, (1, LANES), 1) < 64
    kp = _seg_norm(proj(2816, 2944), e, gks_ref[...]).astype(BF16)
    fs = fs_ref[...]
    kk_ref[0] = jnp.where(low, kp, fs)
    kk_ref[1] = jnp.where(low, fs, kp)
    vs_ref[...] = proj(2944, 3072).astype(BF16)
    kp = _seg_norm(proj(3072, 3200), e, gkw_ref[...]).astype(BF16)
    fw = fw_ref[...]
    kk_ref[2] = jnp.where(low, kp, fw)
    kk_ref[3] = jnp.where(low, fw, kp)
    vw_ref[...] = proj(3200, 3328).astype(BF16)
    y = proj(3328, 3840)
    nz_ref[...] = (y * _sigmoid(y)).astype(BF16)
    gate_ref[...] = _sigmoid(proj(3840, 3968))


def _cmp_kernel(hk_ref, hv_ref, pa_ref, pb_ref, w1a_ref, w1b_ref, b1_ref, w2_ref, e_ref, gkc_ref,
                kc_ref, vct_ref):
    n_rows = hk_ref.shape[0] // CMP_STRIDE
    row_ok = lax.broadcasted_iota(jnp.int32, (n_rows, 1), 0) < (n_rows - 1)
    low = lax.broadcasted_iota(jnp.int32, (1, LANES), 1) < 64

    def mlp(h_ref, idx):
        hf = jnp.concatenate([h_ref[pl.ds(t, n_rows, stride=CMP_STRIDE), :] for t in range(CMP_STRIDE)], axis=1)
        ha = (hf + pa_ref[idx]).astype(BF16)
        hb = (hf + pb_ref[idx]).astype(BF16)
        a = _dot(ha, w1a_ref[idx])
        b = _dot(hb, w1b_ref[idx])
        hid = a + pltpu.roll(b, n_rows - 1, 0) + b1_ref[idx]
        hid = hid * _sigmoid(hid)
        return _dot(hid.astype(BF16), w2_ref[idx])

    yk = mlp(hk_ref, 0)
    yk = _seg_norm(yk, e_ref[...], gkc_ref[...])
    yk = jnp.where(row_ok, yk, 0.0).astype(BF16)
    zero = jnp.zeros_like(yk)
    kc_ref[0, 0] = jnp.where(low, yk, zero)
    kc_ref[0, 1] = jnp.where(low, zero, yk)
    yvt = jnp.where(row_ok, mlp(hv_ref, 1), 0.0).T.astype(BF16)
    vct_ref[0, 0] = yvt[:NSA_D]
    vct_ref[0, 1] = yvt[NSA_D:]


def _diff_kernel(q_ref, k_ref, v_ref, z_ref, fk_ref, qf_ref, lam_ref, gsub_ref, o_ref,
                 vt_ref, qq_ref, mask_ref, st_a, st_b, m_ref, acc_ref):
    i = pl.program_id(2)
    seq = k_ref.shape[0]
    heads = range(DIFF_HEADS_PER_STEP)

    @pl.when(i == 0)
    def _():
        for hh in heads:
            vt_ref[hh, 0:DIFF_V, :] = _transpose_bf16(v_ref[:, hh * DIFF_V:(hh + 1) * DIFF_V])
            vt_ref[hh, DIFF_V:DIFF_VT_ROWS, :] = _ones_rows(BF16_ROWS, seq)
        causal = lax.broadcasted_iota(jnp.int32, (TK, TQ), 0) <= lax.broadcasted_iota(jnp.int32, (TK, TQ), 1)
        mask_ref[0] = jnp.zeros((TK, TQ), F32)
        mask_ref[1] = jnp.where(causal, 0.0, -jnp.inf)

    rows = lax.broadcasted_iota(jnp.int32, (LANES, TQ), 0)
    for hh in heads:
        qt = _transpose_bf16(q_ref[:, hh * LANES:(hh + 1) * LANES])
        zero = jnp.zeros_like(qt)
        qq_ref[hh, 0:LANES, :] = jnp.concatenate(
            [jnp.where(rows < 64, qt, zero), jnp.where(rows >= 64, qt, zero)], axis=1)
        qq_ref[hh, LANES:2 * LANES, :] = jnp.concatenate([qf_ref[hh]] * (2 * TQ // LANES), axis=1)

    m_ref[...] = jnp.full(m_ref.shape, NEG_BIG, F32)
    acc_ref[...] = jnp.zeros(acc_ref.shape, F32)

    def qk(j, st_ref, hh):
        k0 = pl.multiple_of(j * TK, TK)
        mask = mask_ref[jnp.where(j == i, 1, 0)]
        keys = jnp.concatenate([k_ref[pl.ds(k0, TK), hh * LANES:(hh + 1) * LANES], fk_ref[pl.ds(k0, TK), :]],
                               axis=1)
        st = _dot(keys, qq_ref[hh])
        st_ref[hh] = jnp.concatenate([st[:, a:a + TQ] + mask for a in range(0, 2 * TQ, TQ)], axis=1)

    def consume(j, st_ref, hh):
        k0 = pl.multiple_of(j * TK, TK)
        _online_softmax_step(st_ref[hh], vt_ref[hh, :, pl.ds(k0, TK)], m_ref.at[hh], acc_ref.at[hh])

    _pipelined_sweep(i, DIFF_HEADS_PER_STEP, qk, consume, st_a, st_b, interleave=False)

    lam_v = lam_ref[...]
    lam = (jnp.exp(jnp.sum(lam_v[0:1] * lam_v[1:2], axis=-1, keepdims=True))
           - jnp.exp(jnp.sum(lam_v[2:3] * lam_v[3:4], axis=-1, keepdims=True)) + LAM_INIT)
    for hh in heads:
        acc = acc_ref[hh]
        o2 = acc[0:DIFF_V] / acc[DIFF_V:DIFF_V + 1]
        o = (o2[:, :TQ] - lam * o2[:, TQ:]).T
        ms = jnp.mean(o * o, axis=-1, keepdims=True)
        o = o * lax.rsqrt(ms + EPS) * gsub_ref[...] * (1.0 - LAM_INIT)
        sl = slice(hh * DIFF_V, (hh + 1) * DIFF_V)
        o_ref[:, sl] = (o * z_ref[:, sl].astype(F32)).astype(BF16)


def _nsa_kernel(q_ref, kc_ref, vct_ref, kk_ref, vs_ref, vw_ref, gate_ref, nz_ref, qf_ref, ovt_ref,
                o_ref, vt_ref, qq_ref, mask_ref, st_a, st_b, m_ref, acc_ref):
    i = pl.program_id(1)
    seq = vs_ref.shape[0]
    q0 = i * TQ
    SEL, WIN = 0, 1
    NO_MASK, CAUSAL, UPPER = 0, 1, 2
    kv_heads = range(N_KV)
    krow = lax.broadcasted_iota(jnp.int32, (TK, TQ), 0)
    qcol = lax.broadcasted_iota(jnp.int32, (TK, TQ), 1)

    @pl.when(i == 0)
    def _():
        for br, v_ref in ((SEL, vs_ref), (WIN, vw_ref)):
            vt = _transpose_bf16(v_ref[...])
            for kvh in kv_heads:
                vt_ref[br, kvh, 0:NSA_D, :] = vt[kvh * NSA_D:(kvh + 1) * NSA_D]
                vt_ref[br, kvh, NSA_D:NSA_VT_ROWS, :] = _ones_rows(BF16_ROWS, seq)
        mask_ref[NO_MASK] = jnp.zeros((TK, TQ), F32)
        mask_ref[CAUSAL] = jnp.where(krow <= qcol, 0.0, -jnp.inf)
        mask_ref[UPPER] = jnp.where(krow > qcol, 0.0, -jnp.inf)

    rows = lax.broadcasted_iota(jnp.int32, (LANES, TQ), 0)
    qts = [_transpose_bf16(q_ref[:, g * LANES:(g + 1) * LANES]) for g in range(N_GRP)]
    qbs = []
    for kvh in kv_heads:
        half = (rows < 64) if kvh == 0 else (rows >= 64)
        qb = jnp.concatenate(
            [jnp.where(half, qts[g], jnp.concatenate([qf_ref[kvh, g]] * (TQ // LANES), axis=1))
             for g in range(N_GRP)], axis=1)
        qq_ref[WIN, kvh] = qb
        qq_ref[SEL, kvh] = qb
        qbs.append(qb)
    qpos = q0 + lax.broadcasted_iota(jnp.int32, (1, TQ), 1)

    m_ref[...] = jnp.full(m_ref.shape, NEG_BIG, F32)
    acc_ref[...] = jnp.zeros(acc_ref.shape, F32)

    valid = (rows * CMP_STRIDE + (CMP_BLOCK - 1)) <= qpos
    o_cmp, scores = [], []
    bidx = lax.broadcasted_iota(jnp.int32, (N_SEL_BLOCKS, TQ), 0)
    blk_t = qpos >> 6
    valid_b = bidx <= blk_t
    forced = valid_b & ((bidx == 0) | (bidx >= blk_t - (N_LOCAL - 1)))
    for kvh in kv_heads:
        st = _dot(kc_ref[0, kvh], qbs[kvh])
        p_parts = []
        for g in range(N_GRP):
            sg = jnp.where(valid, st[:, g * TQ:(g + 1) * TQ], NEG_BIG)
            mx = jnp.max(sg, axis=0, keepdims=True)
            ex = jnp.where(valid, jnp.exp2(sg - mx), 0.0)
            den = jnp.sum(ex, axis=0, keepdims=True)
            p_parts.append(ex * jnp.where(den > 0.0, 1.0 / den, 0.0))
        pt = jnp.concatenate(p_parts, axis=1)
        o_cmp.append(_dot(vct_ref[0, kvh], pt.astype(BF16)))
        psum = p_parts[0] + p_parts[1] + p_parts[2] + p_parts[3]
        ovt = ovt_ref[kvh]
        p_hi = psum.astype(BF16)
        p_lo = (psum - p_hi.astype(F32)).astype(BF16)
        seg0 = 96 - 64 * kvh
        imp = (_dot(ovt, p_hi) + _dot(ovt, p_lo))[seg0:seg0 + N_SEL_BLOCKS]
        scores.append(jnp.where(valid_b, imp, -1.0) + jnp.where(forced, FORCE_BONUS, 0.0))

    v0 = 2 - jnp.minimum(i, 2)
    n_chunks = N_KV * N_GRP

    def source(u):
        v = u + v0
        is_win = v < 3
        src = jnp.where(is_win, WIN, SEL)
        kb = jnp.where(is_win, i - 2 + v, v - 3)
        mt = jnp.where(is_win, jnp.where(v == 0, UPPER, jnp.where(v == 2, CAUSAL, NO_MASK)),
                       jnp.where(kb == i, CAUSAL, NO_MASK))
        return src, pl.multiple_of(kb * TK, TK), mt

    def qk(u, st_ref, c):
        kvh, g = divmod(c, N_GRP)
        src, k0, mt = source(u)
        cs = slice(g * TQ, (g + 1) * TQ)
        st_ref[kvh, :, cs] = _dot(kk_ref[2 * src + kvh, pl.ds(k0, TK), :], qq_ref[src, kvh, :, cs]) + mask_ref[mt]

    def consume(u, st_ref, c):
        kvh, g = divmod(c, N_GRP)
        src, k0, _ = source(u)
        cs = slice(g * TQ, (g + 1) * TQ)
        _online_softmax_step(st_ref[kvh, :, cs], vt_ref[src, kvh, :, pl.ds(k0, TK)],
                             m_ref.at[src, kvh, :, cs], acc_ref.at[src, kvh, :, cs])

    row_f = bidx.astype(F32)
    chosen = [jnp.zeros((N_SEL_BLOCKS, TQ), F32) for _ in kv_heads]
    for r in range(max(SEL_TOPK, n_chunks)):
        if r < SEL_TOPK:
            for kvh in kv_heads:
                best = jnp.max(scores[kvh], axis=0, keepdims=True)
                first = jnp.min(jnp.where(scores[kvh] == best, row_f, 1e9), axis=0, keepdims=True)
                pick = row_f == first
                chosen[kvh] = jnp.where(pick, 1.0, chosen[kvh])
                scores[kvh] = jnp.where(pick, -jnp.inf, scores[kvh])
        if r < n_chunks:
            qk(0, st_a, r)
    for kvh in kv_heads:
        sel_bias = jnp.where(chosen[kvh] == 0.0, SEL_MASK_BIAS, 0.0).astype(BF16)
        seg0 = 96 - 64 * kvh
        for g in range(N_GRP):
            qq_ref[SEL, kvh, seg0:seg0 + N_SEL_BLOCKS, g * TQ:(g + 1) * TQ] = sel_bias

    _pipelined_sweep(3 + i - v0, n_chunks, qk, consume, st_a, st_b, first_scores_done=True)

    gt = gate_ref[...].T
    for g in range(N_GRP):
        cs = slice(g * TQ, (g + 1) * TQ)
        ots = []
        for kvh in kv_heads:
            gr = [gt[br * 8 + kvh * 4 + g:br * 8 + kvh * 4 + g + 1] for br in range(3)]
            a_s = acc_ref[SEL, kvh, :, cs]
            a_w = acc_ref[WIN, kvh, :, cs]
            ots.append(gr[0] * o_cmp[kvh][:, cs]
                       + (gr[1] / a_s[NSA_D:NSA_D + 1]) * a_s[:NSA_D]
                       + (gr[2] / a_w[NSA_D:NSA_D + 1]) * a_w[:NSA_D])
        og = jnp.concatenate(ots, axis=0).T
        sl = slice(g * LANES, (g + 1) * LANES)
        o_ref[:, sl] = (og * nz_ref[:, sl].astype(F32)).astype(BF16)


def _out_kernel(x_ref, od_ref, on_ref, w_ref, o_ref):
    acc = _dot(od_ref[...], w_ref[:DIFF_W, :]) + _dot(on_ref[...], w_ref[DIFF_W:, :])
    o_ref[...] = x_ref[...] + acc


def _params(sem):
    return pltpu.CompilerParams(dimension_semantics=sem, vmem_limit_bytes=VMEM_LIMIT)


def _bf16_pieces(v):
    out = []
    r = np.float32(v)
    for _ in range(3):
        p = np.asarray(r).astype(BF16).astype(np.float32)
        out.append(float(p))
        r = np.float32(r - p)
    return out


def _slope_features(slope):
    f = []
    for piece in _bf16_pieces(slope * LOG2E):
        f += [piece * 256.0, piece]
    return f


def _key_features(seq, with_blocks):
    t = np.arange(seq)
    f = np.zeros((seq, 64), np.float32)
    for r in range(3):
        f[:, 2 * r] = t // 256
        f[:, 2 * r + 1] = t % 256
    if with_blocks:
        f[t, 32 + t // SEL_BLOCK] = 1.0
    return jnp.asarray(np.concatenate([f, f], axis=1), dtype=BF16)


def _nsa_query_features():
    f = np.zeros((N_KV, N_GRP, LANES), np.float32)
    for kvh in range(N_KV):
        base = 64 if kvh == 0 else 0
        for g in range(N_GRP):
            f[kvh, g, base:base + 6] = _slope_features(2.0 ** (-(kvh * N_GRP + g + 1)))
    return jnp.asarray(np.repeat(f[..., None], LANES, axis=-1), dtype=BF16)


def _diff_query_features():
    f = np.zeros((N_DIFF_HEADS, LANES), np.float32)
    for h in range(N_DIFF_HEADS):
        f[h, 0:6] = _slope_features(2.0 ** (-8.0 * (h + 1) / N_DIFF_HEADS))
    return jnp.asarray(np.repeat(f[..., None], LANES, axis=-1), dtype=BF16)


def _overlap_t(seq):
    n_cmp = (seq - CMP_BLOCK) // CMP_STRIDE + 1
    n_sb = seq // SEL_BLOCK
    cs = np.arange(n_cmp) * CMP_STRIDE
    ss = np.arange(n_sb) * SEL_BLOCK
    ovl = ((cs[:, None] < ss[None, :] + SEL_BLOCK) & (cs[:, None] + CMP_BLOCK > ss[None, :])).astype(np.float32)
    out = np.zeros((N_KV, LANES, LANES), np.float32)
    out[0, 96:96 + n_sb, :n_cmp] = ovl.T
    out[1, 32:32 + n_sb, :n_cmp] = ovl.T
    return jnp.asarray(out, dtype=BF16)


def _to_pair_order(a, axis):
    shp = a.shape
    a = a.reshape(shp[:axis] + (N_KV, N_GRP, NSA_D) + shp[axis + 1:])
    a = jnp.swapaxes(a, axis, axis + 1)
    return a.reshape(shp)


def kernel(x, norm_g, w_in, diff_q_norm_g, diff_k_norm_g, diff_lambda_q1, diff_lambda_k1, diff_lambda_q2,
           diff_lambda_k2, diff_subln_g, nsa_q_norm_g, nsa_k_norm_g, cmp_pos, cmp_w1, cmp_b1, cmp_w2, w_out):
    B, T, D = x.shape
    BT = B * T
    assert D == D_MODEL and TQ == TK and T == N_SEL_BLOCKS * SEL_BLOCK and T % TM_PROJ == 0
    n_half = T // CMP_STRIDE
    nq = T // TQ

    w = w_in[0]
    w_p = jnp.concatenate([w[:, :2048], _to_pair_order(w[:, 2048:2560], 1), w[:, 2560:3328],
                           _to_pair_order(w[:, 3328:3840], 1), w[:, IN_MAIN:],
                           jnp.zeros((D, IN_PAD - IN_MAIN - N_GATE), w.dtype)], axis=1).astype(BF16)
    w_o = jnp.concatenate([w_out[0][:DIFF_W], _to_pair_order(w_out[0][DIFF_W:], 0)], axis=0).astype(BF16)
    seg = np.arange(256) // 64
    e_mat = jnp.asarray((seg[:, None] == seg[None, :]).astype(np.float32) / 64.0, dtype=BF16)
    gq = (jnp.tile(diff_q_norm_g[0], 8) * (DIFF_QK ** -0.5 * LOG2E))[None]
    gk = jnp.tile(diff_k_norm_g[0], 8)[None]
    gnq = (jnp.tile(nsa_q_norm_g[0], 8) * (NSA_D ** -0.5 * LOG2E))[None]
    gkc = jnp.tile(nsa_k_norm_g[0, 0], 2)[None]
    gks = jnp.tile(nsa_k_norm_g[0, 1], 2)[None]
    gkw = jnp.tile(nsa_k_norm_g[0, 2], 2)[None]
    feat_s = _key_features(T, True)
    feat_w = _key_features(T, False)

    x2 = x.reshape(BT, D)
    n_rt = BT // TM_PROJ
    rt_per_seq = T // TM_PROJ
    row_spec = lambda n: pl.BlockSpec((TM_PROJ, n), lambda i: (i, 0))
    full = lambda shp: pl.BlockSpec(shp, lambda i: (0,) * len(shp))
    feat_spec = pl.BlockSpec((TM_PROJ, LANES), lambda i: (i % rt_per_seq, 0))
    bt = lambda n, dt=BF16: jax.ShapeDtypeStruct((BT, n), dt)
    outs = pl.pallas_call(
        _proj_kernel,
        grid=(n_rt,),
        in_specs=[row_spec(D), full((1, D)), full((D, IN_PAD)), full((256, 256)),
                  full((1, 512)), full((1, 512)), full((1, 512)), full((1, LANES)), full((1, LANES)),
                  feat_spec, feat_spec],
        out_specs=[row_spec(512), row_spec(512), row_spec(512), row_spec(512), row_spec(512),
                   row_spec(LANES), row_spec(LANES), pl.BlockSpec((4, TM_PROJ, LANES), lambda i: (0, i, 0)),
                   row_spec(LANES), row_spec(LANES), row_spec(512), row_spec(LANES)],
        out_shape=[bt(512), bt(512), bt(512), bt(512), bt(512), bt(LANES, F32), bt(LANES, F32),
                   jax.ShapeDtypeStruct((4, BT, LANES), BF16), bt(LANES), bt(LANES), bt(512), bt(LANES, F32)],
        compiler_params=_params(("parallel",)),
    )(x2, norm_g[0][None], w_p, e_mat, gq, gk, gnq, gks, gkw, feat_s, feat_w)
    dq, dk, dv, dz, nqp, kcr, vcr, kk, vs, vw, nz, gates = outs

    eye2 = jnp.eye(N_KV, dtype=F32)

    def pair_w1(w1):
        wh = w1.reshape(2, 16, 1, NSA_D, 1, CMP_HIDDEN) * eye2[None, None, :, None, :, None]
        return wh.reshape(2, 16 * N_KV * NSA_D, N_KV * CMP_HIDDEN).astype(BF16)

    def pair_w2(w2):
        z = w2[None, :, None, :] * eye2[:, None, :, None]
        return z.reshape(N_KV * CMP_HIDDEN, N_KV * NSA_D).astype(BF16)

    def pair_pos(p):
        ph = p.reshape(2, 16, 1, NSA_D)
        return jnp.broadcast_to(ph, (2, 16, N_KV, NSA_D)).reshape(2, 1, 16 * N_KV * NSA_D)

    w1k = pair_w1(cmp_w1[0, 0]); w1v = pair_w1(cmp_w1[0, 1])
    w1a = jnp.stack([w1k[0], w1v[0]]); w1b = jnp.stack([w1k[1], w1v[1]])
    w2p = jnp.stack([pair_w2(cmp_w2[0, 0]), pair_w2(cmp_w2[0, 1])])
    pk = pair_pos(cmp_pos[0, 0]); pv = pair_pos(cmp_pos[0, 1])
    pa = jnp.stack([pk[0], pv[0]]); pb = jnp.stack([pk[1], pv[1]])
    b1p = jnp.stack([jnp.tile(cmp_b1[0, 0], 2)[None], jnp.tile(cmp_b1[0, 1], 2)[None]])
    hspec = pl.BlockSpec((T, LANES), lambda b: (b, 0))
    kc, vct = pl.pallas_call(
        _cmp_kernel,
        grid=(B,),
        in_specs=[hspec, hspec, full((2, 1, 2048)), full((2, 1, 2048)), full((2, 2048, 512)),
                  full((2, 2048, 512)), full((2, 1, 512)), full((2, 512, LANES)), full((256, 256)),
                  full((1, LANES))],
        out_specs=[pl.BlockSpec((1, 2, n_half, LANES), lambda b: (b, 0, 0, 0)),
                   pl.BlockSpec((1, 2, NSA_D, n_half), lambda b: (b, 0, 0, 0))],
        out_shape=[jax.ShapeDtypeStruct((B, 2, n_half, LANES), BF16),
                   jax.ShapeDtypeStruct((B, 2, NSA_D, n_half), BF16)],
        compiler_params=_params(("parallel",)),
    )(kcr, vcr, pa, pb, w1a, w1b, b1p, w2p, e_mat, gkc)

    lam_v = jnp.concatenate([diff_lambda_q1, diff_lambda_k1, diff_lambda_q2, diff_lambda_k2], axis=0).astype(F32)
    hps = DIFF_HEADS_PER_STEP
    qd_spec = pl.BlockSpec((TQ, hps * LANES), lambda b, h, i: (b * nq + i, h))
    kd_spec = pl.BlockSpec((T, hps * LANES), lambda b, h, i: (b, h))
    st_scratch = pltpu.VMEM((hps, TK, 2 * TQ), F32)
    o_diff = pl.pallas_call(
        _diff_kernel,
        grid=(B, N_DIFF_HEADS // hps, nq),
        in_specs=[qd_spec, kd_spec, kd_spec, qd_spec,
                  pl.BlockSpec((T, LANES), lambda b, h, i: (0, 0)),
                  pl.BlockSpec((hps, LANES, LANES), lambda b, h, i: (h, 0, 0)),
                  pl.BlockSpec((4, DIFF_QK), lambda b, h, i: (0, 0)),
                  pl.BlockSpec((1, DIFF_V), lambda b, h, i: (0, 0))],
        out_specs=qd_spec,
        out_shape=bt(DIFF_W),
        scratch_shapes=[pltpu.VMEM((hps, DIFF_VT_ROWS, T), BF16), pltpu.VMEM((hps, 2 * LANES, 2 * TQ), BF16),
                        pltpu.VMEM((2, TK, TQ), F32),
                        st_scratch, st_scratch,
                        pltpu.VMEM((hps, 1, 2 * TQ), F32), pltpu.VMEM((hps, DIFF_ACC_ROWS, 2 * TQ), F32)],
        compiler_params=_params(("arbitrary", "arbitrary", "arbitrary")),
    )(dq, dk, dv, dz, feat_w, _diff_query_features(), lam_v, diff_subln_g[0][None])

    n_all = N_GRP * TQ
    qn_spec = pl.BlockSpec((TQ, 512), lambda b, i: (b * nq + i, 0))
    vpair_spec = pl.BlockSpec((T, LANES), lambda b, i: (b, 0))
    st_nsa = pltpu.VMEM((N_KV, TK, n_all), F32)
    o_nsa = pl.pallas_call(
        _nsa_kernel,
        grid=(B, nq),
        in_specs=[qn_spec,
                  pl.BlockSpec((1, N_KV, n_half, LANES), lambda b, i: (b, 0, 0, 0)),
                  pl.BlockSpec((1, N_KV, NSA_D, n_half), lambda b, i: (b, 0, 0, 0)),
                  pl.BlockSpec((4, T, LANES), lambda b, i: (0, b, 0)),
                  vpair_spec, vpair_spec,
                  pl.BlockSpec((TQ, LANES), lambda b, i: (b * nq + i, 0)),
                  qn_spec,
                  pl.BlockSpec((N_KV, N_GRP, LANES, LANES), lambda b, i: (0, 0, 0, 0)),
                  pl.BlockSpec((N_KV, LANES, LANES), lambda b, i: (0, 0, 0))],
        out_specs=qn_spec,
        out_shape=bt(NSA_W),
        scratch_shapes=[pltpu.VMEM((2, N_KV, NSA_VT_ROWS, T), BF16),
                        pltpu.VMEM((2, N_KV, LANES, n_all), BF16),
                        pltpu.VMEM((3, TK, TQ), F32),
                        st_nsa, st_nsa,
                        pltpu.VMEM((2, N_KV, 1, n_all), F32), pltpu.VMEM((2, N_KV, NSA_ACC_ROWS, n_all), F32)],
        compiler_params=_params(("arbitrary", "arbitrary")),
    )(nqp, kc, vct, kk, vs, vw, gates, nz, _nsa_query_features(), _overlap_t(T))

    out = pl.pallas_call(
        _out_kernel,
        grid=(n_rt,),
        in_specs=[row_spec(D), row_spec(DIFF_W), row_spec(NSA_W), full((D, D))],
        out_specs=row_spec(D),
        out_shape=jax.ShapeDtypeStruct((BT, D), x.dtype),
        compiler_params=_params(("parallel",)),
    )(x2, o_diff, o_nsa, w_o)
    return out.reshape(B, T, D)
```

```python
import math

import numpy as np
import jax
import jax.numpy as jnp
from jax import lax
from jax.experimental import pallas as pl
from jax.experimental.pallas import tpu as pltpu

F32 = jnp.float32
BF16 = jnp.bfloat16

D_MODEL = 1024
N_DIFF_HEADS = 4
DIFF_QK = 64
DIFF_V = 128
DIFF_W = 512
N_NSA_HEADS = 8
NSA_D = 64
N_KV = 2
N_GRP = 4
NSA_W = 512
CMP_BLOCK = 32
CMP_STRIDE = 16
CMP_HIDDEN = 256
SEL_BLOCK = 64
N_SEL_BLOCKS = 32
SEL_TOPK = 8
N_LOCAL = 2
FORCE_BONUS = 1000.0
WINDOW = 512
EPS = 1e-6
NEG_BIG = -1e30
LAM_INIT = 0.8 - 0.6 * math.exp(-0.3 * 0)
LOG2E = math.log2(math.e)

IN_MAIN = 3840
N_GATE = 24
IN_PAD = 3968
LANES = 128
SUBLANES = 8
BF16_ROWS = 16
SEL_MASK_BIAS = -32768.0

TM_PROJ = 512
TQ = 256
TK = 256
VMEM_LIMIT = 56 * 1024 * 1024
DIFF_ACC_ROWS = DIFF_V + SUBLANES
DIFF_VT_ROWS = DIFF_V + BF16_ROWS
DIFF_HEADS_PER_STEP = 2
NSA_ACC_ROWS = NSA_D + SUBLANES
NSA_VT_ROWS = NSA_D + BF16_ROWS


def _dot(a, b):
    return jnp.dot(a, b, preferred_element_type=F32)


def _sigmoid(y):
    return 1.0 / (1.0 + jnp.exp(-y))


def _seg_mean_sq(y, e):
    n = y.shape[1]
    y2 = (y * y).astype(BF16)
    if n == LANES:
        return _dot(y2, e[:LANES, :LANES])
    return jnp.concatenate([_dot(y2[:, c:c + 256], e) for c in range(0, n, 256)], axis=1)


def _seg_norm(y, e, gain):
    return y * lax.rsqrt(_seg_mean_sq(y, e) + EPS) * gain


def _transpose_bf16(a):
    return a.astype(F32).T.astype(BF16)


def _ones_rows(n_rows, width):
    return jnp.where(lax.broadcasted_iota(jnp.int32, (n_rows, width), 0) == 0, 1.0, 0.0).astype(BF16)


def _mask_cols(st, keep, width):
    n = st.shape[1]
    return jnp.concatenate([jnp.where(keep, st[:, a:a + width], -jnp.inf) for a in range(0, n, width)], axis=1)


def _online_softmax_step(st, vt_blk, m_ref, acc_ref):
    m_prev = m_ref[...]
    m_new = jnp.maximum(m_prev, jnp.max(st, axis=0, keepdims=True))
    alpha = jnp.exp2(m_prev - m_new)
    p = jnp.exp2(st - m_new).astype(BF16)
    pv = _dot(vt_blk, p)
    acc_ref[...] = alpha * acc_ref[...] + pv[:acc_ref.shape[0]]
    m_ref[...] = m_new


def _pipelined_sweep(n, n_chunks, qk, consume, st_a, st_b, interleave=True, first_scores_done=False, zero=None):
    chunks = range(n_chunks)

    def both(j_next, st_next, j_cur, st_cur):
        if interleave:
            for c in chunks:
                qk(j_next, st_next, c)
                consume(j_cur, st_cur, c)
        else:
            for c in chunks:
                qk(j_next, st_next, c)
            for c in chunks:
                consume(j_cur, st_cur, c)

    if not first_scores_done:
        for c in chunks:
            qk(0, st_a, c)

    def body(jj, carry):
        j = 2 * jj
        both(j + 1, st_b, j, st_a)
        both(j + 2, st_a, j + 1, st_b)
        return carry

    def tail_even():
        for c in chunks:
            consume(n, st_a, c)

    def tail_odd():
        both(n, st_b, n - 1, st_a)
        for c in chunks:
            consume(n, st_b, c)

    if zero is not None:
        lax.fori_loop(0, n // 2 + zero, body, 0)
        pl.when(zero == 0)(tail_even if n % 2 == 0 else tail_odd)
    elif isinstance(n, int):
        if n // 2 > 0:
            lax.fori_loop(0, n // 2, body, 0)
        (tail_even if n % 2 == 0 else tail_odd)()
    else:
        lax.fori_loop(0, n // 2, body, 0)
        pl.when(n % 2 == 0)(tail_even)
        pl.when(n % 2 == 1)(tail_odd)


def _proj_kernel(x_ref, g_ref, w_ref, e_ref, gq_ref, gk_ref, gnq_ref, gks_ref, gkw_ref, fs_ref, fw_ref,
                 dq_ref, dk_ref, dv_ref, dz_ref, nq_ref, kc_ref, vc_ref, kk_ref, vs_ref, vw_ref,
                 nz_ref, gate_ref):
    x = x_ref[...]
    ms = jnp.mean(x * x, axis=-1, keepdims=True)
    h = (x * lax.rsqrt(ms + EPS) * g_ref[...]).astype(BF16)
    e = e_ref[...]

    def proj(a, b):
        return _dot(h, w_ref[:, a:b])

    dq_ref[...] = _seg_norm(proj(0, 512), e, gq_ref[...]).astype(BF16)
    dk_ref[...] = _seg_norm(proj(512, 1024), e, gk_ref[...]).astype(BF16)
    dv_ref[...] = proj(1024, 1536).astype(BF16)
    y = proj(1536, 2048)
    dz_ref[...] = (y * _sigmoid(y)).astype(BF16)
    nq_ref[...] = _seg_norm(proj(2048, 2560), e, gnq_ref[...]).astype(BF16)
    y = proj(2560, 2816)
    kc_ref[...] = y[:, :LANES]
    vc_ref[...] = y[:, LANES:]

    low = lax.broadcasted_iota(jnp.int32, (1, LANES), 1) < 64
    y = proj(2816, 3072)
    kp = _seg_norm(y[:, :LANES], e, gks_ref[...]).astype(BF16)
    fs = fs_ref[...]
    kk_ref[0] = jnp.where(low, kp, fs)
    kk_ref[1] = jnp.where(low, fs, kp)
    vs_ref[...] = y[:, LANES:].astype(BF16)
    y = proj(3072, 3328)
    kp = _seg_norm(y[:, :LANES], e, gkw_ref[...]).astype(BF16)
    fw = fw_ref[...]
    kk_ref[2] = jnp.where(low, kp, fw)
    kk_ref[3] = jnp.where(low, fw, kp)
    vw_ref[...] = y[:, LANES:].astype(BF16)
    y = proj(3328, 3840)
    nz_ref[...] = (y * _sigmoid(y)).astype(BF16)
    gate_ref[...] = _sigmoid(proj(3840, 3968))


def _cmp_kernel(hk_ref, hv_ref, pa_ref, pb_ref, w1a_ref, w1b_ref, b1_ref, w2_ref, e_ref, gkc_ref,
                kc_ref, vct_ref):
    n_rows = hk_ref.shape[0] // CMP_STRIDE
    row_ok = lax.broadcasted_iota(jnp.int32, (n_rows, 1), 0) < (n_rows - 1)
    low = lax.broadcasted_iota(jnp.int32, (1, LANES), 1) < 64

    def mlp(h_ref, idx):
        hf = jnp.concatenate([h_ref[pl.ds(t, n_rows, stride=CMP_STRIDE), :] for t in range(CMP_STRIDE)], axis=1)
        ha = (hf + pa_ref[idx]).astype(BF16)
        hb = (hf + pb_ref[idx]).astype(BF16)
        a = _dot(ha, w1a_ref[idx])
        b = _dot(hb, w1b_ref[idx])
        hid = a + pltpu.roll(b, n_rows - 1, 0) + b1_ref[idx]
        hid = hid * _sigmoid(hid)
        return _dot(hid.astype(BF16), w2_ref[idx])

    yk = mlp(hk_ref, 0)
    yk = _seg_norm(yk, e_ref[...], gkc_ref[...])
    yk = jnp.where(row_ok, yk, 0.0).astype(BF16)
    zero = jnp.zeros_like(yk)
    kc_ref[0, 0] = jnp.where(low, yk, zero)
    kc_ref[0, 1] = jnp.where(low, zero, yk)
    yvt = jnp.where(row_ok, mlp(hv_ref, 1), 0.0).T.astype(BF16)
    vct_ref[0, 0] = yvt[:NSA_D]
    vct_ref[0, 1] = yvt[NSA_D:]


def _diff_kernel(zero_ref, q_ref, k_ref, v_ref, z_ref, fk_ref, qf_ref, lam_ref, gsub_ref, o_ref,
                 vt_ref, qq_ref, mask_ref, st_a, st_b, m_ref, acc_ref):
    seq = k_ref.shape[0]
    heads = range(DIFF_HEADS_PER_STEP)

    for hh in heads:
        vt_ref[hh, 0:DIFF_V, :] = _transpose_bf16(v_ref[:, hh * DIFF_V:(hh + 1) * DIFF_V])
        vt_ref[hh, DIFF_V:DIFF_VT_ROWS, :] = _ones_rows(BF16_ROWS, seq)
    causal = lax.broadcasted_iota(jnp.int32, (TK, TQ), 0) <= lax.broadcasted_iota(jnp.int32, (TK, TQ), 1)
    mask_ref[0] = jnp.zeros((TK, TQ), F32)
    mask_ref[1] = jnp.where(causal, 0.0, -jnp.inf)
    rows = lax.broadcasted_iota(jnp.int32, (LANES, TQ), 0)
    lam_v = lam_ref[...]
    lam = (jnp.exp(jnp.sum(lam_v[0:1] * lam_v[1:2], axis=-1, keepdims=True))
           - jnp.exp(jnp.sum(lam_v[2:3] * lam_v[3:4], axis=-1, keepdims=True)) + LAM_INIT)

    def qk(i, j, st_ref, hh):
        k0 = j * TK if isinstance(j, int) else pl.multiple_of(j * TK, TK)
        diag = int(j == i) if isinstance(j, int) else jnp.where(j == i, 1, 0)
        mask = mask_ref[diag]
        keys = jnp.concatenate([k_ref[pl.ds(k0, TK), hh * LANES:(hh + 1) * LANES], fk_ref[pl.ds(k0, TK), :]],
                               axis=1)
        st = _dot(keys, qq_ref[i % 2, hh])
        st_ref[hh] = jnp.concatenate([st[:, a:a + TQ] + mask for a in range(0, 2 * TQ, TQ)], axis=1)

    def consume(j, st_ref, hh):
        k0 = j * TK if isinstance(j, int) else pl.multiple_of(j * TK, TK)
        _online_softmax_step(st_ref[hh], vt_ref[hh, :, pl.ds(k0, TK)], m_ref.at[hh], acc_ref.at[hh])

    def prepare(i, hh):
        qt = _transpose_bf16(q_ref[i * TQ:(i + 1) * TQ, hh * LANES:(hh + 1) * LANES])
        zero = jnp.zeros_like(qt)
        qq_ref[i % 2, hh, 0:LANES, :] = jnp.concatenate(
            [jnp.where(rows < 64, qt, zero), jnp.where(rows >= 64, qt, zero)], axis=1)
        qq_ref[i % 2, hh, LANES:2 * LANES, :] = jnp.concatenate([qf_ref[hh]] * (2 * TQ // LANES), axis=1)
        qk(i, 0, st_a, hh)

    def finalize(i, hh):
        acc = acc_ref[hh]
        o2 = acc[0:DIFF_V] / acc[DIFF_V:DIFF_V + 1]
        o = (o2[:, :TQ] - lam * o2[:, TQ:]).T
        ms = jnp.mean(o * o, axis=-1, keepdims=True)
        o = o * lax.rsqrt(ms + EPS) * gsub_ref[...] * (1.0 - LAM_INIT)
        sl = slice(hh * DIFF_V, (hh + 1) * DIFF_V)
        rs = slice(i * TQ, (i + 1) * TQ)
        o_ref[rs, sl] = (o * z_ref[rs, sl].astype(F32)).astype(BF16)

    def reset():
        m_ref[...] = jnp.full(m_ref.shape, NEG_BIG, F32)
        acc_ref[...] = jnp.zeros(acc_ref.shape, F32)

    reset()
    for hh in heads:
        prepare(0, hh)
    n_q = seq // TQ
    for i in range(n_q):
        _pipelined_sweep(i, DIFF_HEADS_PER_STEP, lambda j, st, hh: qk(i, j, st, hh), consume, st_a, st_b,
                         interleave=False, first_scores_done=True, zero=zero_ref[0])
        for hh in heads:
            if i + 1 < n_q:
                prepare(i + 1, hh)
            finalize(i, hh)
        reset()


def _nsa_kernel(q_ref, kc_ref, vct_ref, kk_ref, vs_ref, vw_ref, gate_ref, nz_ref, qf_ref, ovt_ref,
                o_ref, vt_ref, qq_ref, ocmp_ref, mask_ref, st_a, st_b, m_ref, acc_ref):
    seq = vs_ref.shape[0]
    n_q = seq // TQ
    SEL, WIN = 0, 1
    NO_MASK, CAUSAL, UPPER = 0, 1, 2
    kv_heads = range(N_KV)
    n_chunks = N_KV * N_GRP
    krow = lax.broadcasted_iota(jnp.int32, (TK, TQ), 0)
    qcol = lax.broadcasted_iota(jnp.int32, (TK, TQ), 1)
    rows = lax.broadcasted_iota(jnp.int32, (LANES, TQ), 0)
    bidx = lax.broadcasted_iota(jnp.int32, (N_SEL_BLOCKS, TQ), 0)
    row_f = bidx.astype(F32)

    for br, v_ref in ((SEL, vs_ref), (WIN, vw_ref)):
        vt = _transpose_bf16(v_ref[...])
        for kvh in kv_heads:
            vt_ref[br, kvh, 0:NSA_D, :] = vt[kvh * NSA_D:(kvh + 1) * NSA_D]
            vt_ref[br, kvh, NSA_D:NSA_VT_ROWS, :] = _ones_rows(BF16_ROWS, seq)
    mask_ref[NO_MASK] = jnp.zeros((TK, TQ), F32)
    mask_ref[CAUSAL] = jnp.where(krow <= qcol, 0.0, -jnp.inf)
    mask_ref[UPPER] = jnp.where(krow > qcol, 0.0, -jnp.inf)

    def reset():
        m_ref[...] = jnp.full(m_ref.shape, NEG_BIG, F32)
        acc_ref[...] = jnp.zeros(acc_ref.shape, F32)

    def first_virtual(i):
        return 2 - min(i, 2)

    def source(i, u):
        v = u + first_virtual(i)
        if isinstance(u, int):
            if v < 3:
                return WIN, (i - 2 + v) * TK, (UPPER if v == 0 else CAUSAL if v == 2 else NO_MASK)
            return SEL, (v - 3) * TK, (CAUSAL if v - 3 == i else NO_MASK)
        is_win = v < 3
        src = jnp.where(is_win, WIN, SEL)
        kb = jnp.where(is_win, i - 2 + v, v - 3)
        mt = jnp.where(is_win, jnp.where(v == 0, UPPER, jnp.where(v == 2, CAUSAL, NO_MASK)),
                       jnp.where(kb == i, CAUSAL, NO_MASK))
        return src, pl.multiple_of(kb * TK, TK), mt

    def qk(i, u, st_ref, c):
        kvh, g = divmod(c, N_GRP)
        src, k0, mt = source(i, u)
        cs = slice(g * TQ, (g + 1) * TQ)
        st_ref[kvh, :, cs] = (_dot(kk_ref[2 * src + kvh, pl.ds(k0, TK), :], qq_ref[i % 2, src, kvh, :, cs])
                              + mask_ref[mt])

    def consume(i, u, st_ref, c):
        kvh, g = divmod(c, N_GRP)
        src, k0, _ = source(i, u)
        cs = slice(g * TQ, (g + 1) * TQ)
        _online_softmax_step(st_ref[kvh, :, cs], vt_ref[src, kvh, :, pl.ds(k0, TK)],
                             m_ref.at[src, kvh, :, cs], acc_ref.at[src, kvh, :, cs])

    def prepare(i, interleaved):
        rs = slice(i * TQ, (i + 1) * TQ)
        slot = i % 2
        qts = [_transpose_bf16(q_ref[rs, g * LANES:(g + 1) * LANES]) for g in range(N_GRP)]
        qbs = []
        for kvh in kv_heads:
            half = (rows < 64) if kvh == 0 else (rows >= 64)
            qb = jnp.concatenate(
                [jnp.where(half, qts[g], jnp.concatenate([qf_ref[kvh, g]] * (TQ // LANES), axis=1))
                 for g in range(N_GRP)], axis=1)
            qq_ref[slot, WIN, kvh] = qb
            qq_ref[slot, SEL, kvh] = qb
            qbs.append(qb)
        qpos = i * TQ + lax.broadcasted_iota(jnp.int32, (1, TQ), 1)

        valid = (rows * CMP_STRIDE + (CMP_BLOCK - 1)) <= qpos
        blk_t = qpos >> 6
        valid_b = bidx <= blk_t
        forced = valid_b & ((bidx == 0) | (bidx >= blk_t - (N_LOCAL - 1)))
        scores = []
        for kvh in kv_heads:
            st = _dot(kc_ref[0, kvh], qbs[kvh])
            p_parts = []
            for g in range(N_GRP):
                sg = jnp.where(valid, st[:, g * TQ:(g + 1) * TQ], NEG_BIG)
                mx = jnp.max(sg, axis=0, keepdims=True)
                ex = jnp.where(valid, jnp.exp2(sg - mx), 0.0)
                den = jnp.sum(ex, axis=0, keepdims=True)
                p_parts.append(ex * jnp.where(den > 0.0, 1.0 / den, 0.0))
            pt = jnp.concatenate(p_parts, axis=1)
            ocmp_ref[slot, kvh] = _dot(vct_ref[0, kvh], pt.astype(BF16))
            psum = p_parts[0] + p_parts[1] + p_parts[2] + p_parts[3]
            ovt = ovt_ref[kvh]
            p_hi = psum.astype(BF16)
            p_lo = (psum - p_hi.astype(F32)).astype(BF16)
            seg0 = 96 - 64 * kvh
            imp = (_dot(ovt, p_hi) + _dot(ovt, p_lo))[seg0:seg0 + N_SEL_BLOCKS]
            scores.append(jnp.where(valid_b, imp, -1.0) + jnp.where(forced, FORCE_BONUS, 0.0))

        chosen = [jnp.zeros((N_SEL_BLOCKS, TQ), F32) for _ in kv_heads]
        for r in range(max(SEL_TOPK, n_chunks, len(interleaved))):
            if r < SEL_TOPK:
                for kvh in kv_heads:
                    best = jnp.max(scores[kvh], axis=0, keepdims=True)
                    first = jnp.min(jnp.where(scores[kvh] == best, row_f, 1e9), axis=0, keepdims=True)
                    pick = row_f == first
                    chosen[kvh] = jnp.where(pick, 1.0, chosen[kvh])
                    scores[kvh] = jnp.where(pick, -jnp.inf, scores[kvh])
            if r < n_chunks:
                qk(i, 0, st_a, r)
            if r < len(interleaved):
                interleaved[r]()
        for kvh in kv_heads:
            sel_bias = jnp.where(chosen[kvh] == 0.0, SEL_MASK_BIAS, 0.0).astype(BF16)
            seg0 = 96 - 64 * kvh
            for g in range(N_GRP):
                qq_ref[slot, SEL, kvh, seg0:seg0 + N_SEL_BLOCKS, g * TQ:(g + 1) * TQ] = sel_bias

    gate_rows = {}

    def output_piece(i, g):
        rs = slice(i * TQ, (i + 1) * TQ)
        if i not in gate_rows:
            gate_rows[i] = gate_ref[rs, :].T
        gt = gate_rows[i]
        cs = slice(g * TQ, (g + 1) * TQ)
        ots = []
        for kvh in kv_heads:
            gr = [gt[br * 8 + kvh * 4 + g:br * 8 + kvh * 4 + g + 1] for br in range(3)]
            a_s = acc_ref[SEL, kvh, :, cs]
            a_w = acc_ref[WIN, kvh, :, cs]
            ots.append(gr[0] * ocmp_ref[i % 2, kvh, :, cs]
                       + (gr[1] / a_s[NSA_D:NSA_D + 1]) * a_s[:NSA_D]
                       + (gr[2] / a_w[NSA_D:NSA_D + 1]) * a_w[:NSA_D])
        og = jnp.concatenate(ots, axis=0).T
        sl = slice(g * LANES, (g + 1) * LANES)
        o_ref[rs, sl] = (og * nz_ref[rs, sl].astype(F32)).astype(BF16)

    reset()
    prepare(0, [])
    for i in range(n_q):
        _pipelined_sweep(3 + i - first_virtual(i), n_chunks,
                         lambda u, st, c: qk(i, u, st, c), lambda u, st, c: consume(i, u, st, c),
                         st_a, st_b, first_scores_done=True)
        pieces = [(lambda g=g: output_piece(i, g)) for g in range(N_GRP)]
        if i + 1 < n_q:
            prepare(i + 1, pieces)
        else:
            for piece in pieces:
                piece()
        reset()


def _out_kernel(x_ref, od_ref, on_ref, w_ref, o_ref):
    acc = _dot(od_ref[...], w_ref[:DIFF_W, :]) + _dot(on_ref[...], w_ref[DIFF_W:, :])
    o_ref[...] = x_ref[...] + acc


def _params(sem):
    return pltpu.CompilerParams(dimension_semantics=sem, vmem_limit_bytes=VMEM_LIMIT)


def _bf16_pieces(v):
    out = []
    r = np.float32(v)
    for _ in range(3):
        p = np.asarray(r).astype(BF16).astype(np.float32)
        out.append(float(p))
        r = np.float32(r - p)
    return out


def _slope_features(slope):
    f = []
    for piece in _bf16_pieces(slope * LOG2E):
        f += [piece * 256.0, piece]
    return f


def _key_features(seq, with_blocks):
    t = np.arange(seq)
    f = np.zeros((seq, 64), np.float32)
    for r in range(3):
        f[:, 2 * r] = t // 256
        f[:, 2 * r + 1] = t % 256
    if with_blocks:
        f[t, 32 + t // SEL_BLOCK] = 1.0
    return jnp.asarray(np.concatenate([f, f], axis=1), dtype=BF16)


def _nsa_query_features():
    f = np.zeros((N_KV, N_GRP, LANES), np.float32)
    for kvh in range(N_KV):
        base = 64 if kvh == 0 else 0
        for g in range(N_GRP):
            f[kvh, g, base:base + 6] = _slope_features(2.0 ** (-(kvh * N_GRP + g + 1)))
    return jnp.asarray(np.repeat(f[..., None], LANES, axis=-1), dtype=BF16)


def _diff_query_features():
    f = np.zeros((N_DIFF_HEADS, LANES), np.float32)
    for h in range(N_DIFF_HEADS):
        f[h, 0:6] = _slope_features(2.0 ** (-8.0 * (h + 1) / N_DIFF_HEADS))
    return jnp.asarray(np.repeat(f[..., None], LANES, axis=-1), dtype=BF16)


def _overlap_t(seq):
    n_cmp = (seq - CMP_BLOCK) // CMP_STRIDE + 1
    n_sb = seq // SEL_BLOCK
    cs = np.arange(n_cmp) * CMP_STRIDE
    ss = np.arange(n_sb) * SEL_BLOCK
    ovl = ((cs[:, None] < ss[None, :] + SEL_BLOCK) & (cs[:, None] + CMP_BLOCK > ss[None, :])).astype(np.float32)
    out = np.zeros((N_KV, LANES, LANES), np.float32)
    out[0, 96:96 + n_sb, :n_cmp] = ovl.T
    out[1, 32:32 + n_sb, :n_cmp] = ovl.T
    return jnp.asarray(out, dtype=BF16)


def _to_pair_order(a, axis):
    shp = a.shape
    a = a.reshape(shp[:axis] + (N_KV, N_GRP, NSA_D) + shp[axis + 1:])
    a = jnp.swapaxes(a, axis, axis + 1)
    return a.reshape(shp)


def kernel(x, norm_g, w_in, diff_q_norm_g, diff_k_norm_g, diff_lambda_q1, diff_lambda_k1, diff_lambda_q2,
           diff_lambda_k2, diff_subln_g, nsa_q_norm_g, nsa_k_norm_g, cmp_pos, cmp_w1, cmp_b1, cmp_w2, w_out):
    B, T, D = x.shape
    BT = B * T
    assert D == D_MODEL and TQ == TK and T == N_SEL_BLOCKS * SEL_BLOCK and T % TM_PROJ == 0
    n_half = T // CMP_STRIDE
    nq = T // TQ

    w = w_in[0]
    w_p = jnp.concatenate([w[:, :2048], _to_pair_order(w[:, 2048:2560], 1), w[:, 2560:3328],
                           _to_pair_order(w[:, 3328:3840], 1), w[:, IN_MAIN:],
                           jnp.zeros((D, IN_PAD - IN_MAIN - N_GATE), w.dtype)], axis=1).astype(BF16)
    w_o = jnp.concatenate([w_out[0][:DIFF_W], _to_pair_order(w_out[0][DIFF_W:], 0)], axis=0).astype(BF16)
    seg = np.arange(256) // 64
    e_mat = jnp.asarray((seg[:, None] == seg[None, :]).astype(np.float32) / 64.0, dtype=BF16)
    gq = (jnp.tile(diff_q_norm_g[0], 8) * (DIFF_QK ** -0.5 * LOG2E))[None]
    gk = jnp.tile(diff_k_norm_g[0], 8)[None]
    gnq = (jnp.tile(nsa_q_norm_g[0], 8) * (NSA_D ** -0.5 * LOG2E))[None]
    gkc = jnp.tile(nsa_k_norm_g[0, 0], 2)[None]
    gks = jnp.tile(nsa_k_norm_g[0, 1], 2)[None]
    gkw = jnp.tile(nsa_k_norm_g[0, 2], 2)[None]
    feat_s = _key_features(T, True)
    feat_w = _key_features(T, False)

    x2 = x.reshape(BT, D)
    n_rt = BT // TM_PROJ
    rt_per_seq = T // TM_PROJ
    row_spec = lambda n: pl.BlockSpec((TM_PROJ, n), lambda i: (i, 0))
    full = lambda shp: pl.BlockSpec(shp, lambda i: (0,) * len(shp))
    feat_spec = pl.BlockSpec((TM_PROJ, LANES), lambda i: (i % rt_per_seq, 0))
    bt = lambda n, dt=BF16: jax.ShapeDtypeStruct((BT, n), dt)
    outs = pl.pallas_call(
        _proj_kernel,
        grid=(n_rt,),
        in_specs=[row_spec(D), full((1, D)), full((D, IN_PAD)), full((256, 256)),
                  full((1, 512)), full((1, 512)), full((1, 512)), full((1, LANES)), full((1, LANES)),
                  feat_spec, feat_spec],
        out_specs=[row_spec(512), row_spec(512), row_spec(512), row_spec(512), row_spec(512),
                   row_spec(LANES), row_spec(LANES), pl.BlockSpec((4, TM_PROJ, LANES), lambda i: (0, i, 0)),
                   row_spec(LANES), row_spec(LANES), row_spec(512), row_spec(LANES)],
        out_shape=[bt(512), bt(512), bt(512), bt(512), bt(512), bt(LANES, F32), bt(LANES, F32),
                   jax.ShapeDtypeStruct((4, BT, LANES), BF16), bt(LANES), bt(LANES), bt(512), bt(LANES, F32)],
        compiler_params=_params(("parallel",)),
    )(x2, norm_g[0][None], w_p, e_mat, gq, gk, gnq, gks, gkw, feat_s, feat_w)
    dq, dk, dv, dz, nqp, kcr, vcr, kk, vs, vw, nz, gates = outs

    eye2 = jnp.eye(N_KV, dtype=F32)

    def pair_w1(w1):
        wh = w1.reshape(2, 16, 1, NSA_D, 1, CMP_HIDDEN) * eye2[None, None, :, None, :, None]
        return wh.reshape(2, 16 * N_KV * NSA_D, N_KV * CMP_HIDDEN).astype(BF16)

    def pair_w2(w2):
        z = w2[None, :, None, :] * eye2[:, None, :, None]
        return z.reshape(N_KV * CMP_HIDDEN, N_KV * NSA_D).astype(BF16)

    def pair_pos(p):
        ph = p.reshape(2, 16, 1, NSA_D)
        return jnp.broadcast_to(ph, (2, 16, N_KV, NSA_D)).reshape(2, 1, 16 * N_KV * NSA_D)

    w1k = pair_w1(cmp_w1[0, 0]); w1v = pair_w1(cmp_w1[0, 1])
    w1a = jnp.stack([w1k[0], w1v[0]]); w1b = jnp.stack([w1k[1], w1v[1]])
    w2p = jnp.stack([pair_w2(cmp_w2[0, 0]), pair_w2(cmp_w2[0, 1])])
    pk = pair_pos(cmp_pos[0, 0]); pv = pair_pos(cmp_pos[0, 1])
    pa = jnp.stack([pk[0], pv[0]]); pb = jnp.stack([pk[1], pv[1]])
    b1p = jnp.stack([jnp.tile(cmp_b1[0, 0], 2)[None], jnp.tile(cmp_b1[0, 1], 2)[None]])
    hspec = pl.BlockSpec((T, LANES), lambda b: (b, 0))
    kc, vct = pl.pallas_call(
        _cmp_kernel,
        grid=(B,),
        in_specs=[hspec, hspec, full((2, 1, 2048)), full((2, 1, 2048)), full((2, 2048, 512)),
                  full((2, 2048, 512)), full((2, 1, 512)), full((2, 512, LANES)), full((256, 256)),
                  full((1, LANES))],
        out_specs=[pl.BlockSpec((1, 2, n_half, LANES), lambda b: (b, 0, 0, 0)),
                   pl.BlockSpec((1, 2, NSA_D, n_half), lambda b: (b, 0, 0, 0))],
        out_shape=[jax.ShapeDtypeStruct((B, 2, n_half, LANES), BF16),
                   jax.ShapeDtypeStruct((B, 2, NSA_D, n_half), BF16)],
        compiler_params=_params(("parallel",)),
    )(kcr, vcr, pa, pb, w1a, w1b, b1p, w2p, e_mat, gkc)

    lam_v = jnp.concatenate([diff_lambda_q1, diff_lambda_k1, diff_lambda_q2, diff_lambda_k2], axis=0).astype(F32)
    hps = DIFF_HEADS_PER_STEP
    seq_spec = pl.BlockSpec((T, hps * LANES), lambda b, h: (b, h))
    st_scratch = pltpu.VMEM((hps, TK, 2 * TQ), F32)
    o_diff = pl.pallas_call(
        _diff_kernel,
        grid=(B, N_DIFF_HEADS // hps),
        in_specs=[pl.BlockSpec(memory_space=pltpu.SMEM), seq_spec, seq_spec, seq_spec, seq_spec,
                  pl.BlockSpec((T, LANES), lambda b, h: (0, 0)),
                  pl.BlockSpec((hps, LANES, LANES), lambda b, h: (h, 0, 0)),
                  pl.BlockSpec((4, DIFF_QK), lambda b, h: (0, 0)),
                  pl.BlockSpec((1, DIFF_V), lambda b, h: (0, 0))],
        out_specs=seq_spec,
        out_shape=bt(DIFF_W),
        scratch_shapes=[pltpu.VMEM((hps, DIFF_VT_ROWS, T), BF16),
                        pltpu.VMEM((2, hps, 2 * LANES, 2 * TQ), BF16),
                        pltpu.VMEM((2, TK, TQ), F32),
                        st_scratch, st_scratch,
                        pltpu.VMEM((hps, 1, 2 * TQ), F32), pltpu.VMEM((hps, DIFF_ACC_ROWS, 2 * TQ), F32)],
        compiler_params=_params(("arbitrary", "arbitrary")),
    )(jnp.zeros((1,), jnp.int32), dq, dk, dv, dz, feat_w, _diff_query_features(), lam_v, diff_subln_g[0][None])

    n_all = N_GRP * TQ
    seq512 = pl.BlockSpec((T, 512), lambda b: (b, 0))
    seq128 = pl.BlockSpec((T, LANES), lambda b: (b, 0))
    st_nsa = pltpu.VMEM((N_KV, TK, n_all), F32)
    o_nsa = pl.pallas_call(
        _nsa_kernel,
        grid=(B,),
        in_specs=[seq512,
                  pl.BlockSpec((1, N_KV, n_half, LANES), lambda b: (b, 0, 0, 0)),
                  pl.BlockSpec((1, N_KV, NSA_D, n_half), lambda b: (b, 0, 0, 0)),
                  pl.BlockSpec((4, T, LANES), lambda b: (0, b, 0)),
                  seq128, seq128, seq128, seq512,
                  pl.BlockSpec((N_KV, N_GRP, LANES, LANES), lambda b: (0, 0, 0, 0)),
                  pl.BlockSpec((N_KV, LANES, LANES), lambda b: (0, 0, 0))],
        out_specs=seq512,
        out_shape=bt(NSA_W),
        scratch_shapes=[pltpu.VMEM((2, N_KV, NSA_VT_ROWS, T), BF16),
                        pltpu.VMEM((2, 2, N_KV, LANES, n_all), BF16),
                        pltpu.VMEM((2, N_KV, NSA_D, n_all), F32),
                        pltpu.VMEM((3, TK, TQ), F32),
                        st_nsa, st_nsa,
                        pltpu.VMEM((2, N_KV, 1, n_all), F32), pltpu.VMEM((2, N_KV, NSA_ACC_ROWS, n_all), F32)],
        compiler_params=_params(("arbitrary",)),
    )(nqp, kc, vct, kk, vs, vw, gates, nz, _nsa_query_features(), _overlap_t(T))

    out = pl.pallas_call(
        _out_kernel,
        grid=(n_rt,),
        in_specs=[row_spec(D), row_spec(DIFF_W), row_spec(NSA_W), full((D, D))],
        out_specs=row_spec(D),
        out_shape=jax.ShapeDtypeStruct((BT, D), x.dtype),
        compiler_params=_params(("parallel",)),
    )(x2, o_diff, o_nsa, w_o)
    return out.reshape(B, T, D)
```

```python
import math

import numpy as np
import jax
import jax.numpy as jnp
from jax import lax
from jax.experimental import pallas as pl
from jax.experimental.pallas import tpu as pltpu

F32 = jnp.float32
BF16 = jnp.bfloat16

D_MODEL = 1024
N_DIFF_HEADS = 4
DIFF_QK = 64
DIFF_V = 128
DIFF_W = 512
N_NSA_HEADS = 8
NSA_D = 64
N_KV = 2
N_GRP = 4
NSA_W = 512
CMP_BLOCK = 32
CMP_STRIDE = 16
CMP_HIDDEN = 256
SEL_BLOCK = 64
N_SEL_BLOCKS = 32
SEL_TOPK = 8
N_LOCAL = 2
FORCE_BONUS = 1000.0
WINDOW = 512
EPS = 1e-6
NEG_BIG = -1e30
LAM_INIT = 0.8 - 0.6 * math.exp(-0.3 * 0)
LOG2E = math.log2(math.e)

IN_MAIN = 3840
N_GATE = 24
IN_PAD = 3968
LANES = 128
SUBLANES = 8
BF16_ROWS = 16
SEL_MASK_BIAS = -32768.0

TM_PROJ = 512
TQ = 256
TK = 256
VMEM_LIMIT = 56 * 1024 * 1024
DIFF_ACC_ROWS = DIFF_V + SUBLANES
DIFF_VT_ROWS = DIFF_V + BF16_ROWS
DIFF_HEADS_PER_STEP = 2
NSA_ACC_ROWS = NSA_D + SUBLANES
NSA_VT_ROWS = NSA_D + BF16_ROWS


def _dot(a, b):
    return jnp.dot(a, b, preferred_element_type=F32)


def _sigmoid(y):
    return 1.0 / (1.0 + jnp.exp(-y))


def _seg_mean_sq(y, e):
    n = y.shape[1]
    y2 = (y * y).astype(BF16)
    if n == LANES:
        return _dot(y2, e[:LANES, :LANES])
    return jnp.concatenate([_dot(y2[:, c:c + 256], e) for c in range(0, n, 256)], axis=1)


def _seg_norm(y, e, gain):
    return y * lax.rsqrt(_seg_mean_sq(y, e) + EPS) * gain


def _transpose_bf16(a):
    return a.astype(F32).T.astype(BF16)


def _ones_rows(n_rows, width):
    return jnp.where(lax.broadcasted_iota(jnp.int32, (n_rows, width), 0) == 0, 1.0, 0.0).astype(BF16)


def _mask_cols(st, keep, width):
    n = st.shape[1]
    return jnp.concatenate([jnp.where(keep, st[:, a:a + width], -jnp.inf) for a in range(0, n, width)], axis=1)


def _online_softmax_step(st, vt_blk, m_ref, acc_ref):
    m_prev = m_ref[...]
    m_new = jnp.maximum(m_prev, jnp.max(st, axis=0, keepdims=True))
    alpha = jnp.exp2(m_prev - m_new)
    p = jnp.exp2(st - m_new).astype(BF16)
    pv = _dot(vt_blk, p)
    acc_ref[...] = alpha * acc_ref[...] + pv[:acc_ref.shape[0]]
    m_ref[...] = m_new


def _pipelined_sweep(n, n_chunks, qk, consume, st_a, st_b, interleave=True, first_scores_done=False, zero=None):
    chunks = range(n_chunks)

    def both(j_next, st_next, j_cur, st_cur):
        if interleave:
            for c in chunks:
                qk(j_next, st_next, c)
                consume(j_cur, st_cur, c)
        else:
            for c in chunks:
                qk(j_next, st_next, c)
            for c in chunks:
                consume(j_cur, st_cur, c)

    if not first_scores_done:
        for c in chunks:
            qk(0, st_a, c)

    def body(jj, carry):
        j = 2 * jj
        both(j + 1, st_b, j, st_a)
        both(j + 2, st_a, j + 1, st_b)
        return carry

    def tail_even():
        for c in chunks:
            consume(n, st_a, c)

    def tail_odd():
        both(n, st_b, n - 1, st_a)
        for c in chunks:
            consume(n, st_b, c)

    if zero is not None:
        lax.fori_loop(0, n // 2 + zero, body, 0)
        (tail_even if n % 2 == 0 else tail_odd)()
    elif isinstance(n, int):
        if n // 2 > 0:
            lax.fori_loop(0, n // 2, body, 0)
        (tail_even if n % 2 == 0 else tail_odd)()
    else:
        lax.fori_loop(0, n // 2, body, 0)
        pl.when(n % 2 == 0)(tail_even)
        pl.when(n % 2 == 1)(tail_odd)


def _proj_kernel(x_ref, g_ref, w_ref, e_ref, gq_ref, gk_ref, gnq_ref, gks_ref, gkw_ref, fs_ref, fw_ref,
                 dq_ref, dk_ref, dv_ref, dz_ref, nq_ref, kc_ref, vc_ref, kk_ref, vs_ref, vw_ref,
                 nz_ref, gate_ref):
    x = x_ref[...]
    ms = jnp.mean(x * x, axis=-1, keepdims=True)
    h = (x * lax.rsqrt(ms + EPS) * g_ref[...]).astype(BF16)
    e = e_ref[...]

    def proj(a, b):
        return _dot(h, w_ref[:, a:b])

    dq_ref[...] = _seg_norm(proj(0, 512), e, gq_ref[...]).astype(BF16)
    dk_ref[...] = _seg_norm(proj(512, 1024), e, gk_ref[...]).astype(BF16)
    dv_ref[...] = proj(1024, 1536).astype(BF16)
    y = proj(1536, 2048)
    dz_ref[...] = (y * _sigmoid(y)).astype(BF16)
    nq_ref[...] = _seg_norm(proj(2048, 2560), e, gnq_ref[...]).astype(BF16)
    y = proj(2560, 2816)
    kc_ref[...] = y[:, :LANES]
    vc_ref[...] = y[:, LANES:]

    low = lax.broadcasted_iota(jnp.int32, (1, LANES), 1) < 64
    y = proj(2816, 3072)
    kp = _seg_norm(y[:, :LANES], e, gks_ref[...]).astype(BF16)
    fs = fs_ref[...]
    kk_ref[0] = jnp.where(low, kp, fs)
    kk_ref[1] = jnp.where(low, fs, kp)
    vs_ref[...] = y[:, LANES:].astype(BF16)
    y = proj(3072, 3328)
    kp = _seg_norm(y[:, :LANES], e, gkw_ref[...]).astype(BF16)
    fw = fw_ref[...]
    kk_ref[2] = jnp.where(low, kp, fw)
    kk_ref[3] = jnp.where(low, fw, kp)
    vw_ref[...] = y[:, LANES:].astype(BF16)
    y = proj(3328, 3840)
    nz_ref[...] = (y * _sigmoid(y)).astype(BF16)
    gate_ref[...] = _sigmoid(proj(3840, 3968))


def _cmp_kernel(hk_ref, hv_ref, pa_ref, pb_ref, w1a_ref, w1b_ref, b1_ref, w2_ref, e_ref, gkc_ref,
                kc_ref, vct_ref):
    n_rows = hk_ref.shape[0] // CMP_STRIDE
    row_ok = lax.broadcasted_iota(jnp.int32, (n_rows, 1), 0) < (n_rows - 1)
    low = lax.broadcasted_iota(jnp.int32, (1, LANES), 1) < 64

    def mlp(h_ref, idx):
        hf = jnp.concatenate([h_ref[pl.ds(t, n_rows, stride=CMP_STRIDE), :] for t in range(CMP_STRIDE)], axis=1)
        ha = (hf + pa_ref[idx]).astype(BF16)
        hb = (hf + pb_ref[idx]).astype(BF16)
        a = _dot(ha, w1a_ref[idx])
        b = _dot(hb, w1b_ref[idx])
        hid = a + pltpu.roll(b, n_rows - 1, 0) + b1_ref[idx]
        hid = hid * _sigmoid(hid)
        return _dot(hid.astype(BF16), w2_ref[idx])

    yk = mlp(hk_ref, 0)
    yk = _seg_norm(yk, e_ref[...], gkc_ref[...])
    yk = jnp.where(row_ok, yk, 0.0).astype(BF16)
    zero = jnp.zeros_like(yk)
    kc_ref[0, 0] = jnp.where(low, yk, zero)
    kc_ref[0, 1] = jnp.where(low, zero, yk)
    yvt = jnp.where(row_ok, mlp(hv_ref, 1), 0.0).T.astype(BF16)
    vct_ref[0, 0] = yvt[:NSA_D]
    vct_ref[0, 1] = yvt[NSA_D:]


def _diff_kernel(zero_ref, q_ref, k_ref, v_ref, z_ref, fk_ref, qf_ref, lam_ref, gsub_ref, o_ref,
                 vt_ref, qq_ref, mask_ref, st_a, st_b, m_ref, acc_ref):
    seq = k_ref.shape[0]
    heads = range(DIFF_HEADS_PER_STEP)

    for hh in heads:
        vt_ref[hh, 0:DIFF_V, :] = _transpose_bf16(v_ref[:, hh * DIFF_V:(hh + 1) * DIFF_V])
        vt_ref[hh, DIFF_V:DIFF_VT_ROWS, :] = _ones_rows(BF16_ROWS, seq)
    causal = lax.broadcasted_iota(jnp.int32, (TK, TQ), 0) <= lax.broadcasted_iota(jnp.int32, (TK, TQ), 1)
    mask_ref[0] = jnp.zeros((TK, TQ), F32)
    mask_ref[1] = jnp.where(causal, 0.0, -jnp.inf)
    rows = lax.broadcasted_iota(jnp.int32, (LANES, TQ), 0)
    lam_v = lam_ref[...]
    lam = (jnp.exp(jnp.sum(lam_v[0:1] * lam_v[1:2], axis=-1, keepdims=True))
           - jnp.exp(jnp.sum(lam_v[2:3] * lam_v[3:4], axis=-1, keepdims=True)) + LAM_INIT)

    def qk(i, j, st_ref, hh):
        k0 = j * TK if isinstance(j, int) else pl.multiple_of(j * TK, TK)
        diag = int(j == i) if isinstance(j, int) else jnp.where(j == i, 1, 0)
        mask = mask_ref[diag]
        keys = jnp.concatenate([k_ref[pl.ds(k0, TK), hh * LANES:(hh + 1) * LANES], fk_ref[pl.ds(k0, TK), :]],
                               axis=1)
        st = _dot(keys, qq_ref[i % 2, hh])
        st_ref[hh] = jnp.concatenate([st[:, a:a + TQ] + mask for a in range(0, 2 * TQ, TQ)], axis=1)

    def consume(j, st_ref, hh):
        k0 = j * TK if isinstance(j, int) else pl.multiple_of(j * TK, TK)
        _online_softmax_step(st_ref[hh], vt_ref[hh, :, pl.ds(k0, TK)], m_ref.at[hh], acc_ref.at[hh])

    def prepare(i, hh):
        qt = _transpose_bf16(q_ref[i * TQ:(i + 1) * TQ, hh * LANES:(hh + 1) * LANES])
        zero = jnp.zeros_like(qt)
        qq_ref[i % 2, hh, 0:LANES, :] = jnp.concatenate(
            [jnp.where(rows < 64, qt, zero), jnp.where(rows >= 64, qt, zero)], axis=1)
        qq_ref[i % 2, hh, LANES:2 * LANES, :] = jnp.concatenate([qf_ref[hh]] * (2 * TQ // LANES), axis=1)
        qk(i, 0, st_a, hh)

    def finalize(i, hh):
        acc = acc_ref[hh]
        o2 = acc[0:DIFF_V] / acc[DIFF_V:DIFF_V + 1]
        o = (o2[:, :TQ] - lam * o2[:, TQ:]).T
        ms = jnp.mean(o * o, axis=-1, keepdims=True)
        o = o * lax.rsqrt(ms + EPS) * gsub_ref[...] * (1.0 - LAM_INIT)
        sl = slice(hh * DIFF_V, (hh + 1) * DIFF_V)
        rs = slice(i * TQ, (i + 1) * TQ)
        o_ref[rs, sl] = (o * z_ref[rs, sl].astype(F32)).astype(BF16)

    def reset():
        m_ref[...] = jnp.full(m_ref.shape, NEG_BIG, F32)
        acc_ref[...] = jnp.zeros(acc_ref.shape, F32)

    reset()
    for hh in heads:
        prepare(0, hh)
    n_q = seq // TQ
    for i in range(n_q):
        _pipelined_sweep(i, DIFF_HEADS_PER_STEP, lambda j, st, hh: qk(i, j, st, hh), consume, st_a, st_b,
                         interleave=False, first_scores_done=True, zero=zero_ref[0])
        for hh in heads:
            if i + 1 < n_q:
                prepare(i + 1, hh)
            finalize(i, hh)
        reset()


def _nsa_kernel(q_ref, kc_ref, vct_ref, kk_ref, vs_ref, vw_ref, gate_ref, nz_ref, qf_ref, ovt_ref,
                o_ref, vt_ref, qq_ref, ocmp_ref, mask_ref, st_a, st_b, m_ref, acc_ref):
    seq = vs_ref.shape[0]
    n_q = seq // TQ
    SEL, WIN = 0, 1
    NO_MASK, CAUSAL, UPPER = 0, 1, 2
    kv_heads = range(N_KV)
    n_chunks = N_KV * N_GRP
    krow = lax.broadcasted_iota(jnp.int32, (TK, TQ), 0)
    qcol = lax.broadcasted_iota(jnp.int32, (TK, TQ), 1)
    rows = lax.broadcasted_iota(jnp.int32, (LANES, TQ), 0)
    bidx = lax.broadcasted_iota(jnp.int32, (N_SEL_BLOCKS, TQ), 0)
    row_f = bidx.astype(F32)

    for br, v_ref in ((SEL, vs_ref), (WIN, vw_ref)):
        vt = _transpose_bf16(v_ref[...])
        for kvh in kv_heads:
            vt_ref[br, kvh, 0:NSA_D, :] = vt[kvh * NSA_D:(kvh + 1) * NSA_D]
            vt_ref[br, kvh, NSA_D:NSA_VT_ROWS, :] = _ones_rows(BF16_ROWS, seq)
    mask_ref[NO_MASK] = jnp.zeros((TK, TQ), F32)
    mask_ref[CAUSAL] = jnp.where(krow <= qcol, 0.0, -jnp.inf)
    mask_ref[UPPER] = jnp.where(krow > qcol, 0.0, -jnp.inf)

    def reset():
        m_ref[...] = jnp.full(m_ref.shape, NEG_BIG, F32)
        acc_ref[...] = jnp.zeros(acc_ref.shape, F32)

    def first_virtual(i):
        return 2 - min(i, 2)

    def source(i, u):
        v = u + first_virtual(i)
        if isinstance(u, int):
            if v < 3:
                return WIN, (i - 2 + v) * TK, (UPPER if v == 0 else CAUSAL if v == 2 else NO_MASK)
            return SEL, (v - 3) * TK, (CAUSAL if v - 3 == i else NO_MASK)
        is_win = v < 3
        src = jnp.where(is_win, WIN, SEL)
        kb = jnp.where(is_win, i - 2 + v, v - 3)
        mt = jnp.where(is_win, jnp.where(v == 0, UPPER, jnp.where(v == 2, CAUSAL, NO_MASK)),
                       jnp.where(kb == i, CAUSAL, NO_MASK))
        return src, pl.multiple_of(kb * TK, TK), mt

    def qk(i, u, st_ref, c):
        kvh, g = divmod(c, N_GRP)
        src, k0, mt = source(i, u)
        cs = slice(g * TQ, (g + 1) * TQ)
        st_ref[kvh, :, cs] = (_dot(kk_ref[2 * src + kvh, pl.ds(k0, TK), :], qq_ref[i % 2, src, kvh, :, cs])
                              + mask_ref[mt])

    def consume(i, u, st_ref, c):
        kvh, g = divmod(c, N_GRP)
        src, k0, _ = source(i, u)
        cs = slice(g * TQ, (g + 1) * TQ)
        _online_softmax_step(st_ref[kvh, :, cs], vt_ref[src, kvh, :, pl.ds(k0, TK)],
                             m_ref.at[src, kvh, :, cs], acc_ref.at[src, kvh, :, cs])

    def prepare(i, interleaved):
        rs = slice(i * TQ, (i + 1) * TQ)
        slot = i % 2
        qts = [_transpose_bf16(q_ref[rs, g * LANES:(g + 1) * LANES]) for g in range(N_GRP)]
        qbs = []
        for kvh in kv_heads:
            half = (rows < 64) if kvh == 0 else (rows >= 64)
            qb = jnp.concatenate(
                [jnp.where(half, qts[g], jnp.concatenate([qf_ref[kvh, g]] * (TQ // LANES), axis=1))
                 for g in range(N_GRP)], axis=1)
            qq_ref[slot, WIN, kvh] = qb
            qq_ref[slot, SEL, kvh] = qb
            qbs.append(qb)
        qpos = i * TQ + lax.broadcasted_iota(jnp.int32, (1, TQ), 1)

        valid = (rows * CMP_STRIDE + (CMP_BLOCK - 1)) <= qpos
        blk_t = qpos >> 6
        valid_b = bidx <= blk_t
        forced = valid_b & ((bidx == 0) | (bidx >= blk_t - (N_LOCAL - 1)))
        scores = []
        for kvh in kv_heads:
            st = _dot(kc_ref[0, kvh], qbs[kvh])
            p_parts = []
            for g in range(N_GRP):
                sg = jnp.where(valid, st[:, g * TQ:(g + 1) * TQ], NEG_BIG)
                mx = jnp.max(sg, axis=0, keepdims=True)
                ex = jnp.where(valid, jnp.exp2(sg - mx), 0.0)
                den = jnp.sum(ex, axis=0, keepdims=True)
                p_parts.append(ex * jnp.where(den > 0.0, 1.0 / den, 0.0))
            pt = jnp.concatenate(p_parts, axis=1)
            ocmp_ref[slot, kvh] = _dot(vct_ref[0, kvh], pt.astype(BF16))
            psum = p_parts[0] + p_parts[1] + p_parts[2] + p_parts[3]
            ovt = ovt_ref[kvh]
            p_hi = psum.astype(BF16)
            p_lo = (psum - p_hi.astype(F32)).astype(BF16)
            seg0 = 96 - 64 * kvh
            imp = (_dot(ovt, p_hi) + _dot(ovt, p_lo))[seg0:seg0 + N_SEL_BLOCKS]
            scores.append(jnp.where(valid_b, imp, -1.0) + jnp.where(forced, FORCE_BONUS, 0.0))

        chosen = [jnp.zeros((N_SEL_BLOCKS, TQ), F32) for _ in kv_heads]
        for r in range(max(SEL_TOPK, n_chunks, len(interleaved))):
            if r < SEL_TOPK:
                for kvh in kv_heads:
                    best = jnp.max(scores[kvh], axis=0, keepdims=True)
                    first = jnp.min(jnp.where(scores[kvh] == best, row_f, 1e9), axis=0, keepdims=True)
                    pick = row_f == first
                    chosen[kvh] = jnp.where(pick, 1.0, chosen[kvh])
                    scores[kvh] = jnp.where(pick, -jnp.inf, scores[kvh])
            if r < n_chunks:
                qk(i, 0, st_a, r)
            if r < len(interleaved):
                interleaved[r]()
        for kvh in kv_heads:
            sel_bias = jnp.where(chosen[kvh] == 0.0, SEL_MASK_BIAS, 0.0).astype(BF16)
            seg0 = 96 - 64 * kvh
            for g in range(N_GRP):
                qq_ref[slot, SEL, kvh, seg0:seg0 + N_SEL_BLOCKS, g * TQ:(g + 1) * TQ] = sel_bias

    gate_rows = {}

    def output_piece(i, g):
        rs = slice(i * TQ, (i + 1) * TQ)
        if i not in gate_rows:
            gate_rows[i] = gate_ref[rs, :].T
        gt = gate_rows[i]
        cs = slice(g * TQ, (g + 1) * TQ)
        ots = []
        for kvh in kv_heads:
            gr = [gt[br * 8 + kvh * 4 + g:br * 8 + kvh * 4 + g + 1] for br in range(3)]
            a_s = acc_ref[SEL, kvh, :, cs]
            a_w = acc_ref[WIN, kvh, :, cs]
            ots.append(gr[0] * ocmp_ref[i % 2, kvh, :, cs]
                       + (gr[1] / a_s[NSA_D:NSA_D + 1]) * a_s[:NSA_D]
                       + (gr[2] / a_w[NSA_D:NSA_D + 1]) * a_w[:NSA_D])
        og = jnp.concatenate(ots, axis=0).T
        sl = slice(g * LANES, (g + 1) * LANES)
        o_ref[rs, sl] = (og * nz_ref[rs, sl].astype(F32)).astype(BF16)

    reset()
    prepare(0, [])
    for i in range(n_q):
        _pipelined_sweep(3 + i - first_virtual(i), n_chunks,
                         lambda u, st, c: qk(i, u, st, c), lambda u, st, c: consume(i, u, st, c),
                         st_a, st_b, first_scores_done=True)
        pieces = [(lambda g=g: output_piece(i, g)) for g in range(N_GRP)]
        if i + 1 < n_q:
            prepare(i + 1, pieces)
        else:
            for piece in pieces:
                piece()
        reset()


def _out_kernel(x_ref, od_ref, on_ref, w_ref, o_ref):
    acc = _dot(od_ref[...], w_ref[:DIFF_W, :]) + _dot(on_ref[...], w_ref[DIFF_W:, :])
    o_ref[...] = x_ref[...] + acc


def _params(sem):
    return pltpu.CompilerParams(dimension_semantics=sem, vmem_limit_bytes=VMEM_LIMIT)


def _bf16_pieces(v):
    out = []
    r = np.float32(v)
    for _ in range(3):
        p = np.asarray(r).astype(BF16).astype(np.float32)
        out.append(float(p))
        r = np.float32(r - p)
    return out


def _slope_features(slope):
    f = []
    for piece in _bf16_pieces(slope * LOG2E):
        f += [piece * 256.0, piece]
    return f


def _key_features(seq, with_blocks):
    t = np.arange(seq)
    f = np.zeros((seq, 64), np.float32)
    for r in range(3):
        f[:, 2 * r] = t // 256
        f[:, 2 * r + 1] = t % 256
    if with_blocks:
        f[t, 32 + t // SEL_BLOCK] = 1.0
    return jnp.asarray(np.concatenate([f, f], axis=1), dtype=BF16)


def _nsa_query_features():
    f = np.zeros((N_KV, N_GRP, LANES), np.float32)
    for kvh in range(N_KV):
        base = 64 if kvh == 0 else 0
        for g in range(N_GRP):
            f[kvh, g, base:base + 6] = _slope_features(2.0 ** (-(kvh * N_GRP + g + 1)))
    return jnp.asarray(np.repeat(f[..., None], LANES, axis=-1), dtype=BF16)


def _diff_query_features():
    f = np.zeros((N_DIFF_HEADS, LANES), np.float32)
    for h in range(N_DIFF_HEADS):
        f[h, 0:6] = _slope_features(2.0 ** (-8.0 * (h + 1) / N_DIFF_HEADS))
    return jnp.asarray(np.repeat(f[..., None], LANES, axis=-1), dtype=BF16)


def _overlap_t(seq):
    n_cmp = (seq - CMP_BLOCK) // CMP_STRIDE + 1
    n_sb = seq // SEL_BLOCK
    cs = np.arange(n_cmp) * CMP_STRIDE
    ss = np.arange(n_sb) * SEL_BLOCK
    ovl = ((cs[:, None] < ss[None, :] + SEL_BLOCK) & (cs[:, None] + CMP_BLOCK > ss[None, :])).astype(np.float32)
    out = np.zeros((N_KV, LANES, LANES), np.float32)
    out[0, 96:96 + n_sb, :n_cmp] = ovl.T
    out[1, 32:32 + n_sb, :n_cmp] = ovl.T
    return jnp.asarray(out, dtype=BF16)


def _to_pair_order(a, axis):
    shp = a.shape
    a = a.reshape(shp[:axis] + (N_KV, N_GRP, NSA_D) + shp[axis + 1:])
    a = jnp.swapaxes(a, axis, axis + 1)
    return a.reshape(shp)


def kernel(x, norm_g, w_in, diff_q_norm_g, diff_k_norm_g, diff_lambda_q1, diff_lambda_k1, diff_lambda_q2,
           diff_lambda_k2, diff_subln_g, nsa_q_norm_g, nsa_k_norm_g, cmp_pos, cmp_w1, cmp_b1, cmp_w2, w_out):
    B, T, D = x.shape
    BT = B * T
    assert D == D_MODEL and TQ == TK and T == N_SEL_BLOCKS * SEL_BLOCK and T % TM_PROJ == 0
    n_half = T // CMP_STRIDE
    nq = T // TQ

    w = w_in[0]
    w_p = jnp.concatenate([w[:, :2048], _to_pair_order(w[:, 2048:2560], 1), w[:, 2560:3328],
                           _to_pair_order(w[:, 3328:3840], 1), w[:, IN_MAIN:],
                           jnp.zeros((D, IN_PAD - IN_MAIN - N_GATE), w.dtype)], axis=1).astype(BF16)
    w_o = jnp.concatenate([w_out[0][:DIFF_W], _to_pair_order(w_out[0][DIFF_W:], 0)], axis=0).astype(BF16)
    seg = np.arange(256) // 64
    e_mat = jnp.asarray((seg[:, None] == seg[None, :]).astype(np.float32) / 64.0, dtype=BF16)
    gq = (jnp.tile(diff_q_norm_g[0], 8) * (DIFF_QK ** -0.5 * LOG2E))[None]
    gk = jnp.tile(diff_k_norm_g[0], 8)[None]
    gnq = (jnp.tile(nsa_q_norm_g[0], 8) * (NSA_D ** -0.5 * LOG2E))[None]
    gkc = jnp.tile(nsa_k_norm_g[0, 0], 2)[None]
    gks = jnp.tile(nsa_k_norm_g[0, 1], 2)[None]
    gkw = jnp.tile(nsa_k_norm_g[0, 2], 2)[None]
    feat_s = _key_features(T, True)
    feat_w = _key_features(T, False)

    x2 = x.reshape(BT, D)
    n_rt = BT // TM_PROJ
    rt_per_seq = T // TM_PROJ
    row_spec = lambda n: pl.BlockSpec((TM_PROJ, n), lambda i: (i, 0))
    full = lambda shp: pl.BlockSpec(shp, lambda i: (0,) * len(shp))
    feat_spec = pl.BlockSpec((TM_PROJ, LANES), lambda i: (i % rt_per_seq, 0))
    bt = lambda n, dt=BF16: jax.ShapeDtypeStruct((BT, n), dt)
    outs = pl.pallas_call(
        _proj_kernel,
        grid=(n_rt,),
        in_specs=[row_spec(D), full((1, D)), full((D, IN_PAD)), full((256, 256)),
                  full((1, 512)), full((1, 512)), full((1, 512)), full((1, LANES)), full((1, LANES)),
                  feat_spec, feat_spec],
        out_specs=[row_spec(512), row_spec(512), row_spec(512), row_spec(512), row_spec(512),
                   row_spec(LANES), row_spec(LANES), pl.BlockSpec((4, TM_PROJ, LANES), lambda i: (0, i, 0)),
                   row_spec(LANES), row_spec(LANES), row_spec(512), row_spec(LANES)],
        out_shape=[bt(512), bt(512), bt(512), bt(512), bt(512), bt(LANES, F32), bt(LANES, F32),
                   jax.ShapeDtypeStruct((4, BT, LANES), BF16), bt(LANES), bt(LANES), bt(512), bt(LANES, F32)],
        compiler_params=_params(("parallel",)),
    )(x2, norm_g[0][None], w_p, e_mat, gq, gk, gnq, gks, gkw, feat_s, feat_w)
    dq, dk, dv, dz, nqp, kcr, vcr, kk, vs, vw, nz, gates = outs

    eye2 = jnp.eye(N_KV, dtype=F32)

    def pair_w1(w1):
        wh = w1.reshape(2, 16, 1, NSA_D, 1, CMP_HIDDEN) * eye2[None, None, :, None, :, None]
        return wh.reshape(2, 16 * N_KV * NSA_D, N_KV * CMP_HIDDEN).astype(BF16)

    def pair_w2(w2):
        z = w2[None, :, None, :] * eye2[:, None, :, None]
        return z.reshape(N_KV * CMP_HIDDEN, N_KV * NSA_D).astype(BF16)

    def pair_pos(p):
        ph = p.reshape(2, 16, 1, NSA_D)
        return jnp.broadcast_to(ph, (2, 16, N_KV, NSA_D)).reshape(2, 1, 16 * N_KV * NSA_D)

    w1k = pair_w1(cmp_w1[0, 0]); w1v = pair_w1(cmp_w1[0, 1])
    w1a = jnp.stack([w1k[0], w1v[0]]); w1b = jnp.stack([w1k[1], w1v[1]])
    w2p = jnp.stack([pair_w2(cmp_w2[0, 0]), pair_w2(cmp_w2[0, 1])])
    pk = pair_pos(cmp_pos[0, 0]); pv = pair_pos(cmp_pos[0, 1])
    pa = jnp.stack([pk[0], pv[0]]); pb = jnp.stack([pk[1], pv[1]])
    b1p = jnp.stack([jnp.tile(cmp_b1[0, 0], 2)[None], jnp.tile(cmp_b1[0, 1], 2)[None]])
    hspec = pl.BlockSpec((T, LANES), lambda b: (b, 0))
    kc, vct = pl.pallas_call(
        _cmp_kernel,
        grid=(B,),
        in_specs=[hspec, hspec, full((2, 1, 2048)), full((2, 1, 2048)), full((2, 2048, 512)),
                  full((2, 2048, 512)), full((2, 1, 512)), full((2, 512, LANES)), full((256, 256)),
                  full((1, LANES))],
        out_specs=[pl.BlockSpec((1, 2, n_half, LANES), lambda b: (b, 0, 0, 0)),
                   pl.BlockSpec((1, 2, NSA_D, n_half), lambda b: (b, 0, 0, 0))],
        out_shape=[jax.ShapeDtypeStruct((B, 2, n_half, LANES), BF16),
                   jax.ShapeDtypeStruct((B, 2, NSA_D, n_half), BF16)],
        compiler_params=_params(("parallel",)),
    )(kcr, vcr, pa, pb, w1a, w1b, b1p, w2p, e_mat, gkc)

    lam_v = jnp.concatenate([diff_lambda_q1, diff_lambda_k1, diff_lambda_q2, diff_lambda_k2], axis=0).astype(F32)
    hps = DIFF_HEADS_PER_STEP
    seq_spec = pl.BlockSpec((T, hps * LANES), lambda b, h: (b, h))
    st_scratch = pltpu.VMEM((hps, TK, 2 * TQ), F32)
    o_diff = pl.pallas_call(
        _diff_kernel,
        grid=(B, N_DIFF_HEADS // hps),
        in_specs=[pl.BlockSpec(memory_space=pltpu.SMEM), seq_spec, seq_spec, seq_spec, seq_spec,
                  pl.BlockSpec((T, LANES), lambda b, h: (0, 0)),
                  pl.BlockSpec((hps, LANES, LANES), lambda b, h: (h, 0, 0)),
                  pl.BlockSpec((4, DIFF_QK), lambda b, h: (0, 0)),
                  pl.BlockSpec((1, DIFF_V), lambda b, h: (0, 0))],
        out_specs=seq_spec,
        out_shape=bt(DIFF_W),
        scratch_shapes=[pltpu.VMEM((hps, DIFF_VT_ROWS, T), BF16),
                        pltpu.VMEM((2, hps, 2 * LANES, 2 * TQ), BF16),
                        pltpu.VMEM((2, TK, TQ), F32),
                        st_scratch, st_scratch,
                        pltpu.VMEM((hps, 1, 2 * TQ), F32), pltpu.VMEM((hps, DIFF_ACC_ROWS, 2 * TQ), F32)],
        compiler_params=_params(("arbitrary", "arbitrary")),
    )(jnp.zeros((1,), jnp.int32), dq, dk, dv, dz, feat_w, _diff_query_features(), lam_v, diff_subln_g[0][None])

    n_all = N_GRP * TQ
    seq512 = pl.BlockSpec((T, 512), lambda b: (b, 0))
    seq128 = pl.BlockSpec((T, LANES), lambda b: (b, 0))
    st_nsa = pltpu.VMEM((N_KV, TK, n_all), F32)
    o_nsa = pl.pallas_call(
        _nsa_kernel,
        grid=(B,),
        in_specs=[seq512,
                  pl.BlockSpec((1, N_KV, n_half, LANES), lambda b: (b, 0, 0, 0)),
                  pl.BlockSpec((1, N_KV, NSA_D, n_half), lambda b: (b, 0, 0, 0)),
                  pl.BlockSpec((4, T, LANES), lambda b: (0, b, 0)),
                  seq128, seq128, seq128, seq512,
                  pl.BlockSpec((N_KV, N_GRP, LANES, LANES), lambda b: (0, 0, 0, 0)),
                  pl.BlockSpec((N_KV, LANES, LANES), lambda b: (0, 0, 0))],
        out_specs=seq512,
        out_shape=bt(NSA_W),
        scratch_shapes=[pltpu.VMEM((2, N_KV, NSA_VT_ROWS, T), BF16),
                        pltpu.VMEM((2, 2, N_KV, LANES, n_all), BF16),
                        pltpu.VMEM((2, N_KV, NSA_D, n_all), F32),
                        pltpu.VMEM((3, TK, TQ), F32),
                        st_nsa, st_nsa,
                        pltpu.VMEM((2, N_KV, 1, n_all), F32), pltpu.VMEM((2, N_KV, NSA_ACC_ROWS, n_all), F32)],
        compiler_params=_params(("arbitrary",)),
    )(nqp, kc, vct, kk, vs, vw, gates, nz, _nsa_query_features(), _overlap_t(T))

    out = pl.pallas_call(
        _out_kernel,
        grid=(n_rt,),
        in_specs=[row_spec(D), row_spec(DIFF_W), row_spec(NSA_W), full((D, D))],
        out_specs=row_spec(D),
        out_shape=jax.ShapeDtypeStruct((BT, D), x.dtype),
        compiler_params=_params(("parallel",)),
    )(x2, o_diff, o_nsa, w_o)
    return out.reshape(B, T, D)
```

```python
import math

import numpy as np
import jax
import jax.numpy as jnp
from jax import lax
from jax.experimental import pallas as pl
from jax.experimental.pallas import tpu as pltpu

F32 = jnp.float32
BF16 = jnp.bfloat16

D_MODEL = 1024
N_DIFF_HEADS = 4
DIFF_QK = 64
DIFF_V = 128
DIFF_W = 512
N_NSA_HEADS = 8
NSA_D = 64
N_KV = 2
N_GRP = 4
NSA_W = 512
CMP_BLOCK = 32
CMP_STRIDE = 16
CMP_HIDDEN = 256
SEL_BLOCK = 64
N_SEL_BLOCKS = 32
SEL_TOPK = 8
N_LOCAL = 2
FORCE_BONUS = 1000.0
WINDOW = 512
EPS = 1e-6
NEG_BIG = -1e30
LAM_INIT = 0.8 - 0.6 * math.exp(-0.3 * 0)
LOG2E = math.log2(math.e)

IN_MAIN = 3840
N_GATE = 24
IN_PAD = 3968
LANES = 128
SUBLANES = 8
BF16_ROWS = 16
SMALL_PARAM_ROWS = 32
SEL_MASK_BIAS = -32768.0

TM_PROJ = 512
TQ = 256
TK = 256
VMEM_LIMIT = 56 * 1024 * 1024
DIFF_ACC_ROWS = DIFF_V + SUBLANES
DIFF_VT_ROWS = DIFF_V + BF16_ROWS
DIFF_HEADS_PER_STEP = 4
NSA_ACC_ROWS = NSA_D + SUBLANES
NSA_VT_ROWS = NSA_D + BF16_ROWS


def _dot(a, b):
    return jnp.dot(a, b, preferred_element_type=F32)


def _sigmoid(y):
    return 1.0 / (1.0 + jnp.exp(-y))


def _seg_mean_sq(y, e):
    n = y.shape[1]
    y2 = (y * y).astype(BF16)
    if n == LANES:
        return _dot(y2, e[:LANES, :LANES])
    return jnp.concatenate([_dot(y2[:, c:c + 256], e) for c in range(0, n, 256)], axis=1)


def _seg_norm(y, e, gain):
    return y * lax.rsqrt(_seg_mean_sq(y, e) + EPS) * gain


def _transpose_bf16(a):
    return a.astype(F32).T.astype(BF16)


def _ones_rows(n_rows, width):
    return jnp.where(lax.broadcasted_iota(jnp.int32, (n_rows, width), 0) == 0, 1.0, 0.0).astype(BF16)


def _mask_cols(st, keep, width):
    n = st.shape[1]
    return jnp.concatenate([jnp.where(keep, st[:, a:a + width], -jnp.inf) for a in range(0, n, width)], axis=1)


def _online_softmax_step(st, vt_blk, m_ref, acc_ref):
    m_prev = m_ref[...]
    m_new = jnp.maximum(m_prev, jnp.max(st, axis=0, keepdims=True))
    alpha = jnp.exp2(m_prev - m_new)
    p = jnp.exp2(st - m_new).astype(BF16)
    pv = _dot(vt_blk, p)
    acc_ref[...] = alpha * acc_ref[...] + pv[:acc_ref.shape[0]]
    m_ref[...] = m_new


def _pipelined_sweep(n, n_chunks, qk, consume, st_a, st_b, interleave=True, first_scores_done=False, zero=None):
    chunks = range(n_chunks)

    def both(j_next, st_next, j_cur, st_cur):
        if interleave:
            for c in chunks:
                qk(j_next, st_next, c)
                consume(j_cur, st_cur, c)
        else:
            for c in chunks:
                qk(j_next, st_next, c)
            for c in chunks:
                consume(j_cur, st_cur, c)

    if not first_scores_done:
        for c in chunks:
            qk(0, st_a, c)

    def body(jj, carry):
        j = 2 * jj
        both(j + 1, st_b, j, st_a)
        both(j + 2, st_a, j + 1, st_b)
        return carry

    def tail_even():
        for c in chunks:
            consume(n, st_a, c)

    def tail_odd():
        both(n, st_b, n - 1, st_a)
        for c in chunks:
            consume(n, st_b, c)

    if zero is not None:
        lax.fori_loop(0, n // 2 + zero, body, 0)
        (tail_even if n % 2 == 0 else tail_odd)()
    elif isinstance(n, int):
        if n // 2 > 0:
            lax.fori_loop(0, n // 2, body, 0)
        (tail_even if n % 2 == 0 else tail_odd)()
    else:
        lax.fori_loop(0, n // 2, body, 0)
        pl.when(n % 2 == 0)(tail_even)
        pl.when(n % 2 == 1)(tail_odd)


def _proj_kernel(x_ref, g_ref, w_ref, e_ref, gq_ref, gk_ref, gnq_ref, gks_ref, gkw_ref, fs_ref, fw_ref,
                 dq_ref, dk_ref, dv_ref, dz_ref, nq_ref, kc_ref, vc_ref, kk_ref, vs_ref, vw_ref,
                 nz_ref, gate_ref):
    x = x_ref[...]
    ms = jnp.mean(x * x, axis=-1, keepdims=True)
    h = (x * lax.rsqrt(ms + EPS) * g_ref[...]).astype(BF16)
    e = e_ref[...]

    def proj(a, b):
        return _dot(h, w_ref[:, a:b])

    dq_ref[...] = _seg_norm(proj(0, 512), e, gq_ref[...]).astype(BF16)
    dk_ref[...] = _seg_norm(proj(512, 1024), e, gk_ref[...]).astype(BF16)
    dv_ref[...] = proj(1024, 1536).astype(BF16)
    y = proj(1536, 2048)
    dz_ref[...] = (y * _sigmoid(y)).astype(BF16)
    nq_ref[...] = _seg_norm(proj(2048, 2560), e, gnq_ref[...]).astype(BF16)
    y = proj(2560, 2816)
    kc_ref[...] = y[:, :LANES]
    vc_ref[...] = y[:, LANES:]

    low = lax.broadcasted_iota(jnp.int32, (1, LANES), 1) < 64
    y = proj(2816, 3072)
    kp = _seg_norm(y[:, :LANES], e, gks_ref[0:1, :]).astype(BF16)
    fs = fs_ref[...]
    kk_ref[0] = jnp.where(low, kp, fs)
    kk_ref[1] = jnp.where(low, fs, kp)
    vs_ref[...] = y[:, LANES:].astype(BF16)
    y = proj(3072, 3328)
    kp = _seg_norm(y[:, :LANES], e, gkw_ref[0:1, :]).astype(BF16)
    fw = fw_ref[...]
    kk_ref[2] = jnp.where(low, kp, fw)
    kk_ref[3] = jnp.where(low, fw, kp)
    vw_ref[...] = y[:, LANES:].astype(BF16)
    y = proj(3328, 3840)
    nz_ref[...] = (y * _sigmoid(y)).astype(BF16)
    gate_ref[...] = _sigmoid(proj(3840, 3968))


def _cmp_kernel(hk_ref, hv_ref, pa_ref, pb_ref, w1a_ref, w1b_ref, b1_ref, w2_ref, e_ref, gkc_ref,
                kc_ref, vct_ref):
    n_rows = hk_ref.shape[0] // CMP_STRIDE
    row_ok = lax.broadcasted_iota(jnp.int32, (n_rows, 1), 0) < (n_rows - 1)
    low = lax.broadcasted_iota(jnp.int32, (1, LANES), 1) < 64

    def mlp(h_ref, idx):
        hf = jnp.concatenate([h_ref[pl.ds(t, n_rows, stride=CMP_STRIDE), :] for t in range(CMP_STRIDE)], axis=1)
        ha = (hf + pa_ref[idx]).astype(BF16)
        hb = (hf + pb_ref[idx]).astype(BF16)
        a = _dot(ha, w1a_ref[idx])
        b = _dot(hb, w1b_ref[idx])
        hid = a + pltpu.roll(b, n_rows - 1, 0) + b1_ref[idx]
        hid = hid * _sigmoid(hid)
        return _dot(hid.astype(BF16), w2_ref[idx])

    yk = mlp(hk_ref, 0)
    yk = _seg_norm(yk, e_ref[...], gkc_ref[0:1, :])
    yk = jnp.where(row_ok, yk, 0.0).astype(BF16)
    zero = jnp.zeros_like(yk)
    kc_ref[0, 0] = jnp.where(low, yk, zero)
    kc_ref[0, 1] = jnp.where(low, zero, yk)
    yvt = jnp.where(row_ok, mlp(hv_ref, 1), 0.0).T.astype(BF16)
    vct_ref[0, 0] = yvt[:NSA_D]
    vct_ref[0, 1] = yvt[NSA_D:]


def _diff_kernel(zero_ref, q_ref, k_ref, v_ref, z_ref, fk_ref, qf_ref, lam_ref, gsub_ref, o_ref,
                 vt_ref, qq_ref, mask_ref, st_a, st_b, m_ref, acc_ref):
    seq = k_ref.shape[0]
    heads = range(DIFF_HEADS_PER_STEP)

    for hh in heads:
        vt_ref[hh, 0:DIFF_V, :] = _transpose_bf16(v_ref[:, hh * DIFF_V:(hh + 1) * DIFF_V])
        vt_ref[hh, DIFF_V:DIFF_VT_ROWS, :] = _ones_rows(BF16_ROWS, seq)
    causal = lax.broadcasted_iota(jnp.int32, (TK, TQ), 0) <= lax.broadcasted_iota(jnp.int32, (TK, TQ), 1)
    mask_ref[0] = jnp.zeros((TK, TQ), F32)
    mask_ref[1] = jnp.where(causal, 0.0, -jnp.inf)
    rows = lax.broadcasted_iota(jnp.int32, (LANES, TQ), 0)
    lam_v = lam_ref[0:4, 0:DIFF_QK]
    lam = (jnp.exp(jnp.sum(lam_v[0:1] * lam_v[1:2], axis=-1, keepdims=True))
           - jnp.exp(jnp.sum(lam_v[2:3] * lam_v[3:4], axis=-1, keepdims=True)) + LAM_INIT)

    def qk(i, j, st_ref, hh):
        k0 = j * TK if isinstance(j, int) else pl.multiple_of(j * TK, TK)
        diag = int(j == i) if isinstance(j, int) else jnp.where(j == i, 1, 0)
        mask = mask_ref[diag]
        keys = jnp.concatenate([k_ref[pl.ds(k0, TK), hh * LANES:(hh + 1) * LANES], fk_ref[pl.ds(k0, TK), :]],
                               axis=1)
        st = _dot(keys, qq_ref[i % 2, hh])
        st_ref[hh] = jnp.concatenate([st[:, a:a + TQ] + mask for a in range(0, 2 * TQ, TQ)], axis=1)

    def consume(j, st_ref, hh):
        k0 = j * TK if isinstance(j, int) else pl.multiple_of(j * TK, TK)
        _online_softmax_step(st_ref[hh], vt_ref[hh, :, pl.ds(k0, TK)], m_ref.at[hh], acc_ref.at[hh])

    def prepare(i, hh):
        qt = _transpose_bf16(q_ref[i * TQ:(i + 1) * TQ, hh * LANES:(hh + 1) * LANES])
        zero = jnp.zeros_like(qt)
        qq_ref[i % 2, hh, 0:LANES, :] = jnp.concatenate(
            [jnp.where(rows < 64, qt, zero), jnp.where(rows >= 64, qt, zero)], axis=1)
        qq_ref[i % 2, hh, LANES:2 * LANES, :] = jnp.concatenate([qf_ref[hh]] * (2 * TQ // LANES), axis=1)
        qk(i, 0, st_a, hh)

    def finalize(i, hh):
        acc = acc_ref[hh]
        o2 = acc[0:DIFF_V] / acc[DIFF_V:DIFF_V + 1]
        o = (o2[:, :TQ] - lam * o2[:, TQ:]).T
        ms = jnp.mean(o * o, axis=-1, keepdims=True)
        o = o * lax.rsqrt(ms + EPS) * gsub_ref[0:1, :] * (1.0 - LAM_INIT)
        sl = slice(hh * DIFF_V, (hh + 1) * DIFF_V)
        rs = slice(i * TQ, (i + 1) * TQ)
        o_ref[rs, sl] = (o * z_ref[rs, sl].astype(F32)).astype(BF16)

    def reset():
        m_ref[...] = jnp.full(m_ref.shape, NEG_BIG, F32)
        acc_ref[...] = jnp.zeros(acc_ref.shape, F32)

    reset()
    for hh in heads:
        prepare(0, hh)
    n_q = seq // TQ
    for i in range(n_q):
        _pipelined_sweep(i, DIFF_HEADS_PER_STEP, lambda j, st, hh: qk(i, j, st, hh), consume, st_a, st_b,
                         interleave=False, first_scores_done=True, zero=zero_ref[0])
        for hh in heads:
            if i + 1 < n_q:
                prepare(i + 1, hh)
            finalize(i, hh)
        reset()


def _nsa_kernel(q_ref, kc_ref, vct_ref, kk_ref, vs_ref, vw_ref, gate_ref, nz_ref, qf_ref, ovt_ref,
                o_ref, vt_ref, qq_ref, ocmp_ref, mask_ref, st_a, st_b, m_ref, acc_ref):
    seq = vs_ref.shape[0]
    n_q = seq // TQ
    SEL, WIN = 0, 1
    NO_MASK, CAUSAL, UPPER = 0, 1, 2
    kv_heads = range(N_KV)
    n_chunks = N_KV * N_GRP
    krow = lax.broadcasted_iota(jnp.int32, (TK, TQ), 0)
    qcol = lax.broadcasted_iota(jnp.int32, (TK, TQ), 1)
    rows = lax.broadcasted_iota(jnp.int32, (LANES, TQ), 0)
    bidx = lax.broadcasted_iota(jnp.int32, (N_SEL_BLOCKS, TQ), 0)
    row_f = bidx.astype(F32)

    for br, v_ref in ((SEL, vs_ref), (WIN, vw_ref)):
        vt = _transpose_bf16(v_ref[...])
        for kvh in kv_heads:
            vt_ref[br, kvh, 0:NSA_D, :] = vt[kvh * NSA_D:(kvh + 1) * NSA_D]
            vt_ref[br, kvh, NSA_D:NSA_VT_ROWS, :] = _ones_rows(BF16_ROWS, seq)
    mask_ref[NO_MASK] = jnp.zeros((TK, TQ), F32)
    mask_ref[CAUSAL] = jnp.where(krow <= qcol, 0.0, -jnp.inf)
    mask_ref[UPPER] = jnp.where(krow > qcol, 0.0, -jnp.inf)

    def reset():
        m_ref[...] = jnp.full(m_ref.shape, NEG_BIG, F32)
        acc_ref[...] = jnp.zeros(acc_ref.shape, F32)

    def first_virtual(i):
        return 2 - min(i, 2)

    def source(i, u):
        v = u + first_virtual(i)
        if isinstance(u, int):
            if v < 3:
                return WIN, (i - 2 + v) * TK, (UPPER if v == 0 else CAUSAL if v == 2 else NO_MASK)
            return SEL, (v - 3) * TK, (CAUSAL if v - 3 == i else NO_MASK)
        is_win = v < 3
        src = jnp.where(is_win, WIN, SEL)
        kb = jnp.where(is_win, i - 2 + v, v - 3)
        mt = jnp.where(is_win, jnp.where(v == 0, UPPER, jnp.where(v == 2, CAUSAL, NO_MASK)),
                       jnp.where(kb == i, CAUSAL, NO_MASK))
        return src, pl.multiple_of(kb * TK, TK), mt

    def qk(i, u, st_ref, c):
        kvh, g = divmod(c, N_GRP)
        src, k0, mt = source(i, u)
        cs = slice(g * TQ, (g + 1) * TQ)
        st_ref[kvh, :, cs] = (_dot(kk_ref[2 * src + kvh, pl.ds(k0, TK), :], qq_ref[i % 2, src, kvh, :, cs])
                              + mask_ref[mt])

    def consume(i, u, st_ref, c):
        kvh, g = divmod(c, N_GRP)
        src, k0, _ = source(i, u)
        cs = slice(g * TQ, (g + 1) * TQ)
        _online_softmax_step(st_ref[kvh, :, cs], vt_ref[src, kvh, :, pl.ds(k0, TK)],
                             m_ref.at[src, kvh, :, cs], acc_ref.at[src, kvh, :, cs])

    def prepare(i, interleaved):
        rs = slice(i * TQ, (i + 1) * TQ)
        slot = i % 2
        qts = [_transpose_bf16(q_ref[rs, g * LANES:(g + 1) * LANES]) for g in range(N_GRP)]
        qbs = []
        for kvh in kv_heads:
            half = (rows < 64) if kvh == 0 else (rows >= 64)
            qb = jnp.concatenate(
                [jnp.where(half, qts[g], jnp.concatenate([qf_ref[kvh, g]] * (TQ // LANES), axis=1))
                 for g in range(N_GRP)], axis=1)
            qq_ref[slot, WIN, kvh] = qb
            qq_ref[slot, SEL, kvh] = qb
            qbs.append(qb)
        qpos = i * TQ + lax.broadcasted_iota(jnp.int32, (1, TQ), 1)

        valid = (rows * CMP_STRIDE + (CMP_BLOCK - 1)) <= qpos
        blk_t = qpos >> 6
        valid_b = bidx <= blk_t
        forced = valid_b & ((bidx == 0) | (bidx >= blk_t - (N_LOCAL - 1)))
        scores = []
        for kvh in kv_heads:
            st = _dot(kc_ref[0, kvh], qbs[kvh])
            p_parts = []
            for g in range(N_GRP):
                sg = jnp.where(valid, st[:, g * TQ:(g + 1) * TQ], NEG_BIG)
                mx = jnp.max(sg, axis=0, keepdims=True)
                ex = jnp.where(valid, jnp.exp2(sg - mx), 0.0)
                den = jnp.sum(ex, axis=0, keepdims=True)
                p_parts.append(ex * jnp.where(den > 0.0, 1.0 / den, 0.0))
            pt = jnp.concatenate(p_parts, axis=1)
            ocmp_ref[slot, kvh] = _dot(vct_ref[0, kvh], pt.astype(BF16))
            psum = p_parts[0] + p_parts[1] + p_parts[2] + p_parts[3]
            ovt = ovt_ref[kvh]
            p_hi = psum.astype(BF16)
            p_lo = (psum - p_hi.astype(F32)).astype(BF16)
            seg0 = 96 - 64 * kvh
            imp = (_dot(ovt, p_hi) + _dot(ovt, p_lo))[seg0:seg0 + N_SEL_BLOCKS]
            scores.append(jnp.where(valid_b, imp, -1.0) + jnp.where(forced, FORCE_BONUS, 0.0))

        chosen = [jnp.zeros((N_SEL_BLOCKS, TQ), F32) for _ in kv_heads]
        for r in range(max(SEL_TOPK, n_chunks, len(interleaved))):
            if r < SEL_TOPK:
                for kvh in kv_heads:
                    best = jnp.max(scores[kvh], axis=0, keepdims=True)
                    first = jnp.min(jnp.where(scores[kvh] == best, row_f, 1e9), axis=0, keepdims=True)
                    pick = row_f == first
                    chosen[kvh] = jnp.where(pick, 1.0, chosen[kvh])
                    scores[kvh] = jnp.where(pick, -jnp.inf, scores[kvh])
            if r < n_chunks:
                qk(i, 0, st_a, r)
            if r < len(interleaved):
                interleaved[r]()
        for kvh in kv_heads:
            sel_bias = jnp.where(chosen[kvh] == 0.0, SEL_MASK_BIAS, 0.0).astype(BF16)
            seg0 = 96 - 64 * kvh
            for g in range(N_GRP):
                qq_ref[slot, SEL, kvh, seg0:seg0 + N_SEL_BLOCKS, g * TQ:(g + 1) * TQ] = sel_bias

    gate_rows = {}

    def output_piece(i, g):
        rs = slice(i * TQ, (i + 1) * TQ)
        if i not in gate_rows:
            gate_rows[i] = gate_ref[rs, :].T
        gt = gate_rows[i]
        cs = slice(g * TQ, (g + 1) * TQ)
        ots = []
        for kvh in kv_heads:
            gr = [gt[br * 8 + kvh * 4 + g:br * 8 + kvh * 4 + g + 1] for br in range(3)]
            a_s = acc_ref[SEL, kvh, :, cs]
            a_w = acc_ref[WIN, kvh, :, cs]
            ots.append(gr[0] * ocmp_ref[i % 2, kvh, :, cs]
                       + (gr[1] / a_s[NSA_D:NSA_D + 1]) * a_s[:NSA_D]
                       + (gr[2] / a_w[NSA_D:NSA_D + 1]) * a_w[:NSA_D])
        og = jnp.concatenate(ots, axis=0).T
        sl = slice(g * LANES, (g + 1) * LANES)
        o_ref[rs, sl] = (og * nz_ref[rs, sl].astype(F32)).astype(BF16)

    reset()
    prepare(0, [])
    for i in range(n_q):
        _pipelined_sweep(3 + i - first_virtual(i), n_chunks,
                         lambda u, st, c: qk(i, u, st, c), lambda u, st, c: consume(i, u, st, c),
                         st_a, st_b, first_scores_done=True)
        pieces = [(lambda g=g: output_piece(i, g)) for g in range(N_GRP)]
        if i + 1 < n_q:
            prepare(i + 1, pieces)
        else:
            for piece in pieces:
                piece()
        reset()


def _out_kernel(x_ref, od_ref, on_ref, w_ref, o_ref):
    acc = _dot(od_ref[...], w_ref[:DIFF_W, :]) + _dot(on_ref[...], w_ref[DIFF_W:, :])
    o_ref[...] = x_ref[...] + acc


def _params(sem):
    return pltpu.CompilerParams(dimension_semantics=sem, vmem_limit_bytes=VMEM_LIMIT)


def _bf16_pieces(v):
    out = []
    r = np.float32(v)
    for _ in range(3):
        p = np.asarray(r).astype(BF16).astype(np.float32)
        out.append(float(p))
        r = np.float32(r - p)
    return out


def _slope_features(slope):
    f = []
    for piece in _bf16_pieces(slope * LOG2E):
        f += [piece * 256.0, piece]
    return f


def _key_features(seq, with_blocks):
    t = np.arange(seq)
    f = np.zeros((seq, 64), np.float32)
    for r in range(3):
        f[:, 2 * r] = t // 256
        f[:, 2 * r + 1] = t % 256
    if with_blocks:
        f[t, 32 + t // SEL_BLOCK] = 1.0
    return jnp.asarray(np.concatenate([f, f], axis=1), dtype=BF16)


def _nsa_query_features():
    f = np.zeros((N_KV, N_GRP, LANES), np.float32)
    for kvh in range(N_KV):
        base = 64 if kvh == 0 else 0
        for g in range(N_GRP):
            f[kvh, g, base:base + 6] = _slope_features(2.0 ** (-(kvh * N_GRP + g + 1)))
    return jnp.asarray(np.repeat(f[..., None], LANES, axis=-1), dtype=BF16)


def _diff_query_features():
    f = np.zeros((N_DIFF_HEADS, LANES), np.float32)
    for h in range(N_DIFF_HEADS):
        f[h, 0:6] = _slope_features(2.0 ** (-8.0 * (h + 1) / N_DIFF_HEADS))
    return jnp.asarray(np.repeat(f[..., None], LANES, axis=-1), dtype=BF16)


def _overlap_t(seq):
    n_cmp = (seq - CMP_BLOCK) // CMP_STRIDE + 1
    n_sb = seq // SEL_BLOCK
    cs = np.arange(n_cmp) * CMP_STRIDE
    ss = np.arange(n_sb) * SEL_BLOCK
    ovl = ((cs[:, None] < ss[None, :] + SEL_BLOCK) & (cs[:, None] + CMP_BLOCK > ss[None, :])).astype(np.float32)
    out = np.zeros((N_KV, LANES, LANES), np.float32)
    out[0, 96:96 + n_sb, :n_cmp] = ovl.T
    out[1, 32:32 + n_sb, :n_cmp] = ovl.T
    return jnp.asarray(out, dtype=BF16)


def _to_pair_order(a, axis):
    shp = a.shape
    a = a.reshape(shp[:axis] + (N_KV, N_GRP, NSA_D) + shp[axis + 1:])
    a = jnp.swapaxes(a, axis, axis + 1)
    return a.reshape(shp)


def kernel(x, norm_g, w_in, diff_q_norm_g, diff_k_norm_g, diff_lambda_q1, diff_lambda_k1, diff_lambda_q2,
           diff_lambda_k2, diff_subln_g, nsa_q_norm_g, nsa_k_norm_g, cmp_pos, cmp_w1, cmp_b1, cmp_w2, w_out):
    B, T, D = x.shape
    BT = B * T
    assert D == D_MODEL and TQ == TK and T == N_SEL_BLOCKS * SEL_BLOCK and T % TM_PROJ == 0
    n_half = T // CMP_STRIDE
    nq = T // TQ

    w = w_in[0].astype(BF16)
    w_p = jnp.concatenate([w[:, :2048], _to_pair_order(w[:, 2048:2560], 1), w[:, 2560:3328],
                           _to_pair_order(w[:, 3328:3840], 1), w[:, IN_MAIN:],
                           jnp.zeros((D, IN_PAD - IN_MAIN - N_GATE), BF16)], axis=1)
    w_ob = w_out[0].astype(BF16)
    w_o = jnp.concatenate([w_ob[:DIFF_W], _to_pair_order(w_ob[DIFF_W:], 0)], axis=0)
    seg = np.arange(256) // 64
    e_mat = jnp.asarray((seg[:, None] == seg[None, :]).astype(np.float32) / 64.0, dtype=BF16)

    def small(a):
        a = a.astype(F32)
        return jnp.pad(a, ((0, SMALL_PARAM_ROWS - a.shape[0]), (0, LANES - a.shape[1])))

    gq = (jnp.tile(diff_q_norm_g[0], 8) * (DIFF_QK ** -0.5 * LOG2E))[None]
    gk = jnp.tile(diff_k_norm_g[0], 8)[None]
    gnq = (jnp.tile(nsa_q_norm_g[0], 8) * (NSA_D ** -0.5 * LOG2E))[None]
    gkc = small(jnp.tile(nsa_k_norm_g[0, 0], 2)[None])
    gks = small(jnp.tile(nsa_k_norm_g[0, 1], 2)[None])
    gkw = small(jnp.tile(nsa_k_norm_g[0, 2], 2)[None])
    feat_s = _key_features(T, True)
    feat_w = _key_features(T, False)

    x2 = x.reshape(BT, D)
    n_rt = BT // TM_PROJ
    rt_per_seq = T // TM_PROJ
    row_spec = lambda n: pl.BlockSpec((TM_PROJ, n), lambda i: (i, 0))
    full = lambda shp: pl.BlockSpec(shp, lambda i: (0,) * len(shp))
    feat_spec = pl.BlockSpec((TM_PROJ, LANES), lambda i: (i % rt_per_seq, 0))
    bt = lambda n, dt=BF16: jax.ShapeDtypeStruct((BT, n), dt)
    outs = pl.pallas_call(
        _proj_kernel,
        grid=(n_rt,),
        in_specs=[row_spec(D), full((1, D)), full((D, IN_PAD)), full((256, 256)),
                  full((1, 512)), full((1, 512)), full((1, 512)),
                  full((SMALL_PARAM_ROWS, LANES)), full((SMALL_PARAM_ROWS, LANES)),
                  feat_spec, feat_spec],
        out_specs=[row_spec(512), row_spec(512), row_spec(512), row_spec(512), row_spec(512),
                   row_spec(LANES), row_spec(LANES), pl.BlockSpec((4, TM_PROJ, LANES), lambda i: (0, i, 0)),
                   row_spec(LANES), row_spec(LANES), row_spec(512), row_spec(LANES)],
        out_shape=[bt(512), bt(512), bt(512), bt(512), bt(512), bt(LANES, F32), bt(LANES, F32),
                   jax.ShapeDtypeStruct((4, BT, LANES), BF16), bt(LANES), bt(LANES), bt(512), bt(LANES, F32)],
        compiler_params=_params(("parallel",)),
    )(x2, norm_g[0][None], w_p, e_mat, gq, gk, gnq, gks, gkw, feat_s, feat_w)
    dq, dk, dv, dz, nqp, kcr, vcr, kk, vs, vw, nz, gates = outs

    eye2 = jnp.eye(N_KV, dtype=F32)

    def pair_w1(w1):
        wh = w1.reshape(2, 16, 1, NSA_D, 1, CMP_HIDDEN) * eye2[None, None, :, None, :, None]
        return wh.reshape(2, 16 * N_KV * NSA_D, N_KV * CMP_HIDDEN).astype(BF16)

    def pair_w2(w2):
        z = w2[None, :, None, :] * eye2[:, None, :, None]
        return z.reshape(N_KV * CMP_HIDDEN, N_KV * NSA_D).astype(BF16)

    def pair_pos(p):
        ph = p.reshape(2, 16, 1, NSA_D)
        return jnp.broadcast_to(ph, (2, 16, N_KV, NSA_D)).reshape(2, 1, 16 * N_KV * NSA_D)

    w1k = pair_w1(cmp_w1[0, 0]); w1v = pair_w1(cmp_w1[0, 1])
    w1a = jnp.stack([w1k[0], w1v[0]]); w1b = jnp.stack([w1k[1], w1v[1]])
    w2p = jnp.stack([pair_w2(cmp_w2[0, 0]), pair_w2(cmp_w2[0, 1])])
    pk = pair_pos(cmp_pos[0, 0]); pv = pair_pos(cmp_pos[0, 1])
    pa = jnp.stack([pk[0], pv[0]]); pb = jnp.stack([pk[1], pv[1]])
    b1p = jnp.stack([jnp.tile(cmp_b1[0, 0], 2)[None], jnp.tile(cmp_b1[0, 1], 2)[None]])
    hspec = pl.BlockSpec((T, LANES), lambda b: (b, 0))
    kc, vct = pl.pallas_call(
        _cmp_kernel,
        grid=(B,),
        in_specs=[hspec, hspec, full((2, 1, 2048)), full((2, 1, 2048)), full((2, 2048, 512)),
                  full((2, 2048, 512)), full((2, 1, 512)), full((2, 512, LANES)), full((256, 256)),
                  full((SMALL_PARAM_ROWS, LANES))],
        out_specs=[pl.BlockSpec((1, 2, n_half, LANES), lambda b: (b, 0, 0, 0)),
                   pl.BlockSpec((1, 2, NSA_D, n_half), lambda b: (b, 0, 0, 0))],
        out_shape=[jax.ShapeDtypeStruct((B, 2, n_half, LANES), BF16),
                   jax.ShapeDtypeStruct((B, 2, NSA_D, n_half), BF16)],
        compiler_params=_params(("parallel",)),
    )(kcr, vcr, pa, pb, w1a, w1b, b1p, w2p, e_mat, gkc)

    lam_v = small(jnp.concatenate([diff_lambda_q1, diff_lambda_k1, diff_lambda_q2, diff_lambda_k2], axis=0))
    hps = DIFF_HEADS_PER_STEP
    seq_spec = pl.BlockSpec((T, hps * LANES), lambda b, h: (b, h))
    st_scratch = pltpu.VMEM((hps, TK, 2 * TQ), F32)
    o_diff = pl.pallas_call(
        _diff_kernel,
        grid=(B, N_DIFF_HEADS // hps),
        in_specs=[pl.BlockSpec(memory_space=pltpu.SMEM), seq_spec, seq_spec, seq_spec, seq_spec,
                  pl.BlockSpec((T, LANES), lambda b, h: (0, 0)),
                  pl.BlockSpec((hps, LANES, LANES), lambda b, h: (h, 0, 0)),
                  pl.BlockSpec((SMALL_PARAM_ROWS, LANES), lambda b, h: (0, 0)),
                  pl.BlockSpec((SMALL_PARAM_ROWS, LANES), lambda b, h: (0, 0))],
        out_specs=seq_spec,
        out_shape=bt(DIFF_W),
        scratch_shapes=[pltpu.VMEM((hps, DIFF_VT_ROWS, T), BF16),
                        pltpu.VMEM((2, hps, 2 * LANES, 2 * TQ), BF16),
                        pltpu.VMEM((2, TK, TQ), F32),
                        st_scratch, st_scratch,
                        pltpu.VMEM((hps, 1, 2 * TQ), F32), pltpu.VMEM((hps, DIFF_ACC_ROWS, 2 * TQ), F32)],
        compiler_params=_params(("arbitrary", "arbitrary")),
    )(jnp.zeros((1,), jnp.int32), dq, dk, dv, dz, feat_w, _diff_query_features(), lam_v, small(diff_subln_g[0][None]))

    n_all = N_GRP * TQ
    seq512 = pl.BlockSpec((T, 512), lambda b: (b, 0))
    seq128 = pl.BlockSpec((T, LANES), lambda b: (b, 0))
    st_nsa = pltpu.VMEM((N_KV, TK, n_all), F32)
    o_nsa = pl.pallas_call(
        _nsa_kernel,
        grid=(B,),
        in_specs=[seq512,
                  pl.BlockSpec((1, N_KV, n_half, LANES), lambda b: (b, 0, 0, 0)),
                  pl.BlockSpec((1, N_KV, NSA_D, n_half), lambda b: (b, 0, 0, 0)),
                  pl.BlockSpec((4, T, LANES), lambda b: (0, b, 0)),
                  seq128, seq128, seq128, seq512,
                  pl.BlockSpec((N_KV, N_GRP, LANES, LANES), lambda b: (0, 0, 0, 0)),
                  pl.BlockSpec((N_KV, LANES, LANES), lambda b: (0, 0, 0))],
        out_specs=seq512,
        out_shape=bt(NSA_W),
        scratch_shapes=[pltpu.VMEM((2, N_KV, NSA_VT_ROWS, T), BF16),
                        pltpu.VMEM((2, 2, N_KV, LANES, n_all), BF16),
                        pltpu.VMEM((2, N_KV, NSA_D, n_all), F32),
                        pltpu.VMEM((3, TK, TQ), F32),
                        st_nsa, st_nsa,
                        pltpu.VMEM((2, N_KV, 1, n_all), F32), pltpu.VMEM((2, N_KV, NSA_ACC_ROWS, n_all), F32)],
        compiler_params=_params(("arbitrary",)),
    )(nqp, kc, vct, kk, vs, vw, gates, nz, _nsa_query_features(), _overlap_t(T))

    out = pl.pallas_call(
        _out_kernel,
        grid=(n_rt,),
        in_specs=[row_spec(D), row_spec(DIFF_W), row_spec(NSA_W), full((D, D))],
        out_specs=row_spec(D),
        out_shape=jax.ShapeDtypeStruct((BT, D), x.dtype),
        compiler_params=_params(("parallel",)),
    )(x2, o_diff, o_nsa, w_o)
    return out.reshape(B, T, D)
```

```python
import math

import numpy as np
import jax
import jax.numpy as jnp
from jax import lax
from jax.experimental import pallas as pl
from jax.experimental.pallas import tpu as pltpu

F32 = jnp.float32
BF16 = jnp.bfloat16

D_MODEL = 1024
N_DIFF_HEADS = 4
DIFF_QK = 64
DIFF_V = 128
DIFF_W = 512
N_NSA_HEADS = 8
NSA_D = 64
N_KV = 2
N_GRP = 4
NSA_W = 512
CMP_BLOCK = 32
CMP_STRIDE = 16
CMP_HIDDEN = 256
SEL_BLOCK = 64
N_SEL_BLOCKS = 32
SEL_TOPK = 8
N_LOCAL = 2
FORCE_BONUS = 1000.0
WINDOW = 512
EPS = 1e-6
NEG_BIG = -1e30
LAM_INIT = 0.8 - 0.6 * math.exp(-0.3 * 0)
LOG2E = math.log2(math.e)

IN_MAIN = 3840
N_GATE = 24
IN_PAD = 3968
LANES = 128
SUBLANES = 8
BF16_ROWS = 16
SMALL_PARAM_ROWS = 32
SEL_MASK_BIAS = -32768.0

TM_PROJ = 512
TM_OUT = 1024
TQ = 256
TK = 256
VMEM_LIMIT = 56 * 1024 * 1024
DIFF_ACC_ROWS = DIFF_V + SUBLANES
DIFF_VT_ROWS = DIFF_V + BF16_ROWS
DIFF_HEADS_PER_STEP = 4
NSA_ACC_ROWS = NSA_D + SUBLANES
NSA_VT_ROWS = NSA_D + BF16_ROWS


def _dot(a, b):
    return jnp.dot(a, b, preferred_element_type=F32)


def _sigmoid(y):
    return 1.0 / (1.0 + jnp.exp(-y))


def _seg_mean_sq(y, e):
    n = y.shape[1]
    y2 = (y * y).astype(BF16)
    if n == LANES:
        return _dot(y2, e[:LANES, :LANES])
    return jnp.concatenate([_dot(y2[:, c:c + 256], e) for c in range(0, n, 256)], axis=1)


def _seg_norm(y, e, gain):
    return y * lax.rsqrt(_seg_mean_sq(y, e) + EPS) * gain


def _transpose_bf16(a):
    return a.astype(F32).T.astype(BF16)


def _ones_rows(n_rows, width):
    return jnp.where(lax.broadcasted_iota(jnp.int32, (n_rows, width), 0) == 0, 1.0, 0.0).astype(BF16)


def _mask_cols(st, keep, width):
    n = st.shape[1]
    return jnp.concatenate([jnp.where(keep, st[:, a:a + width], -jnp.inf) for a in range(0, n, width)], axis=1)


def _online_softmax_step(st, vt_blk, m_ref, acc_ref):
    m_prev = m_ref[...]
    m_new = jnp.maximum(m_prev, jnp.max(st, axis=0, keepdims=True))
    alpha = jnp.exp2(m_prev - m_new)
    p = jnp.exp2(st - m_new).astype(BF16)
    pv = _dot(vt_blk, p)
    acc_ref[...] = alpha * acc_ref[...] + pv[:acc_ref.shape[0]]
    m_ref[...] = m_new


def _pipelined_sweep(n, n_chunks, qk, consume, st_a, st_b, interleave=True, first_scores_done=False, zero=None):
    chunks = range(n_chunks)

    def both(j_next, st_next, j_cur, st_cur):
        if interleave:
            for c in chunks:
                qk(j_next, st_next, c)
                consume(j_cur, st_cur, c)
        else:
            for c in chunks:
                qk(j_next, st_next, c)
            for c in chunks:
                consume(j_cur, st_cur, c)

    if not first_scores_done:
        for c in chunks:
            qk(0, st_a, c)

    def body(jj, carry):
        j = 2 * jj
        both(j + 1, st_b, j, st_a)
        both(j + 2, st_a, j + 1, st_b)
        return carry

    def tail_even():
        for c in chunks:
            consume(n, st_a, c)

    def tail_odd():
        both(n, st_b, n - 1, st_a)
        for c in chunks:
            consume(n, st_b, c)

    if zero is not None:
        lax.fori_loop(0, n // 2 + zero, body, 0)
        (tail_even if n % 2 == 0 else tail_odd)()
    elif isinstance(n, int):
        if n // 2 > 0:
            lax.fori_loop(0, n // 2, body, 0)
        (tail_even if n % 2 == 0 else tail_odd)()
    else:
        lax.fori_loop(0, n // 2, body, 0)
        pl.when(n % 2 == 0)(tail_even)
        pl.when(n % 2 == 1)(tail_odd)


def _proj_kernel(x_ref, g_ref, w_ref, e_ref, gq_ref, gk_ref, gnq_ref, gks_ref, gkw_ref, fs_ref, fw_ref,
                 dq_ref, dk_ref, dv_ref, dz_ref, nq_ref, kc_ref, vc_ref, kk_ref, vs_ref, vw_ref,
                 nz_ref, gate_ref):
    x = x_ref[...]
    ms = jnp.mean(x * x, axis=-1, keepdims=True)
    h = (x * lax.rsqrt(ms + EPS) * g_ref[...]).astype(BF16)
    e = e_ref[...]

    def proj(a, b):
        return _dot(h, w_ref[:, a:b])

    dq_ref[...] = _seg_norm(proj(0, 512), e, gq_ref[...]).astype(BF16)
    dk_ref[...] = _seg_norm(proj(512, 1024), e, gk_ref[...]).astype(BF16)
    dv_ref[...] = proj(1024, 1536).astype(BF16)
    y = proj(1536, 2048)
    dz_ref[...] = (y * _sigmoid(y)).astype(BF16)
    nq_ref[...] = _seg_norm(proj(2048, 2560), e, gnq_ref[...]).astype(BF16)
    y = proj(2560, 2816)
    kc_ref[...] = y[:, :LANES]
    vc_ref[...] = y[:, LANES:]

    low = lax.broadcasted_iota(jnp.int32, (1, LANES), 1) < 64
    y = proj(2816, 3072)
    kp = _seg_norm(y[:, :LANES], e, gks_ref[0:1, :]).astype(BF16)
    fs = fs_ref[...]
    kk_ref[0] = jnp.where(low, kp, fs)
    kk_ref[1] = jnp.where(low, fs, kp)
    vs_ref[...] = y[:, LANES:].astype(BF16)
    y = proj(3072, 3328)
    kp = _seg_norm(y[:, :LANES], e, gkw_ref[0:1, :]).astype(BF16)
    fw = fw_ref[...]
    kk_ref[2] = jnp.where(low, kp, fw)
    kk_ref[3] = jnp.where(low, fw, kp)
    vw_ref[...] = y[:, LANES:].astype(BF16)
    y = proj(3328, 3840)
    nz_ref[...] = (y * _sigmoid(y)).astype(BF16)
    gate_ref[...] = _sigmoid(proj(3840, 3968))


def _cmp_kernel(hk_ref, hv_ref, pa_ref, pb_ref, w1a_ref, w1b_ref, b1_ref, w2_ref, e_ref, gkc_ref,
                kc_ref, vct_ref):
    n_rows = hk_ref.shape[0] // CMP_STRIDE
    row_ok = lax.broadcasted_iota(jnp.int32, (n_rows, 1), 0) < (n_rows - 1)
    low = lax.broadcasted_iota(jnp.int32, (1, LANES), 1) < 64

    def mlp(h_ref, idx):
        hf = jnp.concatenate([h_ref[pl.ds(t, n_rows, stride=CMP_STRIDE), :] for t in range(CMP_STRIDE)], axis=1)
        ha = (hf + pa_ref[idx]).astype(BF16)
        hb = (hf + pb_ref[idx]).astype(BF16)
        a = _dot(ha, w1a_ref[idx])
        b = _dot(hb, w1b_ref[idx])
        hid = a + pltpu.roll(b, n_rows - 1, 0) + b1_ref[idx]
        hid = hid * _sigmoid(hid)
        return _dot(hid.astype(BF16), w2_ref[idx])

    yk = mlp(hk_ref, 0)
    yk = _seg_norm(yk, e_ref[...], gkc_ref[0:1, :])
    yk = jnp.where(row_ok, yk, 0.0).astype(BF16)
    zero = jnp.zeros_like(yk)
    kc_ref[0, 0] = jnp.where(low, yk, zero)
    kc_ref[0, 1] = jnp.where(low, zero, yk)
    yvt = jnp.where(row_ok, mlp(hv_ref, 1), 0.0).T.astype(BF16)
    vct_ref[0, 0] = yvt[:NSA_D]
    vct_ref[0, 1] = yvt[NSA_D:]


def _diff_kernel(zero_ref, q_ref, k_ref, v_ref, z_ref, fk_ref, qf_ref, lam_ref, gsub_ref, o_ref,
                 vt_ref, qq_ref, mask_ref, st_a, st_b, m_ref, acc_ref):
    seq = k_ref.shape[0]
    heads = range(DIFF_HEADS_PER_STEP)

    for hh in heads:
        vt_ref[hh, 0:DIFF_V, :] = _transpose_bf16(v_ref[:, hh * DIFF_V:(hh + 1) * DIFF_V])
        vt_ref[hh, DIFF_V:DIFF_VT_ROWS, :] = _ones_rows(BF16_ROWS, seq)
    causal = lax.broadcasted_iota(jnp.int32, (TK, TQ), 0) <= lax.broadcasted_iota(jnp.int32, (TK, TQ), 1)
    mask_ref[0] = jnp.zeros((TK, TQ), F32)
    mask_ref[1] = jnp.where(causal, 0.0, -jnp.inf)
    rows = lax.broadcasted_iota(jnp.int32, (LANES, TQ), 0)
    lam_v = lam_ref[0:4, 0:DIFF_QK]
    lam = (jnp.exp(jnp.sum(lam_v[0:1] * lam_v[1:2], axis=-1, keepdims=True))
           - jnp.exp(jnp.sum(lam_v[2:3] * lam_v[3:4], axis=-1, keepdims=True)) + LAM_INIT)

    def qk(i, j, st_ref, hh):
        k0 = j * TK if isinstance(j, int) else pl.multiple_of(j * TK, TK)
        diag = int(j == i) if isinstance(j, int) else jnp.where(j == i, 1, 0)
        mask = mask_ref[diag]
        keys = jnp.concatenate([k_ref[pl.ds(k0, TK), hh * LANES:(hh + 1) * LANES], fk_ref[pl.ds(k0, TK), :]],
                               axis=1)
        st = _dot(keys, qq_ref[i % 2, hh])
        st_ref[hh] = jnp.concatenate([st[:, a:a + TQ] + mask for a in range(0, 2 * TQ, TQ)], axis=1)

    def consume(j, st_ref, hh):
        k0 = j * TK if isinstance(j, int) else pl.multiple_of(j * TK, TK)
        _online_softmax_step(st_ref[hh], vt_ref[hh, :, pl.ds(k0, TK)], m_ref.at[hh], acc_ref.at[hh])

    def prepare(i, hh):
        qt = _transpose_bf16(q_ref[i * TQ:(i + 1) * TQ, hh * LANES:(hh + 1) * LANES])
        zero = jnp.zeros_like(qt)
        qq_ref[i % 2, hh, 0:LANES, :] = jnp.concatenate(
            [jnp.where(rows < 64, qt, zero), jnp.where(rows >= 64, qt, zero)], axis=1)
        qq_ref[i % 2, hh, LANES:2 * LANES, :] = jnp.concatenate([qf_ref[hh]] * (2 * TQ // LANES), axis=1)
        qk(i, 0, st_a, hh)

    def finalize(i, hh):
        acc = acc_ref[hh]
        o2 = acc[0:DIFF_V] / acc[DIFF_V:DIFF_V + 1]
        o = (o2[:, :TQ] - lam * o2[:, TQ:]).T
        ms = jnp.mean(o * o, axis=-1, keepdims=True)
        o = o * lax.rsqrt(ms + EPS) * gsub_ref[0:1, :] * (1.0 - LAM_INIT)
        sl = slice(hh * DIFF_V, (hh + 1) * DIFF_V)
        rs = slice(i * TQ, (i + 1) * TQ)
        o_ref[rs, sl] = (o * z_ref[rs, sl].astype(F32)).astype(BF16)

    def reset():
        m_ref[...] = jnp.full(m_ref.shape, NEG_BIG, F32)
        acc_ref[...] = jnp.zeros(acc_ref.shape, F32)

    reset()
    for hh in heads:
        prepare(0, hh)
    n_q = seq // TQ
    for i in range(n_q):
        _pipelined_sweep(i, DIFF_HEADS_PER_STEP, lambda j, st, hh: qk(i, j, st, hh), consume, st_a, st_b,
                         interleave=True, first_scores_done=True, zero=zero_ref[0])
        for hh in heads:
            if i + 1 < n_q:
                prepare(i + 1, hh)
            finalize(i, hh)
        reset()


def _nsa_kernel(q_ref, kc_ref, vct_ref, kk_ref, vs_ref, vw_ref, gate_ref, nz_ref, qf_ref, ovt_ref,
                o_ref, vt_ref, qq_ref, ocmp_ref, mask_ref, st_a, st_b, m_ref, acc_ref):
    seq = vs_ref.shape[0]
    n_q = seq // TQ
    SEL, WIN = 0, 1
    NO_MASK, CAUSAL, UPPER = 0, 1, 2
    kv_heads = range(N_KV)
    n_chunks = N_KV * N_GRP
    krow = lax.broadcasted_iota(jnp.int32, (TK, TQ), 0)
    qcol = lax.broadcasted_iota(jnp.int32, (TK, TQ), 1)
    rows = lax.broadcasted_iota(jnp.int32, (LANES, TQ), 0)
    bidx = lax.broadcasted_iota(jnp.int32, (N_SEL_BLOCKS, TQ), 0)
    row_f = bidx.astype(F32)

    for br, v_ref in ((SEL, vs_ref), (WIN, vw_ref)):
        vt = _transpose_bf16(v_ref[...])
        for kvh in kv_heads:
            vt_ref[br, kvh, 0:NSA_D, :] = vt[kvh * NSA_D:(kvh + 1) * NSA_D]
            vt_ref[br, kvh, NSA_D:NSA_VT_ROWS, :] = _ones_rows(BF16_ROWS, seq)
    mask_ref[NO_MASK] = jnp.zeros((TK, TQ), F32)
    mask_ref[CAUSAL] = jnp.where(krow <= qcol, 0.0, -jnp.inf)
    mask_ref[UPPER] = jnp.where(krow > qcol, 0.0, -jnp.inf)

    def reset():
        m_ref[...] = jnp.full(m_ref.shape, NEG_BIG, F32)
        acc_ref[...] = jnp.zeros(acc_ref.shape, F32)

    def first_virtual(i):
        return 2 - min(i, 2)

    def source(i, u):
        v = u + first_virtual(i)
        if isinstance(u, int):
            if v < 3:
                return WIN, (i - 2 + v) * TK, (UPPER if v == 0 else CAUSAL if v == 2 else NO_MASK)
            return SEL, (v - 3) * TK, (CAUSAL if v - 3 == i else NO_MASK)
        is_win = v < 3
        src = jnp.where(is_win, WIN, SEL)
        kb = jnp.where(is_win, i - 2 + v, v - 3)
        mt = jnp.where(is_win, jnp.where(v == 0, UPPER, jnp.where(v == 2, CAUSAL, NO_MASK)),
                       jnp.where(kb == i, CAUSAL, NO_MASK))
        return src, pl.multiple_of(kb * TK, TK), mt

    def qk(i, u, st_ref, c):
        kvh, g = divmod(c, N_GRP)
        src, k0, mt = source(i, u)
        cs = slice(g * TQ, (g + 1) * TQ)
        st_ref[kvh, :, cs] = (_dot(kk_ref[2 * src + kvh, pl.ds(k0, TK), :], qq_ref[i % 2, src, kvh, :, cs])
                              + mask_ref[mt])

    def consume(i, u, st_ref, c):
        kvh, g = divmod(c, N_GRP)
        src, k0, _ = source(i, u)
        cs = slice(g * TQ, (g + 1) * TQ)
        _online_softmax_step(st_ref[kvh, :, cs], vt_ref[src, kvh, :, pl.ds(k0, TK)],
                             m_ref.at[src, kvh, :, cs], acc_ref.at[src, kvh, :, cs])

    def prepare(i, interleaved):
        rs = slice(i * TQ, (i + 1) * TQ)
        slot = i % 2
        qts = [_transpose_bf16(q_ref[rs, g * LANES:(g + 1) * LANES]) for g in range(N_GRP)]
        qbs = []
        for kvh in kv_heads:
            half = (rows < 64) if kvh == 0 else (rows >= 64)
            qb = jnp.concatenate(
                [jnp.where(half, qts[g], jnp.concatenate([qf_ref[kvh, g]] * (TQ // LANES), axis=1))
                 for g in range(N_GRP)], axis=1)
            qq_ref[slot, WIN, kvh] = qb
            qq_ref[slot, SEL, kvh] = qb
            qbs.append(qb)
        qpos = i * TQ + lax.broadcasted_iota(jnp.int32, (1, TQ), 1)

        valid = (rows * CMP_STRIDE + (CMP_BLOCK - 1)) <= qpos
        blk_t = qpos >> 6
        valid_b = bidx <= blk_t
        forced = valid_b & ((bidx == 0) | (bidx >= blk_t - (N_LOCAL - 1)))
        scores = []
        for kvh in kv_heads:
            st = _dot(kc_ref[0, kvh], qbs[kvh])
            p_parts = []
            for g in range(N_GRP):
                sg = jnp.where(valid, st[:, g * TQ:(g + 1) * TQ], NEG_BIG)
                mx = jnp.max(sg, axis=0, keepdims=True)
                ex = jnp.where(valid, jnp.exp2(sg - mx), 0.0)
                den = jnp.sum(ex, axis=0, keepdims=True)
                p_parts.append(ex * jnp.where(den > 0.0, 1.0 / den, 0.0))
            pt = jnp.concatenate(p_parts, axis=1)
            ocmp_ref[slot, kvh] = _dot(vct_ref[0, kvh], pt.astype(BF16))
            psum = p_parts[0] + p_parts[1] + p_parts[2] + p_parts[3]
            ovt = ovt_ref[kvh]
            p_hi = psum.astype(BF16)
            p_lo = (psum - p_hi.astype(F32)).astype(BF16)
            seg0 = 96 - 64 * kvh
            imp = (_dot(ovt, p_hi) + _dot(ovt, p_lo))[seg0:seg0 + N_SEL_BLOCKS]
            scores.append(jnp.where(valid_b, imp, -1.0) + jnp.where(forced, FORCE_BONUS, 0.0))

        chosen = [jnp.zeros((N_SEL_BLOCKS, TQ), F32) for _ in kv_heads]
        for r in range(max(SEL_TOPK, n_chunks, len(interleaved))):
            if r < SEL_TOPK:
                for kvh in kv_heads:
                    best = jnp.max(scores[kvh], axis=0, keepdims=True)
                    first = jnp.min(jnp.where(scores[kvh] == best, row_f, 1e9), axis=0, keepdims=True)
                    pick = row_f == first
                    chosen[kvh] = jnp.where(pick, 1.0, chosen[kvh])
                    scores[kvh] = jnp.where(pick, -jnp.inf, scores[kvh])
            if r < n_chunks:
                qk(i, 0, st_a, r)
            if r < len(interleaved):
                interleaved[r]()
        for kvh in kv_heads:
            sel_bias = jnp.where(chosen[kvh] == 0.0, SEL_MASK_BIAS, 0.0).astype(BF16)
            seg0 = 96 - 64 * kvh
            for g in range(N_GRP):
                qq_ref[slot, SEL, kvh, seg0:seg0 + N_SEL_BLOCKS, g * TQ:(g + 1) * TQ] = sel_bias

    gate_rows = {}

    def output_piece(i, g):
        rs = slice(i * TQ, (i + 1) * TQ)
        if i not in gate_rows:
            gate_rows[i] = gate_ref[rs, :].T
        gt = gate_rows[i]
        cs = slice(g * TQ, (g + 1) * TQ)
        ots = []
        for kvh in kv_heads:
            gr = [gt[br * 8 + kvh * 4 + g:br * 8 + kvh * 4 + g + 1] for br in range(3)]
            a_s = acc_ref[SEL, kvh, :, cs]
            a_w = acc_ref[WIN, kvh, :, cs]
            ots.append(gr[0] * ocmp_ref[i % 2, kvh, :, cs]
                       + (gr[1] / a_s[NSA_D:NSA_D + 1]) * a_s[:NSA_D]
                       + (gr[2] / a_w[NSA_D:NSA_D + 1]) * a_w[:NSA_D])
        og = jnp.concatenate(ots, axis=0).T
        sl = slice(g * LANES, (g + 1) * LANES)
        o_ref[rs, sl] = (og * nz_ref[rs, sl].astype(F32)).astype(BF16)

    reset()
    prepare(0, [])
    for i in range(n_q):
        _pipelined_sweep(3 + i - first_virtual(i), n_chunks,
                         lambda u, st, c: qk(i, u, st, c), lambda u, st, c: consume(i, u, st, c),
                         st_a, st_b, first_scores_done=True)
        pieces = [(lambda g=g: output_piece(i, g)) for g in range(N_GRP)]
        if i + 1 < n_q:
            prepare(i + 1, pieces)
        else:
            for piece in pieces:
                piece()
        reset()


def _out_kernel(x_ref, od_ref, on_ref, w_ref, o_ref):
    acc = _dot(od_ref[...], w_ref[:DIFF_W, :]) + _dot(on_ref[...], w_ref[DIFF_W:, :])
    o_ref[...] = x_ref[...] + acc


def _params(sem):
    return pltpu.CompilerParams(dimension_semantics=sem, vmem_limit_bytes=VMEM_LIMIT)


def _bf16_pieces(v):
    out = []
    r = np.float32(v)
    for _ in range(3):
        p = np.asarray(r).astype(BF16).astype(np.float32)
        out.append(float(p))
        r = np.float32(r - p)
    return out


def _slope_features(slope):
    f = []
    for piece in _bf16_pieces(slope * LOG2E):
        f += [piece * 256.0, piece]
    return f


def _key_features(seq, with_blocks):
    t = np.arange(seq)
    f = np.zeros((seq, 64), np.float32)
    for r in range(3):
        f[:, 2 * r] = t // 256
        f[:, 2 * r + 1] = t % 256
    if with_blocks:
        f[t, 32 + t // SEL_BLOCK] = 1.0
    return jnp.asarray(np.concatenate([f, f], axis=1), dtype=BF16)


def _nsa_query_features():
    f = np.zeros((N_KV, N_GRP, LANES), np.float32)
    for kvh in range(N_KV):
        base = 64 if kvh == 0 else 0
        for g in range(N_GRP):
            f[kvh, g, base:base + 6] = _slope_features(2.0 ** (-(kvh * N_GRP + g + 1)))
    return jnp.asarray(np.repeat(f[..., None], LANES, axis=-1), dtype=BF16)


def _diff_query_features():
    f = np.zeros((N_DIFF_HEADS, LANES), np.float32)
    for h in range(N_DIFF_HEADS):
        f[h, 0:6] = _slope_features(2.0 ** (-8.0 * (h + 1) / N_DIFF_HEADS))
    return jnp.asarray(np.repeat(f[..., None], LANES, axis=-1), dtype=BF16)


def _overlap_t(seq):
    n_cmp = (seq - CMP_BLOCK) // CMP_STRIDE + 1
    n_sb = seq // SEL_BLOCK
    cs = np.arange(n_cmp) * CMP_STRIDE
    ss = np.arange(n_sb) * SEL_BLOCK
    ovl = ((cs[:, None] < ss[None, :] + SEL_BLOCK) & (cs[:, None] + CMP_BLOCK > ss[None, :])).astype(np.float32)
    out = np.zeros((N_KV, LANES, LANES), np.float32)
    out[0, 96:96 + n_sb, :n_cmp] = ovl.T
    out[1, 32:32 + n_sb, :n_cmp] = ovl.T
    return jnp.asarray(out, dtype=BF16)


def _to_pair_order(a, axis):
    shp = a.shape
    a = a.reshape(shp[:axis] + (N_KV, N_GRP, NSA_D) + shp[axis + 1:])
    a = jnp.swapaxes(a, axis, axis + 1)
    return a.reshape(shp)


def kernel(x, norm_g, w_in, diff_q_norm_g, diff_k_norm_g, diff_lambda_q1, diff_lambda_k1, diff_lambda_q2,
           diff_lambda_k2, diff_subln_g, nsa_q_norm_g, nsa_k_norm_g, cmp_pos, cmp_w1, cmp_b1, cmp_w2, w_out):
    B, T, D = x.shape
    BT = B * T
    assert D == D_MODEL and TQ == TK and T == N_SEL_BLOCKS * SEL_BLOCK and T % TM_PROJ == 0
    n_half = T // CMP_STRIDE
    nq = T // TQ

    w = w_in[0].astype(BF16)
    w_p = jnp.concatenate([w[:, :2048], _to_pair_order(w[:, 2048:2560], 1), w[:, 2560:3328],
                           _to_pair_order(w[:, 3328:3840], 1), w[:, IN_MAIN:],
                           jnp.zeros((D, IN_PAD - IN_MAIN - N_GATE), BF16)], axis=1)
    w_ob = w_out[0].astype(BF16)
    w_o = jnp.concatenate([w_ob[:DIFF_W], _to_pair_order(w_ob[DIFF_W:], 0)], axis=0)
    seg = np.arange(256) // 64
    e_mat = jnp.asarray((seg[:, None] == seg[None, :]).astype(np.float32) / 64.0, dtype=BF16)

    def small(a):
        a = a.astype(F32)
        return jnp.pad(a, ((0, SMALL_PARAM_ROWS - a.shape[0]), (0, LANES - a.shape[1])))

    gq = (jnp.tile(diff_q_norm_g[0], 8) * (DIFF_QK ** -0.5 * LOG2E))[None]
    gk = jnp.tile(diff_k_norm_g[0], 8)[None]
    gnq = (jnp.tile(nsa_q_norm_g[0], 8) * (NSA_D ** -0.5 * LOG2E))[None]
    gkc = small(jnp.tile(nsa_k_norm_g[0, 0], 2)[None])
    gks = small(jnp.tile(nsa_k_norm_g[0, 1], 2)[None])
    gkw = small(jnp.tile(nsa_k_norm_g[0, 2], 2)[None])
    feat_s = _key_features(T, True)
    feat_w = _key_features(T, False)

    x2 = x.reshape(BT, D)
    n_rt = BT // TM_PROJ
    rt_per_seq = T // TM_PROJ
    row_spec = lambda n: pl.BlockSpec((TM_PROJ, n), lambda i: (i, 0))
    full = lambda shp: pl.BlockSpec(shp, lambda i: (0,) * len(shp))
    feat_spec = pl.BlockSpec((TM_PROJ, LANES), lambda i: (i % rt_per_seq, 0))
    bt = lambda n, dt=BF16: jax.ShapeDtypeStruct((BT, n), dt)
    outs = pl.pallas_call(
        _proj_kernel,
        grid=(n_rt,),
        in_specs=[row_spec(D), full((1, D)), full((D, IN_PAD)), full((256, 256)),
                  full((1, 512)), full((1, 512)), full((1, 512)),
                  full((SMALL_PARAM_ROWS, LANES)), full((SMALL_PARAM_ROWS, LANES)),
                  feat_spec, feat_spec],
        out_specs=[row_spec(512), row_spec(512), row_spec(512), row_spec(512), row_spec(512),
                   row_spec(LANES), row_spec(LANES), pl.BlockSpec((4, TM_PROJ, LANES), lambda i: (0, i, 0)),
                   row_spec(LANES), row_spec(LANES), row_spec(512), row_spec(LANES)],
        out_shape=[bt(512), bt(512), bt(512), bt(512), bt(512), bt(LANES, F32), bt(LANES, F32),
                   jax.ShapeDtypeStruct((4, BT, LANES), BF16), bt(LANES), bt(LANES), bt(512), bt(LANES, F32)],
        compiler_params=_params(("parallel",)),
    )(x2, norm_g[0][None], w_p, e_mat, gq, gk, gnq, gks, gkw, feat_s, feat_w)
    dq, dk, dv, dz, nqp, kcr, vcr, kk, vs, vw, nz, gates = outs

    eye2 = jnp.eye(N_KV, dtype=F32)

    def pair_w1(w1):
        wh = w1.astype(BF16).reshape(2, 16, NSA_D, CMP_HIDDEN)
        z = jnp.zeros_like(wh)
        rows = jnp.stack([jnp.concatenate([wh, z], axis=-1), jnp.concatenate([z, wh], axis=-1)], axis=2)
        return rows.reshape(2, 16 * N_KV * NSA_D, N_KV * CMP_HIDDEN)

    def pair_w2(w2):
        z = w2[None, :, None, :] * eye2[:, None, :, None]
        return z.reshape(N_KV * CMP_HIDDEN, N_KV * NSA_D).astype(BF16)

    def pair_pos(p):
        ph = p.reshape(2, 16, 1, NSA_D)
        return jnp.broadcast_to(ph, (2, 16, N_KV, NSA_D)).reshape(2, 1, 16 * N_KV * NSA_D)

    w1k = pair_w1(cmp_w1[0, 0]); w1v = pair_w1(cmp_w1[0, 1])
    w1a = jnp.stack([w1k[0], w1v[0]]); w1b = jnp.stack([w1k[1], w1v[1]])
    w2p = jnp.stack([pair_w2(cmp_w2[0, 0]), pair_w2(cmp_w2[0, 1])])
    pk = pair_pos(cmp_pos[0, 0]); pv = pair_pos(cmp_pos[0, 1])
    pa = jnp.stack([pk[0], pv[0]]); pb = jnp.stack([pk[1], pv[1]])
    b1p = jnp.stack([jnp.tile(cmp_b1[0, 0], 2)[None], jnp.tile(cmp_b1[0, 1], 2)[None]])
    hspec = pl.BlockSpec((T, LANES), lambda b: (b, 0))
    kc, vct = pl.pallas_call(
        _cmp_kernel,
        grid=(B,),
        in_specs=[hspec, hspec, full((2, 1, 2048)), full((2, 1, 2048)), full((2, 2048, 512)),
                  full((2, 2048, 512)), full((2, 1, 512)), full((2, 512, LANES)), full((256, 256)),
                  full((SMALL_PARAM_ROWS, LANES))],
        out_specs=[pl.BlockSpec((1, 2, n_half, LANES), lambda b: (b, 0, 0, 0)),
                   pl.BlockSpec((1, 2, NSA_D, n_half), lambda b: (b, 0, 0, 0))],
        out_shape=[jax.ShapeDtypeStruct((B, 2, n_half, LANES), BF16),
                   jax.ShapeDtypeStruct((B, 2, NSA_D, n_half), BF16)],
        compiler_params=_params(("parallel",)),
    )(kcr, vcr, pa, pb, w1a, w1b, b1p, w2p, e_mat, gkc)

    lam_v = small(jnp.concatenate([diff_lambda_q1, diff_lambda_k1, diff_lambda_q2, diff_lambda_k2], axis=0))
    hps = DIFF_HEADS_PER_STEP
    seq_spec = pl.BlockSpec((T, hps * LANES), lambda b, h: (b, h))
    st_scratch = pltpu.VMEM((hps, TK, 2 * TQ), F32)
    o_diff = pl.pallas_call(
        _diff_kernel,
        grid=(B, N_DIFF_HEADS // hps),
        in_specs=[pl.BlockSpec(memory_space=pltpu.SMEM), seq_spec, seq_spec, seq_spec, seq_spec,
                  pl.BlockSpec((T, LANES), lambda b, h: (0, 0)),
                  pl.BlockSpec((hps, LANES, LANES), lambda b, h: (h, 0, 0)),
                  pl.BlockSpec((SMALL_PARAM_ROWS, LANES), lambda b, h: (0, 0)),
                  pl.BlockSpec((SMALL_PARAM_ROWS, LANES), lambda b, h: (0, 0))],
        out_specs=seq_spec,
        out_shape=bt(DIFF_W),
        scratch_shapes=[pltpu.VMEM((hps, DIFF_VT_ROWS, T), BF16),
                        pltpu.VMEM((2, hps, 2 * LANES, 2 * TQ), BF16),
                        pltpu.VMEM((2, TK, TQ), F32),
                        st_scratch, st_scratch,
                        pltpu.VMEM((hps, 1, 2 * TQ), F32), pltpu.VMEM((hps, DIFF_ACC_ROWS, 2 * TQ), F32)],
        compiler_params=_params(("arbitrary", "arbitrary")),
    )(jnp.zeros((1,), jnp.int32), dq, dk, dv, dz, feat_w, _diff_query_features(), lam_v, small(diff_subln_g[0][None]))

    n_all = N_GRP * TQ
    seq512 = pl.BlockSpec((T, 512), lambda b: (b, 0))
    seq128 = pl.BlockSpec((T, LANES), lambda b: (b, 0))
    st_nsa = pltpu.VMEM((N_KV, TK, n_all), F32)
    o_nsa = pl.pallas_call(
        _nsa_kernel,
        grid=(B,),
        in_specs=[seq512,
                  pl.BlockSpec((1, N_KV, n_half, LANES), lambda b: (b, 0, 0, 0)),
                  pl.BlockSpec((1, N_KV, NSA_D, n_half), lambda b: (b, 0, 0, 0)),
                  pl.BlockSpec((4, T, LANES), lambda b: (0, b, 0)),
                  seq128, seq128, seq128, seq512,
                  pl.BlockSpec((N_KV, N_GRP, LANES, LANES), lambda b: (0, 0, 0, 0)),
                  pl.BlockSpec((N_KV, LANES, LANES), lambda b: (0, 0, 0))],
        out_specs=seq512,
        out_shape=bt(NSA_W),
        scratch_shapes=[pltpu.VMEM((2, N_KV, NSA_VT_ROWS, T), BF16),
                        pltpu.VMEM((2, 2, N_KV, LANES, n_all), BF16),
                        pltpu.VMEM((2, N_KV, NSA_D, n_all), F32),
                        pltpu.VMEM((3, TK, TQ), F32),
                        st_nsa, st_nsa,
                        pltpu.VMEM((2, N_KV, 1, n_all), F32), pltpu.VMEM((2, N_KV, NSA_ACC_ROWS, n_all), F32)],
        compiler_params=_params(("arbitrary",)),
    )(nqp, kc, vct, kk, vs, vw, gates, nz, _nsa_query_features(), _overlap_t(T))

    out_rows = lambda n: pl.BlockSpec((TM_OUT, n), lambda i: (i, 0))
    out = pl.pallas_call(
        _out_kernel,
        grid=(BT // TM_OUT,),
        in_specs=[out_rows(D), out_rows(DIFF_W), out_rows(NSA_W), full((D, D))],
        out_specs=out_rows(D),
        out_shape=jax.ShapeDtypeStruct((BT, D), x.dtype),
        compiler_params=_params(("parallel",)),
    )(x2, o_diff, o_nsa, w_o)
    return out.reshape(B, T, D)
```

```python
import math

import numpy as np
import jax
import jax.numpy as jnp
from jax import lax
from jax.experimental import pallas as pl
from jax.experimental.pallas import tpu as pltpu

F32 = jnp.float32
BF16 = jnp.bfloat16

D_MODEL = 1024
N_DIFF_HEADS = 4
DIFF_QK = 64
DIFF_V = 128
DIFF_W = 512
N_NSA_HEADS = 8
NSA_D = 64
N_KV = 2
N_GRP = 4
NSA_W = 512
CMP_BLOCK = 32
CMP_STRIDE = 16
CMP_HIDDEN = 256
SEL_BLOCK = 64
N_SEL_BLOCKS = 32
SEL_TOPK = 8
N_LOCAL = 2
FORCE_BONUS = 1000.0
WINDOW = 512
EPS = 1e-6
NEG_BIG = -1e30
LAM_INIT = 0.8 - 0.6 * math.exp(-0.3 * 0)
LOG2E = math.log2(math.e)

IN_MAIN = 3840
N_GATE = 24
IN_PAD = 3968
LANES = 128
SUBLANES = 8
BF16_ROWS = 16
SMALL_PARAM_ROWS = 32
SEL_MASK_BIAS = -32768.0

TM_PROJ = 1024
TM_OUT = 2048
TQ = 256
TK = 256
VMEM_LIMIT = 56 * 1024 * 1024
DIFF_ACC_ROWS = DIFF_V + SUBLANES
DIFF_VT_ROWS = DIFF_V + BF16_ROWS
DIFF_HEADS_PER_STEP = 4
NSA_ACC_ROWS = NSA_D + SUBLANES
NSA_VT_ROWS = NSA_D + BF16_ROWS


def _dot(a, b):
    return jnp.dot(a, b, preferred_element_type=F32)


def _sigmoid(y):
    return 1.0 / (1.0 + jnp.exp(-y))


def _seg_mean_sq(y, e):
    n = y.shape[1]
    y2 = (y * y).astype(BF16)
    if n == LANES:
        return _dot(y2, e[:LANES, :LANES])
    return jnp.concatenate([_dot(y2[:, c:c + 256], e) for c in range(0, n, 256)], axis=1)


def _seg_norm(y, e, gain):
    return y * lax.rsqrt(_seg_mean_sq(y, e) + EPS) * gain


def _transpose_bf16(a):
    return a.astype(F32).T.astype(BF16)


def _ones_rows(n_rows, width):
    return jnp.where(lax.broadcasted_iota(jnp.int32, (n_rows, width), 0) == 0, 1.0, 0.0).astype(BF16)


def _mask_cols(st, keep, width):
    n = st.shape[1]
    return jnp.concatenate([jnp.where(keep, st[:, a:a + width], -jnp.inf) for a in range(0, n, width)], axis=1)


def _online_softmax_step(st, vt_blk, m_ref, acc_ref):
    m_prev = m_ref[...]
    m_new = jnp.maximum(m_prev, jnp.max(st, axis=0, keepdims=True))
    alpha = jnp.exp2(m_prev - m_new)
    p = jnp.exp2(st - m_new).astype(BF16)
    pv = _dot(vt_blk, p)
    acc_ref[...] = alpha * acc_ref[...] + pv[:acc_ref.shape[0]]
    m_ref[...] = m_new


def _pipelined_sweep(n, n_chunks, qk, consume, st_a, st_b, interleave=True, first_scores_done=False, zero=None):
    chunks = range(n_chunks)

    def both(j_next, st_next, j_cur, st_cur):
        if interleave:
            for c in chunks:
                qk(j_next, st_next, c)
                consume(j_cur, st_cur, c)
        else:
            for c in chunks:
                qk(j_next, st_next, c)
            for c in chunks:
                consume(j_cur, st_cur, c)

    if not first_scores_done:
        for c in chunks:
            qk(0, st_a, c)

    def body(jj, carry):
        j = 2 * jj
        both(j + 1, st_b, j, st_a)
        both(j + 2, st_a, j + 1, st_b)
        return carry

    def tail_even():
        for c in chunks:
            consume(n, st_a, c)

    def tail_odd():
        both(n, st_b, n - 1, st_a)
        for c in chunks:
            consume(n, st_b, c)

    if zero is not None:
        lax.fori_loop(0, n // 2 + zero, body, 0)
        (tail_even if n % 2 == 0 else tail_odd)()
    elif isinstance(n, int):
        if n // 2 > 0:
            lax.fori_loop(0, n // 2, body, 0)
        (tail_even if n % 2 == 0 else tail_odd)()
    else:
        lax.fori_loop(0, n // 2, body, 0)
        pl.when(n % 2 == 0)(tail_even)
        pl.when(n % 2 == 1)(tail_odd)


def _proj_kernel(x_ref, g_ref, w_ref, e_ref, gq_ref, gk_ref, gnq_ref, gks_ref, gkw_ref, fs_ref, fw_ref,
                 dq_ref, dk_ref, dv_ref, dz_ref, nq_ref, kc_ref, vc_ref, kk_ref, vs_ref, vw_ref,
                 nz_ref, gate_ref):
    x = x_ref[...]
    ms = jnp.mean(x * x, axis=-1, keepdims=True)
    h = (x * lax.rsqrt(ms + EPS) * g_ref[...]).astype(BF16)
    e = e_ref[...]

    def proj(a, b):
        return _dot(h, w_ref[:, a:b])

    dq_ref[...] = _seg_norm(proj(0, 512), e, gq_ref[...]).astype(BF16)
    dk_ref[...] = _seg_norm(proj(512, 1024), e, gk_ref[...]).astype(BF16)
    dv_ref[...] = proj(1024, 1536).astype(BF16)
    y = proj(1536, 2048)
    dz_ref[...] = (y * _sigmoid(y)).astype(BF16)
    nq_ref[...] = _seg_norm(proj(2048, 2560), e, gnq_ref[...]).astype(BF16)
    y = proj(2560, 2816)
    kc_ref[...] = y[:, :LANES]
    vc_ref[...] = y[:, LANES:]

    low = lax.broadcasted_iota(jnp.int32, (1, LANES), 1) < 64
    y = proj(2816, 3072)
    kp = _seg_norm(y[:, :LANES], e, gks_ref[0:1, :]).astype(BF16)
    fs = fs_ref[...]
    kk_ref[0] = jnp.where(low, kp, fs)
    kk_ref[1] = jnp.where(low, fs, kp)
    vs_ref[...] = y[:, LANES:].astype(BF16)
    y = proj(3072, 3328)
    kp = _seg_norm(y[:, :LANES], e, gkw_ref[0:1, :]).astype(BF16)
    fw = fw_ref[...]
    kk_ref[2] = jnp.where(low, kp, fw)
    kk_ref[3] = jnp.where(low, fw, kp)
    vw_ref[...] = y[:, LANES:].astype(BF16)
    y = proj(3328, 3840)
    nz_ref[...] = (y * _sigmoid(y)).astype(BF16)
    gate_ref[...] = _sigmoid(proj(3840, 3968))


def _cmp_kernel(hk_ref, hv_ref, pa_ref, pb_ref, w1a_ref, w1b_ref, b1_ref, w2_ref, e_ref, gkc_ref,
                kc_ref, vct_ref):
    n_rows = hk_ref.shape[0] // CMP_STRIDE
    row_ok = lax.broadcasted_iota(jnp.int32, (n_rows, 1), 0) < (n_rows - 1)
    low = lax.broadcasted_iota(jnp.int32, (1, LANES), 1) < 64

    def mlp(h_ref, idx):
        hf = jnp.concatenate([h_ref[pl.ds(t, n_rows, stride=CMP_STRIDE), :] for t in range(CMP_STRIDE)], axis=1)
        ha = (hf + pa_ref[idx]).astype(BF16)
        hb = (hf + pb_ref[idx]).astype(BF16)
        a = _dot(ha, w1a_ref[idx])
        b = _dot(hb, w1b_ref[idx])
        hid = a + pltpu.roll(b, n_rows - 1, 0) + b1_ref[idx]
        hid = hid * _sigmoid(hid)
        return _dot(hid.astype(BF16), w2_ref[idx])

    yk = mlp(hk_ref, 0)
    yk = _seg_norm(yk, e_ref[...], gkc_ref[0:1, :])
    yk = jnp.where(row_ok, yk, 0.0).astype(BF16)
    zero = jnp.zeros_like(yk)
    kc_ref[0, 0] = jnp.where(low, yk, zero)
    kc_ref[0, 1] = jnp.where(low, zero, yk)
    yvt = jnp.where(row_ok, mlp(hv_ref, 1), 0.0).T.astype(BF16)
    vct_ref[0, 0] = yvt[:NSA_D]
    vct_ref[0, 1] = yvt[NSA_D:]


def _diff_kernel(zero_ref, q_ref, k_ref, v_ref, z_ref, fk_ref, qf_ref, lam_ref, gsub_ref, o_ref,
                 vt_ref, qq_ref, mask_ref, st_a, st_b, m_ref, acc_ref):
    seq = k_ref.shape[0]
    heads = range(DIFF_HEADS_PER_STEP)

    for hh in heads:
        vt_ref[hh, 0:DIFF_V, :] = _transpose_bf16(v_ref[:, hh * DIFF_V:(hh + 1) * DIFF_V])
        vt_ref[hh, DIFF_V:DIFF_VT_ROWS, :] = _ones_rows(BF16_ROWS, seq)
    causal = lax.broadcasted_iota(jnp.int32, (TK, TQ), 0) <= lax.broadcasted_iota(jnp.int32, (TK, TQ), 1)
    mask_ref[0] = jnp.zeros((TK, TQ), F32)
    mask_ref[1] = jnp.where(causal, 0.0, -jnp.inf)
    rows = lax.broadcasted_iota(jnp.int32, (LANES, TQ), 0)
    lam_v = lam_ref[0:4, 0:DIFF_QK]
    lam = (jnp.exp(jnp.sum(lam_v[0:1] * lam_v[1:2], axis=-1, keepdims=True))
           - jnp.exp(jnp.sum(lam_v[2:3] * lam_v[3:4], axis=-1, keepdims=True)) + LAM_INIT)

    def qk(i, j, st_ref, hh):
        k0 = j * TK if isinstance(j, int) else pl.multiple_of(j * TK, TK)
        diag = int(j == i) if isinstance(j, int) else jnp.where(j == i, 1, 0)
        mask = mask_ref[diag]
        keys = jnp.concatenate([k_ref[pl.ds(k0, TK), hh * LANES:(hh + 1) * LANES], fk_ref[pl.ds(k0, TK), :]],
                               axis=1)
        st = _dot(keys, qq_ref[i % 2, hh])
        st_ref[hh] = jnp.concatenate([st[:, a:a + TQ] + mask for a in range(0, 2 * TQ, TQ)], axis=1)

    def consume(j, st_ref, hh):
        k0 = j * TK if isinstance(j, int) else pl.multiple_of(j * TK, TK)
        _online_softmax_step(st_ref[hh], vt_ref[hh, :, pl.ds(k0, TK)], m_ref.at[hh], acc_ref.at[hh])

    def prepare(i, hh):
        qt = _transpose_bf16(q_ref[i * TQ:(i + 1) * TQ, hh * LANES:(hh + 1) * LANES])
        zero = jnp.zeros_like(qt)
        qq_ref[i % 2, hh, 0:LANES, :] = jnp.concatenate(
            [jnp.where(rows < 64, qt, zero), jnp.where(rows >= 64, qt, zero)], axis=1)
        qq_ref[i % 2, hh, LANES:2 * LANES, :] = jnp.concatenate([qf_ref[hh]] * (2 * TQ // LANES), axis=1)
        qk(i, 0, st_a, hh)

    def finalize(i, hh):
        acc = acc_ref[hh]
        o2 = acc[0:DIFF_V] / acc[DIFF_V:DIFF_V + 1]
        o = (o2[:, :TQ] - lam * o2[:, TQ:]).T
        ms = jnp.mean(o * o, axis=-1, keepdims=True)
        o = o * lax.rsqrt(ms + EPS) * gsub_ref[0:1, :] * (1.0 - LAM_INIT)
        sl = slice(hh * DIFF_V, (hh + 1) * DIFF_V)
        rs = slice(i * TQ, (i + 1) * TQ)
        o_ref[rs, sl] = (o * z_ref[rs, sl].astype(F32)).astype(BF16)

    def reset():
        m_ref[...] = jnp.full(m_ref.shape, NEG_BIG, F32)
        acc_ref[...] = jnp.zeros(acc_ref.shape, F32)

    reset()
    for hh in heads:
        prepare(0, hh)
    n_q = seq // TQ
    for i in range(n_q):
        _pipelined_sweep(i, DIFF_HEADS_PER_STEP, lambda j, st, hh: qk(i, j, st, hh), consume, st_a, st_b,
                         interleave=True, first_scores_done=True, zero=zero_ref[0])
        for hh in heads:
            if i + 1 < n_q:
                prepare(i + 1, hh)
            finalize(i, hh)
        reset()


def _nsa_kernel(q_ref, kc_ref, vct_ref, kk_ref, vs_ref, vw_ref, gate_ref, nz_ref, qf_ref, ovt_ref,
                o_ref, vt_ref, qq_ref, ocmp_ref, mask_ref, st_a, st_b, m_ref, acc_ref):
    seq = vs_ref.shape[0]
    n_q = seq // TQ
    SEL, WIN = 0, 1
    NO_MASK, CAUSAL, UPPER = 0, 1, 2
    kv_heads = range(N_KV)
    n_chunks = N_KV * N_GRP
    krow = lax.broadcasted_iota(jnp.int32, (TK, TQ), 0)
    qcol = lax.broadcasted_iota(jnp.int32, (TK, TQ), 1)
    rows = lax.broadcasted_iota(jnp.int32, (LANES, TQ), 0)
    bidx = lax.broadcasted_iota(jnp.int32, (N_SEL_BLOCKS, TQ), 0)
    row_f = bidx.astype(F32)

    for br, v_ref in ((SEL, vs_ref), (WIN, vw_ref)):
        vt = _transpose_bf16(v_ref[...])
        for kvh in kv_heads:
            vt_ref[br, kvh, 0:NSA_D, :] = vt[kvh * NSA_D:(kvh + 1) * NSA_D]
            vt_ref[br, kvh, NSA_D:NSA_VT_ROWS, :] = _ones_rows(BF16_ROWS, seq)
    mask_ref[NO_MASK] = jnp.zeros((TK, TQ), F32)
    mask_ref[CAUSAL] = jnp.where(krow <= qcol, 0.0, -jnp.inf)
    mask_ref[UPPER] = jnp.where(krow > qcol, 0.0, -jnp.inf)

    def reset():
        m_ref[...] = jnp.full(m_ref.shape, NEG_BIG, F32)
        acc_ref[...] = jnp.zeros(acc_ref.shape, F32)

    def first_virtual(i):
        return 2 - min(i, 2)

    def source(i, u):
        v = u + first_virtual(i)
        if isinstance(u, int):
            if v < 3:
                return WIN, (i - 2 + v) * TK, (UPPER if v == 0 else CAUSAL if v == 2 else NO_MASK)
            return SEL, (v - 3) * TK, (CAUSAL if v - 3 == i else NO_MASK)
        is_win = v < 3
        src = jnp.where(is_win, WIN, SEL)
        kb = jnp.where(is_win, i - 2 + v, v - 3)
        mt = jnp.where(is_win, jnp.where(v == 0, UPPER, jnp.where(v == 2, CAUSAL, NO_MASK)),
                       jnp.where(kb == i, CAUSAL, NO_MASK))
        return src, pl.multiple_of(kb * TK, TK), mt

    def qk(i, u, st_ref, c):
        kvh, g = divmod(c, N_GRP)
        src, k0, mt = source(i, u)
        cs = slice(g * TQ, (g + 1) * TQ)
        st_ref[kvh, :, cs] = (_dot(kk_ref[2 * src + kvh, pl.ds(k0, TK), :], qq_ref[i % 2, src, kvh, :, cs])
                              + mask_ref[mt])

    def consume(i, u, st_ref, c):
        kvh, g = divmod(c, N_GRP)
        src, k0, _ = source(i, u)
        cs = slice(g * TQ, (g + 1) * TQ)
        _online_softmax_step(st_ref[kvh, :, cs], vt_ref[src, kvh, :, pl.ds(k0, TK)],
                             m_ref.at[src, kvh, :, cs], acc_ref.at[src, kvh, :, cs])

    def prepare(i, interleaved):
        rs = slice(i * TQ, (i + 1) * TQ)
        slot = i % 2
        qts = [_transpose_bf16(q_ref[rs, g * LANES:(g + 1) * LANES]) for g in range(N_GRP)]
        qbs = []
        for kvh in kv_heads:
            half = (rows < 64) if kvh == 0 else (rows >= 64)
            qb = jnp.concatenate(
                [jnp.where(half, qts[g], jnp.concatenate([qf_ref[kvh, g]] * (TQ // LANES), axis=1))
                 for g in range(N_GRP)], axis=1)
            qq_ref[slot, WIN, kvh] = qb
            qq_ref[slot, SEL, kvh] = qb
            qbs.append(qb)
        qpos = i * TQ + lax.broadcasted_iota(jnp.int32, (1, TQ), 1)

        valid = (rows * CMP_STRIDE + (CMP_BLOCK - 1)) <= qpos
        blk_t = qpos >> 6
        valid_b = bidx <= blk_t
        forced = valid_b & ((bidx == 0) | (bidx >= blk_t - (N_LOCAL - 1)))
        scores = []
        for kvh in kv_heads:
            st = _dot(kc_ref[0, kvh], qbs[kvh])
            p_parts = []
            for g in range(N_GRP):
                sg = jnp.where(valid, st[:, g * TQ:(g + 1) * TQ], NEG_BIG)
                mx = jnp.max(sg, axis=0, keepdims=True)
                ex = jnp.where(valid, jnp.exp2(sg - mx), 0.0)
                den = jnp.sum(ex, axis=0, keepdims=True)
                p_parts.append(ex * jnp.where(den > 0.0, 1.0 / den, 0.0))
            pt = jnp.concatenate(p_parts, axis=1)
            ocmp_ref[slot, kvh] = _dot(vct_ref[0, kvh], pt.astype(BF16))
            psum = p_parts[0] + p_parts[1] + p_parts[2] + p_parts[3]
            ovt = ovt_ref[kvh]
            p_hi = psum.astype(BF16)
            p_lo = (psum - p_hi.astype(F32)).astype(BF16)
            seg0 = 96 - 64 * kvh
            imp = (_dot(ovt, p_hi) + _dot(ovt, p_lo))[seg0:seg0 + N_SEL_BLOCKS]
            scores.append(jnp.where(valid_b, imp, -1.0) + jnp.where(forced, FORCE_BONUS, 0.0))

        chosen = [jnp.zeros((N_SEL_BLOCKS, TQ), F32) for _ in kv_heads]
        for r in range(max(SEL_TOPK, n_chunks, len(interleaved))):
            if r < SEL_TOPK:
                for kvh in kv_heads:
                    best = jnp.max(scores[kvh], axis=0, keepdims=True)
                    first = jnp.min(jnp.where(scores[kvh] == best, row_f, 1e9), axis=0, keepdims=True)
                    pick = row_f == first
                    chosen[kvh] = jnp.where(pick, 1.0, chosen[kvh])
                    scores[kvh] = jnp.where(pick, -jnp.inf, scores[kvh])
            if r < n_chunks:
                qk(i, 0, st_a, r)
            if r < len(interleaved):
                interleaved[r]()
        for kvh in kv_heads:
            sel_bias = jnp.where(chosen[kvh] == 0.0, SEL_MASK_BIAS, 0.0).astype(BF16)
            seg0 = 96 - 64 * kvh
            for g in range(N_GRP):
                qq_ref[slot, SEL, kvh, seg0:seg0 + N_SEL_BLOCKS, g * TQ:(g + 1) * TQ] = sel_bias

    gate_rows = {}

    def output_piece(i, g):
        rs = slice(i * TQ, (i + 1) * TQ)
        if i not in gate_rows:
            gate_rows[i] = gate_ref[rs, :].T
        gt = gate_rows[i]
        cs = slice(g * TQ, (g + 1) * TQ)
        ots = []
        for kvh in kv_heads:
            gr = [gt[br * 8 + kvh * 4 + g:br * 8 + kvh * 4 + g + 1] for br in range(3)]
            a_s = acc_ref[SEL, kvh, :, cs]
            a_w = acc_ref[WIN, kvh, :, cs]
            ots.append(gr[0] * ocmp_ref[i % 2, kvh, :, cs]
                       + (gr[1] / a_s[NSA_D:NSA_D + 1]) * a_s[:NSA_D]
                       + (gr[2] / a_w[NSA_D:NSA_D + 1]) * a_w[:NSA_D])
        og = jnp.concatenate(ots, axis=0).T
        sl = slice(g * LANES, (g + 1) * LANES)
        o_ref[rs, sl] = (og * nz_ref[rs, sl].astype(F32)).astype(BF16)

    reset()
    prepare(0, [])
    for i in range(n_q):
        _pipelined_sweep(3 + i - first_virtual(i), n_chunks,
                         lambda u, st, c: qk(i, u, st, c), lambda u, st, c: consume(i, u, st, c),
                         st_a, st_b, first_scores_done=True)
        pieces = [(lambda g=g: output_piece(i, g)) for g in range(N_GRP)]
        if i + 1 < n_q:
            prepare(i + 1, pieces)
        else:
            for piece in pieces:
                piece()
        reset()


def _out_kernel(x_ref, od_ref, on_ref, w_ref, o_ref):
    acc = _dot(od_ref[...], w_ref[:DIFF_W, :]) + _dot(on_ref[...], w_ref[DIFF_W:, :])
    o_ref[...] = x_ref[...] + acc


def _params(sem):
    return pltpu.CompilerParams(dimension_semantics=sem, vmem_limit_bytes=VMEM_LIMIT)


def _bf16_pieces(v):
    out = []
    r = np.float32(v)
    for _ in range(3):
        p = np.asarray(r).astype(BF16).astype(np.float32)
        out.append(float(p))
        r = np.float32(r - p)
    return out


def _slope_features(slope):
    f = []
    for piece in _bf16_pieces(slope * LOG2E):
        f += [piece * 256.0, piece]
    return f


def _key_features(seq, with_blocks):
    t = np.arange(seq)
    f = np.zeros((seq, 64), np.float32)
    for r in range(3):
        f[:, 2 * r] = t // 256
        f[:, 2 * r + 1] = t % 256
    if with_blocks:
        f[t, 32 + t // SEL_BLOCK] = 1.0
    return jnp.asarray(np.concatenate([f, f], axis=1), dtype=BF16)


def _nsa_query_features():
    f = np.zeros((N_KV, N_GRP, LANES), np.float32)
    for kvh in range(N_KV):
        base = 64 if kvh == 0 else 0
        for g in range(N_GRP):
            f[kvh, g, base:base + 6] = _slope_features(2.0 ** (-(kvh * N_GRP + g + 1)))
    return jnp.asarray(np.repeat(f[..., None], LANES, axis=-1), dtype=BF16)


def _diff_query_features():
    f = np.zeros((N_DIFF_HEADS, LANES), np.float32)
    for h in range(N_DIFF_HEADS):
        f[h, 0:6] = _slope_features(2.0 ** (-8.0 * (h + 1) / N_DIFF_HEADS))
    return jnp.asarray(np.repeat(f[..., None], LANES, axis=-1), dtype=BF16)


def _overlap_t(seq):
    n_cmp = (seq - CMP_BLOCK) // CMP_STRIDE + 1
    n_sb = seq // SEL_BLOCK
    cs = np.arange(n_cmp) * CMP_STRIDE
    ss = np.arange(n_sb) * SEL_BLOCK
    ovl = ((cs[:, None] < ss[None, :] + SEL_BLOCK) & (cs[:, None] + CMP_BLOCK > ss[None, :])).astype(np.float32)
    out = np.zeros((N_KV, LANES, LANES), np.float32)
    out[0, 96:96 + n_sb, :n_cmp] = ovl.T
    out[1, 32:32 + n_sb, :n_cmp] = ovl.T
    return jnp.asarray(out, dtype=BF16)


def _to_pair_order(a, axis):
    shp = a.shape
    a = a.reshape(shp[:axis] + (N_KV, N_GRP, NSA_D) + shp[axis + 1:])
    a = jnp.swapaxes(a, axis, axis + 1)
    return a.reshape(shp)


def kernel(x, norm_g, w_in, diff_q_norm_g, diff_k_norm_g, diff_lambda_q1, diff_lambda_k1, diff_lambda_q2,
           diff_lambda_k2, diff_subln_g, nsa_q_norm_g, nsa_k_norm_g, cmp_pos, cmp_w1, cmp_b1, cmp_w2, w_out):
    B, T, D = x.shape
    BT = B * T
    assert D == D_MODEL and TQ == TK and T == N_SEL_BLOCKS * SEL_BLOCK and T % TM_PROJ == 0
    n_half = T // CMP_STRIDE
    nq = T // TQ

    w = w_in[0].astype(BF16)
    w_p = jnp.concatenate([w[:, :2048], _to_pair_order(w[:, 2048:2560], 1), w[:, 2560:3328],
                           _to_pair_order(w[:, 3328:3840], 1), w[:, IN_MAIN:],
                           jnp.zeros((D, IN_PAD - IN_MAIN - N_GATE), BF16)], axis=1)
    w_ob = w_out[0].astype(BF16)
    w_o = jnp.concatenate([w_ob[:DIFF_W], _to_pair_order(w_ob[DIFF_W:], 0)], axis=0)
    seg = np.arange(256) // 64
    e_mat = jnp.asarray((seg[:, None] == seg[None, :]).astype(np.float32) / 64.0, dtype=BF16)

    def small(a):
        a = a.astype(F32)
        return jnp.pad(a, ((0, SMALL_PARAM_ROWS - a.shape[0]), (0, LANES - a.shape[1])))

    gq = (jnp.tile(diff_q_norm_g[0], 8) * (DIFF_QK ** -0.5 * LOG2E))[None]
    gk = jnp.tile(diff_k_norm_g[0], 8)[None]
    gnq = (jnp.tile(nsa_q_norm_g[0], 8) * (NSA_D ** -0.5 * LOG2E))[None]
    gkc = small(jnp.tile(nsa_k_norm_g[0, 0], 2)[None])
    gks = small(jnp.tile(nsa_k_norm_g[0, 1], 2)[None])
    gkw = small(jnp.tile(nsa_k_norm_g[0, 2], 2)[None])
    feat_s = _key_features(T, True)
    feat_w = _key_features(T, False)

    x2 = x.reshape(BT, D)
    n_rt = BT // TM_PROJ
    rt_per_seq = T // TM_PROJ
    row_spec = lambda n: pl.BlockSpec((TM_PROJ, n), lambda i: (i, 0))
    full = lambda shp: pl.BlockSpec(shp, lambda i: (0,) * len(shp))
    feat_spec = pl.BlockSpec((TM_PROJ, LANES), lambda i: (i % rt_per_seq, 0))
    bt = lambda n, dt=BF16: jax.ShapeDtypeStruct((BT, n), dt)
    outs = pl.pallas_call(
        _proj_kernel,
        grid=(n_rt,),
        in_specs=[row_spec(D), full((1, D)), full((D, IN_PAD)), full((256, 256)),
                  full((1, 512)), full((1, 512)), full((1, 512)),
                  full((SMALL_PARAM_ROWS, LANES)), full((SMALL_PARAM_ROWS, LANES)),
                  feat_spec, feat_spec],
        out_specs=[row_spec(512), row_spec(512), row_spec(512), row_spec(512), row_spec(512),
                   row_spec(LANES), row_spec(LANES), pl.BlockSpec((4, TM_PROJ, LANES), lambda i: (0, i, 0)),
                   row_spec(LANES), row_spec(LANES), row_spec(512), row_spec(LANES)],
        out_shape=[bt(512), bt(512), bt(512), bt(512), bt(512), bt(LANES, F32), bt(LANES, F32),
                   jax.ShapeDtypeStruct((4, BT, LANES), BF16), bt(LANES), bt(LANES), bt(512), bt(LANES, F32)],
        compiler_params=_params(("parallel",)),
    )(x2, norm_g[0][None], w_p, e_mat, gq, gk, gnq, gks, gkw, feat_s, feat_w)
    dq, dk, dv, dz, nqp, kcr, vcr, kk, vs, vw, nz, gates = outs

    eye2 = jnp.eye(N_KV, dtype=F32)

    def pair_w1(w1):
        wh = w1.astype(BF16).reshape(2, 16, NSA_D, CMP_HIDDEN)
        z = jnp.zeros_like(wh)
        rows = jnp.stack([jnp.concatenate([wh, z], axis=-1), jnp.concatenate([z, wh], axis=-1)], axis=2)
        return rows.reshape(2, 16 * N_KV * NSA_D, N_KV * CMP_HIDDEN)

    def pair_w2(w2):
        z = w2[None, :, None, :] * eye2[:, None, :, None]
        return z.reshape(N_KV * CMP_HIDDEN, N_KV * NSA_D).astype(BF16)

    def pair_pos(p):
        ph = p.reshape(2, 16, 1, NSA_D)
        return jnp.broadcast_to(ph, (2, 16, N_KV, NSA_D)).reshape(2, 1, 16 * N_KV * NSA_D)

    w1k = pair_w1(cmp_w1[0, 0]); w1v = pair_w1(cmp_w1[0, 1])
    w1a = jnp.stack([w1k[0], w1v[0]]); w1b = jnp.stack([w1k[1], w1v[1]])
    w2p = jnp.stack([pair_w2(cmp_w2[0, 0]), pair_w2(cmp_w2[0, 1])])
    pk = pair_pos(cmp_pos[0, 0]); pv = pair_pos(cmp_pos[0, 1])
    pa = jnp.stack([pk[0], pv[0]]); pb = jnp.stack([pk[1], pv[1]])
    b1p = jnp.stack([jnp.tile(cmp_b1[0, 0], 2)[None], jnp.tile(cmp_b1[0, 1], 2)[None]])
    hspec = pl.BlockSpec((T, LANES), lambda b: (b, 0))
    kc, vct = pl.pallas_call(
        _cmp_kernel,
        grid=(B,),
        in_specs=[hspec, hspec, full((2, 1, 2048)), full((2, 1, 2048)), full((2, 2048, 512)),
                  full((2, 2048, 512)), full((2, 1, 512)), full((2, 512, LANES)), full((256, 256)),
                  full((SMALL_PARAM_ROWS, LANES))],
        out_specs=[pl.BlockSpec((1, 2, n_half, LANES), lambda b: (b, 0, 0, 0)),
                   pl.BlockSpec((1, 2, NSA_D, n_half), lambda b: (b, 0, 0, 0))],
        out_shape=[jax.ShapeDtypeStruct((B, 2, n_half, LANES), BF16),
                   jax.ShapeDtypeStruct((B, 2, NSA_D, n_half), BF16)],
        compiler_params=_params(("parallel",)),
    )(kcr, vcr, pa, pb, w1a, w1b, b1p, w2p, e_mat, gkc)

    lam_v = small(jnp.concatenate([diff_lambda_q1, diff_lambda_k1, diff_lambda_q2, diff_lambda_k2], axis=0))
    hps = DIFF_HEADS_PER_STEP
    seq_spec = pl.BlockSpec((T, hps * LANES), lambda b, h: (b, h))
    st_scratch = pltpu.VMEM((hps, TK, 2 * TQ), F32)
    o_diff = pl.pallas_call(
        _diff_kernel,
        grid=(B, N_DIFF_HEADS // hps),
        in_specs=[pl.BlockSpec(memory_space=pltpu.SMEM), seq_spec, seq_spec, seq_spec, seq_spec,
                  pl.BlockSpec((T, LANES), lambda b, h: (0, 0)),
                  pl.BlockSpec((hps, LANES, LANES), lambda b, h: (h, 0, 0)),
                  pl.BlockSpec((SMALL_PARAM_ROWS, LANES), lambda b, h: (0, 0)),
                  pl.BlockSpec((SMALL_PARAM_ROWS, LANES), lambda b, h: (0, 0))],
        out_specs=seq_spec,
        out_shape=bt(DIFF_W),
        scratch_shapes=[pltpu.VMEM((hps, DIFF_VT_ROWS, T), BF16),
                        pltpu.VMEM((2, hps, 2 * LANES, 2 * TQ), BF16),
                        pltpu.VMEM((2, TK, TQ), F32),
                        st_scratch, st_scratch,
                        pltpu.VMEM((hps, 1, 2 * TQ), F32), pltpu.VMEM((hps, DIFF_ACC_ROWS, 2 * TQ), F32)],
        compiler_params=_params(("arbitrary", "arbitrary")),
    )(jnp.zeros((1,), jnp.int32), dq, dk, dv, dz, feat_w, _diff_query_features(), lam_v, small(diff_subln_g[0][None]))

    n_all = N_GRP * TQ
    seq512 = pl.BlockSpec((T, 512), lambda b: (b, 0))
    seq128 = pl.BlockSpec((T, LANES), lambda b: (b, 0))
    st_nsa = pltpu.VMEM((N_KV, TK, n_all), F32)
    o_nsa = pl.pallas_call(
        _nsa_kernel,
        grid=(B,),
        in_specs=[seq512,
                  pl.BlockSpec((1, N_KV, n_half, LANES), lambda b: (b, 0, 0, 0)),
                  pl.BlockSpec((1, N_KV, NSA_D, n_half), lambda b: (b, 0, 0, 0)),
                  pl.BlockSpec((4, T, LANES), lambda b: (0, b, 0)),
                  seq128, seq128, seq128, seq512,
                  pl.BlockSpec((N_KV, N_GRP, LANES, LANES), lambda b: (0, 0, 0, 0)),
                  pl.BlockSpec((N_KV, LANES, LANES), lambda b: (0, 0, 0))],
        out_specs=seq512,
        out_shape=bt(NSA_W),
        scratch_shapes=[pltpu.VMEM((2, N_KV, NSA_VT_ROWS, T), BF16),
                        pltpu.VMEM((2, 2, N_KV, LANES, n_all), BF16),
                        pltpu.VMEM((2, N_KV, NSA_D, n_all), F32),
                        pltpu.VMEM((3, TK, TQ), F32),
                        st_nsa, st_nsa,
                        pltpu.VMEM((2, N_KV, 1, n_all), F32), pltpu.VMEM((2, N_KV, NSA_ACC_ROWS, n_all), F32)],
        compiler_params=_params(("arbitrary",)),
    )(nqp, kc, vct, kk, vs, vw, gates, nz, _nsa_query_features(), _overlap_t(T))

    out_rows = lambda n: pl.BlockSpec((TM_OUT, n), lambda i: (i, 0))
    out = pl.pallas_call(
        _out_kernel,
        grid=(BT // TM_OUT,),
        in_specs=[out_rows(D), out_rows(DIFF_W), out_rows(NSA_W), full((D, D))],
        out_specs=out_rows(D),
        out_shape=jax.ShapeDtypeStruct((BT, D), x.dtype),
        compiler_params=_params(("parallel",)),
    )(x2, o_diff, o_nsa, w_o)
    return out.reshape(B, T, D)
```

```python
import math

import numpy as np
import jax
import jax.numpy as jnp
from jax import lax
from jax.experimental import pallas as pl
from jax.experimental.pallas import tpu as pltpu

F32 = jnp.float32
BF16 = jnp.bfloat16

D_MODEL = 1024
N_DIFF_HEADS = 4
DIFF_QK = 64
DIFF_V = 128
DIFF_W = 512
N_NSA_HEADS = 8
NSA_D = 64
N_KV = 2
N_GRP = 4
NSA_W = 512
CMP_BLOCK = 32
CMP_STRIDE = 16
CMP_HIDDEN = 256
SEL_BLOCK = 64
N_SEL_BLOCKS = 32
SEL_TOPK = 8
N_LOCAL = 2
FORCE_BONUS = 1000.0
WINDOW = 512
EPS = 1e-6
NEG_BIG = -1e30
LAM_INIT = 0.8 - 0.6 * math.exp(-0.3 * 0)
LOG2E = math.log2(math.e)

IN_MAIN = 3840
N_GATE = 24
IN_PAD = 3968
LANES = 128
SUBLANES = 8
BF16_ROWS = 16
SMALL_PARAM_ROWS = 32
SEL_MASK_BIAS = -32768.0

TM_PROJ = 1024
TM_OUT = 1024
TQ = 256
TK = 256
VMEM_LIMIT = 56 * 1024 * 1024
DIFF_ACC_ROWS = DIFF_V + SUBLANES
DIFF_VT_ROWS = DIFF_V + BF16_ROWS
DIFF_HEADS_PER_STEP = 4
NSA_ACC_ROWS = NSA_D + SUBLANES
NSA_VT_ROWS = NSA_D + BF16_ROWS


def _dot(a, b):
    return jnp.dot(a, b, preferred_element_type=F32)


def _sigmoid(y):
    return 1.0 / (1.0 + jnp.exp(-y))


def _seg_mean_sq(y, e):
    n = y.shape[1]
    y2 = (y * y).astype(BF16)
    if n == LANES:
        return _dot(y2, e[:LANES, :LANES])
    return jnp.concatenate([_dot(y2[:, c:c + 256], e) for c in range(0, n, 256)], axis=1)


def _seg_norm(y, e, gain):
    return y * lax.rsqrt(_seg_mean_sq(y, e) + EPS) * gain


def _transpose_bf16(a):
    return a.astype(F32).T.astype(BF16)


def _ones_rows(n_rows, width):
    return jnp.where(lax.broadcasted_iota(jnp.int32, (n_rows, width), 0) == 0, 1.0, 0.0).astype(BF16)


def _mask_cols(st, keep, width):
    n = st.shape[1]
    return jnp.concatenate([jnp.where(keep, st[:, a:a + width], -jnp.inf) for a in range(0, n, width)], axis=1)


def _online_softmax_step(st, vt_blk, m_ref, acc_ref):
    m_prev = m_ref[...]
    m_new = jnp.maximum(m_prev, jnp.max(st, axis=0, keepdims=True))
    alpha = jnp.exp2(m_prev - m_new)
    p = jnp.exp2(st - m_new).astype(BF16)
    pv = _dot(vt_blk, p)
    acc_ref[...] = alpha * acc_ref[...] + pv[:acc_ref.shape[0]]
    m_ref[...] = m_new


def _pipelined_sweep(n, n_chunks, qk, consume, st_a, st_b, interleave=True, first_scores_done=False, zero=None):
    chunks = range(n_chunks)

    def both(j_next, st_next, j_cur, st_cur):
        if interleave:
            for c in chunks:
                qk(j_next, st_next, c)
                consume(j_cur, st_cur, c)
        else:
            for c in chunks:
                qk(j_next, st_next, c)
            for c in chunks:
                consume(j_cur, st_cur, c)

    if not first_scores_done:
        for c in chunks:
            qk(0, st_a, c)

    def body(jj, carry):
        j = 2 * jj
        both(j + 1, st_b, j, st_a)
        both(j + 2, st_a, j + 1, st_b)
        return carry

    def tail_even():
        for c in chunks:
            consume(n, st_a, c)

    def tail_odd():
        both(n, st_b, n - 1, st_a)
        for c in chunks:
            consume(n, st_b, c)

    if zero is not None:
        lax.fori_loop(0, n // 2 + zero, body, 0)
        (tail_even if n % 2 == 0 else tail_odd)()
    elif isinstance(n, int):
        if n // 2 > 0:
            lax.fori_loop(0, n // 2, body, 0)
        (tail_even if n % 2 == 0 else tail_odd)()
    else:
        lax.fori_loop(0, n // 2, body, 0)
        pl.when(n % 2 == 0)(tail_even)
        pl.when(n % 2 == 1)(tail_odd)


def _proj_kernel(x_ref, g_ref, w_ref, e_ref, gq_ref, gk_ref, gnq_ref, gks_ref, gkw_ref, fs_ref, fw_ref,
                 dq_ref, dk_ref, dv_ref, dz_ref, nq_ref, kc_ref, vc_ref, kk_ref, vs_ref, vw_ref,
                 nz_ref, gate_ref):
    x = x_ref[...]
    ms = jnp.mean(x * x, axis=-1, keepdims=True)
    h = (x * lax.rsqrt(ms + EPS) * g_ref[...]).astype(BF16)
    e = e_ref[...]

    def proj(a, b):
        return _dot(h, w_ref[:, a:b])

    dq_ref[...] = _seg_norm(proj(0, 512), e, gq_ref[...]).astype(BF16)
    dk_ref[...] = _seg_norm(proj(512, 1024), e, gk_ref[...]).astype(BF16)
    dv_ref[...] = proj(1024, 1536).astype(BF16)
    y = proj(1536, 2048)
    dz_ref[...] = (y * _sigmoid(y)).astype(BF16)
    nq_ref[...] = _seg_norm(proj(2048, 2560), e, gnq_ref[...]).astype(BF16)
    y = proj(2560, 2816)
    kc_ref[...] = y[:, :LANES]
    vc_ref[...] = y[:, LANES:]

    low = lax.broadcasted_iota(jnp.int32, (1, LANES), 1) < 64
    y = proj(2816, 3072)
    kp = _seg_norm(y[:, :LANES], e, gks_ref[0:1, :]).astype(BF16)
    fs = fs_ref[...]
    kk_ref[0] = jnp.where(low, kp, fs)
    kk_ref[1] = jnp.where(low, fs, kp)
    vs_ref[...] = y[:, LANES:].astype(BF16)
    y = proj(3072, 3328)
    kp = _seg_norm(y[:, :LANES], e, gkw_ref[0:1, :]).astype(BF16)
    fw = fw_ref[...]
    kk_ref[2] = jnp.where(low, kp, fw)
    kk_ref[3] = jnp.where(low, fw, kp)
    vw_ref[...] = y[:, LANES:].astype(BF16)
    y = proj(3328, 3840)
    nz_ref[...] = (y * _sigmoid(y)).astype(BF16)
    gate_ref[...] = _sigmoid(proj(3840, 3968))


def _cmp_kernel(hk_ref, hv_ref, pa_ref, pb_ref, w1a_ref, w1b_ref, b1_ref, w2_ref, e_ref, gkc_ref,
                kc_ref, vct_ref):
    n_rows = hk_ref.shape[0] // CMP_STRIDE
    row_ok = lax.broadcasted_iota(jnp.int32, (n_rows, 1), 0) < (n_rows - 1)
    low = lax.broadcasted_iota(jnp.int32, (1, LANES), 1) < 64

    def mlp(h_ref, idx):
        hf = jnp.concatenate([h_ref[pl.ds(t, n_rows, stride=CMP_STRIDE), :] for t in range(CMP_STRIDE)], axis=1)
        ha = (hf + pa_ref[idx]).astype(BF16)
        hb = (hf + pb_ref[idx]).astype(BF16)
        a = _dot(ha, w1a_ref[idx])
        b = _dot(hb, w1b_ref[idx])
        hid = a + pltpu.roll(b, n_rows - 1, 0) + b1_ref[idx]
        hid = hid * _sigmoid(hid)
        return _dot(hid.astype(BF16), w2_ref[idx])

    yk = mlp(hk_ref, 0)
    yk = _seg_norm(yk, e_ref[...], gkc_ref[0:1, :])
    yk = jnp.where(row_ok, yk, 0.0).astype(BF16)
    zero = jnp.zeros_like(yk)
    kc_ref[0, 0] = jnp.where(low, yk, zero)
    kc_ref[0, 1] = jnp.where(low, zero, yk)
    yvt = jnp.where(row_ok, mlp(hv_ref, 1), 0.0).T.astype(BF16)
    vct_ref[0, 0] = yvt[:NSA_D]
    vct_ref[0, 1] = yvt[NSA_D:]


def _diff_kernel(zero_ref, q_ref, k_ref, v_ref, z_ref, fk_ref, qf_ref, lam_ref, gsub_ref, o_ref,
                 vt_ref, qq_ref, mask_ref, st_a, st_b, m_ref, acc_ref):
    seq = k_ref.shape[0]
    heads = range(DIFF_HEADS_PER_STEP)

    for hh in heads:
        vt_ref[hh, 0:DIFF_V, :] = _transpose_bf16(v_ref[:, hh * DIFF_V:(hh + 1) * DIFF_V])
        vt_ref[hh, DIFF_V:DIFF_VT_ROWS, :] = _ones_rows(BF16_ROWS, seq)
    causal = lax.broadcasted_iota(jnp.int32, (TK, TQ), 0) <= lax.broadcasted_iota(jnp.int32, (TK, TQ), 1)
    mask_ref[0] = jnp.zeros((TK, TQ), F32)
    mask_ref[1] = jnp.where(causal, 0.0, -jnp.inf)
    rows = lax.broadcasted_iota(jnp.int32, (LANES, TQ), 0)
    lam_v = lam_ref[0:4, 0:DIFF_QK]
    lam = (jnp.exp(jnp.sum(lam_v[0:1] * lam_v[1:2], axis=-1, keepdims=True))
           - jnp.exp(jnp.sum(lam_v[2:3] * lam_v[3:4], axis=-1, keepdims=True)) + LAM_INIT)

    def qk(i, j, st_ref, hh):
        k0 = j * TK if isinstance(j, int) else pl.multiple_of(j * TK, TK)
        diag = int(j == i) if isinstance(j, int) else jnp.where(j == i, 1, 0)
        mask = mask_ref[diag]
        keys = jnp.concatenate([k_ref[pl.ds(k0, TK), hh * LANES:(hh + 1) * LANES], fk_ref[pl.ds(k0, TK), :]],
                               axis=1)
        st = _dot(keys, qq_ref[i % 2, hh])
        st_ref[hh] = jnp.concatenate([st[:, a:a + TQ] + mask for a in range(0, 2 * TQ, TQ)], axis=1)

    def consume(j, st_ref, hh):
        k0 = j * TK if isinstance(j, int) else pl.multiple_of(j * TK, TK)
        _online_softmax_step(st_ref[hh], vt_ref[hh, :, pl.ds(k0, TK)], m_ref.at[hh], acc_ref.at[hh])

    def prepare(i, hh):
        qt = _transpose_bf16(q_ref[i * TQ:(i + 1) * TQ, hh * LANES:(hh + 1) * LANES])
        zero = jnp.zeros_like(qt)
        qq_ref[i % 2, hh, 0:LANES, :] = jnp.concatenate(
            [jnp.where(rows < 64, qt, zero), jnp.where(rows >= 64, qt, zero)], axis=1)
        qq_ref[i % 2, hh, LANES:2 * LANES, :] = jnp.concatenate([qf_ref[hh]] * (2 * TQ // LANES), axis=1)
        qk(i, 0, st_a, hh)

    def finalize(i, hh):
        acc = acc_ref[hh]
        o2 = acc[0:DIFF_V] / acc[DIFF_V:DIFF_V + 1]
        o = (o2[:, :TQ] - lam * o2[:, TQ:]).T
        ms = jnp.mean(o * o, axis=-1, keepdims=True)
        o = o * lax.rsqrt(ms + EPS) * gsub_ref[0:1, :] * (1.0 - LAM_INIT)
        sl = slice(hh * DIFF_V, (hh + 1) * DIFF_V)
        rs = slice(i * TQ, (i + 1) * TQ)
        o_ref[rs, sl] = (o * z_ref[rs, sl].astype(F32)).astype(BF16)

    def reset():
        m_ref[...] = jnp.full(m_ref.shape, NEG_BIG, F32)
        acc_ref[...] = jnp.zeros(acc_ref.shape, F32)

    reset()
    for hh in heads:
        prepare(0, hh)
    n_q = seq // TQ
    for i in range(n_q):
        _pipelined_sweep(i, DIFF_HEADS_PER_STEP, lambda j, st, hh: qk(i, j, st, hh), consume, st_a, st_b,
                         interleave=True, first_scores_done=True, zero=zero_ref[0])
        for hh in heads:
            if i + 1 < n_q:
                prepare(i + 1, hh)
            finalize(i, hh)
        reset()


def _nsa_kernel(q_ref, kc_ref, vct_ref, kk_ref, vs_ref, vw_ref, gate_ref, nz_ref, qf_ref, ovt_ref,
                o_ref, vt_ref, qq_ref, ocmp_ref, mask_ref, st_a, st_b, m_ref, acc_ref):
    seq = vs_ref.shape[0]
    n_q = seq // TQ
    SEL, WIN = 0, 1
    NO_MASK, CAUSAL, UPPER = 0, 1, 2
    kv_heads = range(N_KV)
    n_chunks = N_KV * N_GRP
    krow = lax.broadcasted_iota(jnp.int32, (TK, TQ), 0)
    qcol = lax.broadcasted_iota(jnp.int32, (TK, TQ), 1)
    rows = lax.broadcasted_iota(jnp.int32, (LANES, TQ), 0)
    bidx = lax.broadcasted_iota(jnp.int32, (N_SEL_BLOCKS, TQ), 0)
    row_f = bidx.astype(F32)

    for br, v_ref in ((SEL, vs_ref), (WIN, vw_ref)):
        vt = _transpose_bf16(v_ref[...])
        for kvh in kv_heads:
            vt_ref[br, kvh, 0:NSA_D, :] = vt[kvh * NSA_D:(kvh + 1) * NSA_D]
            vt_ref[br, kvh, NSA_D:NSA_VT_ROWS, :] = _ones_rows(BF16_ROWS, seq)
    mask_ref[NO_MASK] = jnp.zeros((TK, TQ), F32)
    mask_ref[CAUSAL] = jnp.where(krow <= qcol, 0.0, -jnp.inf)
    mask_ref[UPPER] = jnp.where(krow > qcol, 0.0, -jnp.inf)

    def reset():
        m_ref[...] = jnp.full(m_ref.shape, NEG_BIG, F32)
        acc_ref[...] = jnp.zeros(acc_ref.shape, F32)

    def first_virtual(i):
        return 2 - min(i, 2)

    def source(i, u):
        v = u + first_virtual(i)
        if isinstance(u, int):
            if v < 3:
                return WIN, (i - 2 + v) * TK, (UPPER if v == 0 else CAUSAL if v == 2 else NO_MASK)
            return SEL, (v - 3) * TK, (CAUSAL if v - 3 == i else NO_MASK)
        is_win = v < 3
        src = jnp.where(is_win, WIN, SEL)
        kb = jnp.where(is_win, i - 2 + v, v - 3)
        mt = jnp.where(is_win, jnp.where(v == 0, UPPER, jnp.where(v == 2, CAUSAL, NO_MASK)),
                       jnp.where(kb == i, CAUSAL, NO_MASK))
        return src, pl.multiple_of(kb * TK, TK), mt

    def qk(i, u, st_ref, c):
        kvh, g = divmod(c, N_GRP)
        src, k0, mt = source(i, u)
        cs = slice(g * TQ, (g + 1) * TQ)
        st_ref[kvh, :, cs] = (_dot(kk_ref[2 * src + kvh, pl.ds(k0, TK), :], qq_ref[i % 2, src, kvh, :, cs])
                              + mask_ref[mt])

    def consume(i, u, st_ref, c):
        kvh, g = divmod(c, N_GRP)
        src, k0, _ = source(i, u)
        cs = slice(g * TQ, (g + 1) * TQ)
        _online_softmax_step(st_ref[kvh, :, cs], vt_ref[src, kvh, :, pl.ds(k0, TK)],
                             m_ref.at[src, kvh, :, cs], acc_ref.at[src, kvh, :, cs])

    def prepare(i, interleaved):
        rs = slice(i * TQ, (i + 1) * TQ)
        slot = i % 2
        qts = [_transpose_bf16(q_ref[rs, g * LANES:(g + 1) * LANES]) for g in range(N_GRP)]
        qbs = []
        for kvh in kv_heads:
            half = (rows < 64) if kvh == 0 else (rows >= 64)
            qb = jnp.concatenate(
                [jnp.where(half, qts[g], jnp.concatenate([qf_ref[kvh, g]] * (TQ // LANES), axis=1))
                 for g in range(N_GRP)], axis=1)
            qq_ref[slot, WIN, kvh] = qb
            qq_ref[slot, SEL, kvh] = qb
            qbs.append(qb)
        qpos = i * TQ + lax.broadcasted_iota(jnp.int32, (1, TQ), 1)

        valid = (rows * CMP_STRIDE + (CMP_BLOCK - 1)) <= qpos
        blk_t = qpos >> 6
        valid_b = bidx <= blk_t
        forced = valid_b & ((bidx == 0) | (bidx >= blk_t - (N_LOCAL - 1)))
        scores = []
        for kvh in kv_heads:
            st = _dot(kc_ref[0, kvh], qbs[kvh])
            p_parts = []
            for g in range(N_GRP):
                sg = jnp.where(valid, st[:, g * TQ:(g + 1) * TQ], NEG_BIG)
                mx = jnp.max(sg, axis=0, keepdims=True)
                ex = jnp.where(valid, jnp.exp2(sg - mx), 0.0)
                den = jnp.sum(ex, axis=0, keepdims=True)
                p_parts.append(ex * jnp.where(den > 0.0, 1.0 / den, 0.0))
            pt = jnp.concatenate(p_parts, axis=1)
            ocmp_ref[slot, kvh] = _dot(vct_ref[0, kvh], pt.astype(BF16))
            psum = p_parts[0] + p_parts[1] + p_parts[2] + p_parts[3]
            ovt = ovt_ref[kvh]
            p_hi = psum.astype(BF16)
            p_lo = (psum - p_hi.astype(F32)).astype(BF16)
            seg0 = 96 - 64 * kvh
            imp = (_dot(ovt, p_hi) + _dot(ovt, p_lo))[seg0:seg0 + N_SEL_BLOCKS]
            scores.append(jnp.where(valid_b, imp, -1.0) + jnp.where(forced, FORCE_BONUS, 0.0))

        chosen = [jnp.zeros((N_SEL_BLOCKS, TQ), F32) for _ in kv_heads]
        for r in range(max(SEL_TOPK, n_chunks, len(interleaved))):
            if r < SEL_TOPK:
                for kvh in kv_heads:
                    best = jnp.max(scores[kvh], axis=0, keepdims=True)
                    first = jnp.min(jnp.where(scores[kvh] == best, row_f, 1e9), axis=0, keepdims=True)
                    pick = row_f == first
                    chosen[kvh] = jnp.where(pick, 1.0, chosen[kvh])
                    scores[kvh] = jnp.where(pick, -jnp.inf, scores[kvh])
            if r < n_chunks:
                qk(i, 0, st_a, r)
            if r < len(interleaved):
                interleaved[r]()
        for kvh in kv_heads:
            sel_bias = jnp.where(chosen[kvh] == 0.0, SEL_MASK_BIAS, 0.0).astype(BF16)
            seg0 = 96 - 64 * kvh
            for g in range(N_GRP):
                qq_ref[slot, SEL, kvh, seg0:seg0 + N_SEL_BLOCKS, g * TQ:(g + 1) * TQ] = sel_bias

    gate_rows = {}

    def output_piece(i, g):
        rs = slice(i * TQ, (i + 1) * TQ)
        if i not in gate_rows:
            gate_rows[i] = gate_ref[rs, :].T
        gt = gate_rows[i]
        cs = slice(g * TQ, (g + 1) * TQ)
        ots = []
        for kvh in kv_heads:
            gr = [gt[br * 8 + kvh * 4 + g:br * 8 + kvh * 4 + g + 1] for br in range(3)]
            a_s = acc_ref[SEL, kvh, :, cs]
            a_w = acc_ref[WIN, kvh, :, cs]
            ots.append(gr[0] * ocmp_ref[i % 2, kvh, :, cs]
                       + (gr[1] / a_s[NSA_D:NSA_D + 1]) * a_s[:NSA_D]
                       + (gr[2] / a_w[NSA_D:NSA_D + 1]) * a_w[:NSA_D])
        og = jnp.concatenate(ots, axis=0).T
        sl = slice(g * LANES, (g + 1) * LANES)
        o_ref[rs, sl] = (og * nz_ref[rs, sl].astype(F32)).astype(BF16)

    reset()
    prepare(0, [])
    for i in range(n_q):
        _pipelined_sweep(3 + i - first_virtual(i), n_chunks,
                         lambda u, st, c: qk(i, u, st, c), lambda u, st, c: consume(i, u, st, c),
                         st_a, st_b, first_scores_done=True)
        pieces = [(lambda g=g: output_piece(i, g)) for g in range(N_GRP)]
        if i + 1 < n_q:
            prepare(i + 1, pieces)
        else:
            for piece in pieces:
                piece()
        reset()


def _out_kernel(x_ref, od_ref, on_ref, w_ref, o_ref):
    acc = _dot(od_ref[...], w_ref[:DIFF_W, :]) + _dot(on_ref[...], w_ref[DIFF_W:, :])
    o_ref[...] = x_ref[...] + acc


def _params(sem):
    return pltpu.CompilerParams(dimension_semantics=sem, vmem_limit_bytes=VMEM_LIMIT)


def _bf16_pieces(v):
    out = []
    r = np.float32(v)
    for _ in range(3):
        p = np.asarray(r).astype(BF16).astype(np.float32)
        out.append(float(p))
        r = np.float32(r - p)
    return out


def _slope_features(slope):
    f = []
    for piece in _bf16_pieces(slope * LOG2E):
        f += [piece * 256.0, piece]
    return f


def _key_features(seq, with_blocks):
    t = np.arange(seq)
    f = np.zeros((seq, 64), np.float32)
    for r in range(3):
        f[:, 2 * r] = t // 256
        f[:, 2 * r + 1] = t % 256
    if with_blocks:
        f[t, 32 + t // SEL_BLOCK] = 1.0
    return jnp.asarray(np.concatenate([f, f], axis=1), dtype=BF16)


def _nsa_query_features():
    f = np.zeros((N_KV, N_GRP, LANES), np.float32)
    for kvh in range(N_KV):
        base = 64 if kvh == 0 else 0
        for g in range(N_GRP):
            f[kvh, g, base:base + 6] = _slope_features(2.0 ** (-(kvh * N_GRP + g + 1)))
    return jnp.asarray(np.repeat(f[..., None], LANES, axis=-1), dtype=BF16)


def _diff_query_features():
    f = np.zeros((N_DIFF_HEADS, LANES), np.float32)
    for h in range(N_DIFF_HEADS):
        f[h, 0:6] = _slope_features(2.0 ** (-8.0 * (h + 1) / N_DIFF_HEADS))
    return jnp.asarray(np.repeat(f[..., None], LANES, axis=-1), dtype=BF16)


def _overlap_t(seq):
    n_cmp = (seq - CMP_BLOCK) // CMP_STRIDE + 1
    n_sb = seq // SEL_BLOCK
    cs = np.arange(n_cmp) * CMP_STRIDE
    ss = np.arange(n_sb) * SEL_BLOCK
    ovl = ((cs[:, None] < ss[None, :] + SEL_BLOCK) & (cs[:, None] + CMP_BLOCK > ss[None, :])).astype(np.float32)
    out = np.zeros((N_KV, LANES, LANES), np.float32)
    out[0, 96:96 + n_sb, :n_cmp] = ovl.T
    out[1, 32:32 + n_sb, :n_cmp] = ovl.T
    return jnp.asarray(out, dtype=BF16)


def _to_pair_order(a, axis):
    shp = a.shape
    a = a.reshape(shp[:axis] + (N_KV, N_GRP, NSA_D) + shp[axis + 1:])
    a = jnp.swapaxes(a, axis, axis + 1)
    return a.reshape(shp)


def kernel(x, norm_g, w_in, diff_q_norm_g, diff_k_norm_g, diff_lambda_q1, diff_lambda_k1, diff_lambda_q2,
           diff_lambda_k2, diff_subln_g, nsa_q_norm_g, nsa_k_norm_g, cmp_pos, cmp_w1, cmp_b1, cmp_w2, w_out):
    B, T, D = x.shape
    BT = B * T
    assert D == D_MODEL and TQ == TK and T == N_SEL_BLOCKS * SEL_BLOCK and T % TM_PROJ == 0
    n_half = T // CMP_STRIDE
    nq = T // TQ

    w = w_in[0].astype(BF16)
    w_p = jnp.concatenate([w[:, :2048], _to_pair_order(w[:, 2048:2560], 1), w[:, 2560:3328],
                           _to_pair_order(w[:, 3328:3840], 1), w[:, IN_MAIN:],
                           jnp.zeros((D, IN_PAD - IN_MAIN - N_GATE), BF16)], axis=1)
    w_ob = w_out[0].astype(BF16)
    w_o = jnp.concatenate([w_ob[:DIFF_W], _to_pair_order(w_ob[DIFF_W:], 0)], axis=0)
    seg = np.arange(256) // 64
    e_mat = jnp.asarray((seg[:, None] == seg[None, :]).astype(np.float32) / 64.0, dtype=BF16)

    def small(a):
        a = a.astype(F32)
        return jnp.pad(a, ((0, SMALL_PARAM_ROWS - a.shape[0]), (0, LANES - a.shape[1])))

    gq = (jnp.tile(diff_q_norm_g[0], 8) * (DIFF_QK ** -0.5 * LOG2E))[None]
    gk = jnp.tile(diff_k_norm_g[0], 8)[None]
    gnq = (jnp.tile(nsa_q_norm_g[0], 8) * (NSA_D ** -0.5 * LOG2E))[None]
    gkc = small(jnp.tile(nsa_k_norm_g[0, 0], 2)[None])
    gks = small(jnp.tile(nsa_k_norm_g[0, 1], 2)[None])
    gkw = small(jnp.tile(nsa_k_norm_g[0, 2], 2)[None])
    feat_s = _key_features(T, True)
    feat_w = _key_features(T, False)

    x2 = x.reshape(BT, D)
    n_rt = BT // TM_PROJ
    rt_per_seq = T // TM_PROJ
    row_spec = lambda n: pl.BlockSpec((TM_PROJ, n), lambda i: (i, 0))
    full = lambda shp: pl.BlockSpec(shp, lambda i: (0,) * len(shp))
    feat_spec = pl.BlockSpec((TM_PROJ, LANES), lambda i: (i % rt_per_seq, 0))
    bt = lambda n, dt=BF16: jax.ShapeDtypeStruct((BT, n), dt)
    outs = pl.pallas_call(
        _proj_kernel,
        grid=(n_rt,),
        in_specs=[row_spec(D), full((1, D)), full((D, IN_PAD)), full((256, 256)),
                  full((1, 512)), full((1, 512)), full((1, 512)),
                  full((SMALL_PARAM_ROWS, LANES)), full((SMALL_PARAM_ROWS, LANES)),
                  feat_spec, feat_spec],
        out_specs=[row_spec(512), row_spec(512), row_spec(512), row_spec(512), row_spec(512),
                   row_spec(LANES), row_spec(LANES), pl.BlockSpec((4, TM_PROJ, LANES), lambda i: (0, i, 0)),
                   row_spec(LANES), row_spec(LANES), row_spec(512), row_spec(LANES)],
        out_shape=[bt(512), bt(512), bt(512), bt(512), bt(512), bt(LANES, F32), bt(LANES, F32),
                   jax.ShapeDtypeStruct((4, BT, LANES), BF16), bt(LANES), bt(LANES), bt(512), bt(LANES, F32)],
        compiler_params=_params(("parallel",)),
    )(x2, norm_g[0][None], w_p, e_mat, gq, gk, gnq, gks, gkw, feat_s, feat_w)
    dq, dk, dv, dz, nqp, kcr, vcr, kk, vs, vw, nz, gates = outs

    eye2 = jnp.eye(N_KV, dtype=F32)

    def pair_w1(w1):
        wh = w1.astype(BF16).reshape(2, 16, NSA_D, CMP_HIDDEN)
        z = jnp.zeros_like(wh)
        rows = jnp.stack([jnp.concatenate([wh, z], axis=-1), jnp.concatenate([z, wh], axis=-1)], axis=2)
        return rows.reshape(2, 16 * N_KV * NSA_D, N_KV * CMP_HIDDEN)

    def pair_w2(w2):
        z = w2[None, :, None, :] * eye2[:, None, :, None]
        return z.reshape(N_KV * CMP_HIDDEN, N_KV * NSA_D).astype(BF16)

    def pair_pos(p):
        ph = p.reshape(2, 16, 1, NSA_D)
        return jnp.broadcast_to(ph, (2, 16, N_KV, NSA_D)).reshape(2, 1, 16 * N_KV * NSA_D)

    w1k = pair_w1(cmp_w1[0, 0]); w1v = pair_w1(cmp_w1[0, 1])
    w1a = jnp.stack([w1k[0], w1v[0]]); w1b = jnp.stack([w1k[1], w1v[1]])
    w2p = jnp.stack([pair_w2(cmp_w2[0, 0]), pair_w2(cmp_w2[0, 1])])
    pk = pair_pos(cmp_pos[0, 0]); pv = pair_pos(cmp_pos[0, 1])
    pa = jnp.stack([pk[0], pv[0]]); pb = jnp.stack([pk[1], pv[1]])
    b1p = jnp.stack([jnp.tile(cmp_b1[0, 0], 2)[None], jnp.tile(cmp_b1[0, 1], 2)[None]])
    hspec = pl.BlockSpec((T, LANES), lambda b: (b, 0))
    kc, vct = pl.pallas_call(
        _cmp_kernel,
        grid=(B,),
        in_specs=[hspec, hspec, full((2, 1, 2048)), full((2, 1, 2048)), full((2, 2048, 512)),
                  full((2, 2048, 512)), full((2, 1, 512)), full((2, 512, LANES)), full((256, 256)),
                  full((SMALL_PARAM_ROWS, LANES))],
        out_specs=[pl.BlockSpec((1, 2, n_half, LANES), lambda b: (b, 0, 0, 0)),
                   pl.BlockSpec((1, 2, NSA_D, n_half), lambda b: (b, 0, 0, 0))],
        out_shape=[jax.ShapeDtypeStruct((B, 2, n_half, LANES), BF16),
                   jax.ShapeDtypeStruct((B, 2, NSA_D, n_half), BF16)],
        compiler_params=_params(("parallel",)),
    )(kcr, vcr, pa, pb, w1a, w1b, b1p, w2p, e_mat, gkc)

    lam_v = small(jnp.concatenate([diff_lambda_q1, diff_lambda_k1, diff_lambda_q2, diff_lambda_k2], axis=0))
    hps = DIFF_HEADS_PER_STEP
    seq_spec = pl.BlockSpec((T, hps * LANES), lambda b, h: (b, h))
    st_scratch = pltpu.VMEM((hps, TK, 2 * TQ), F32)
    o_diff = pl.pallas_call(
        _diff_kernel,
        grid=(B, N_DIFF_HEADS // hps),
        in_specs=[pl.BlockSpec(memory_space=pltpu.SMEM), seq_spec, seq_spec, seq_spec, seq_spec,
                  pl.BlockSpec((T, LANES), lambda b, h: (0, 0)),
                  pl.BlockSpec((hps, LANES, LANES), lambda b, h: (h, 0, 0)),
                  pl.BlockSpec((SMALL_PARAM_ROWS, LANES), lambda b, h: (0, 0)),
                  pl.BlockSpec((SMALL_PARAM_ROWS, LANES), lambda b, h: (0, 0))],
        out_specs=seq_spec,
        out_shape=bt(DIFF_W),
        scratch_shapes=[pltpu.VMEM((hps, DIFF_VT_ROWS, T), BF16),
                        pltpu.VMEM((2, hps, 2 * LANES, 2 * TQ), BF16),
                        pltpu.VMEM((2, TK, TQ), F32),
                        st_scratch, st_scratch,
                        pltpu.VMEM((hps, 1, 2 * TQ), F32), pltpu.VMEM((hps, DIFF_ACC_ROWS, 2 * TQ), F32)],
        compiler_params=_params(("arbitrary", "arbitrary")),
    )(jnp.zeros((1,), jnp.int32), dq, dk, dv, dz, feat_w, _diff_query_features(), lam_v, small(diff_subln_g[0][None]))

    n_all = N_GRP * TQ
    seq512 = pl.BlockSpec((T, 512), lambda b: (b, 0))
    seq128 = pl.BlockSpec((T, LANES), lambda b: (b, 0))
    st_nsa = pltpu.VMEM((N_KV, TK, n_all), F32)
    o_nsa = pl.pallas_call(
        _nsa_kernel,
        grid=(B,),
        in_specs=[seq512,
                  pl.BlockSpec((1, N_KV, n_half, LANES), lambda b: (b, 0, 0, 0)),
                  pl.BlockSpec((1, N_KV, NSA_D, n_half), lambda b: (b, 0, 0, 0)),
                  pl.BlockSpec((4, T, LANES), lambda b: (0, b, 0)),
                  seq128, seq128, seq128, seq512,
                  pl.BlockSpec((N_KV, N_GRP, LANES, LANES), lambda b: (0, 0, 0, 0)),
                  pl.BlockSpec((N_KV, LANES, LANES), lambda b: (0, 0, 0))],
        out_specs=seq512,
        out_shape=bt(NSA_W),
        scratch_shapes=[pltpu.VMEM((2, N_KV, NSA_VT_ROWS, T), BF16),
                        pltpu.VMEM((2, 2, N_KV, LANES, n_all), BF16),
                        pltpu.VMEM((2, N_KV, NSA_D, n_all), F32),
                        pltpu.VMEM((3, TK, TQ), F32),
                        st_nsa, st_nsa,
                        pltpu.VMEM((2, N_KV, 1, n_all), F32), pltpu.VMEM((2, N_KV, NSA_ACC_ROWS, n_all), F32)],
        compiler_params=_params(("arbitrary",)),
    )(nqp, kc, vct, kk, vs, vw, gates, nz, _nsa_query_features(), _overlap_t(T))

    out_rows = lambda n: pl.BlockSpec((TM_OUT, n), lambda i: (i, 0))
    out = pl.pallas_call(
        _out_kernel,
        grid=(BT // TM_OUT,),
        in_specs=[out_rows(D), out_rows(DIFF_W), out_rows(NSA_W), full((D, D))],
        out_specs=out_rows(D),
        out_shape=jax.ShapeDtypeStruct((BT, D), x.dtype),
        compiler_params=_params(("parallel",)),
    )(x2, o_diff, o_nsa, w_o)
    return out.reshape(B, T, D)
```

```python
import math

import numpy as np
import jax
import jax.numpy as jnp
from jax import lax
from jax.experimental import pallas as pl
from jax.experimental.pallas import tpu as pltpu

F32 = jnp.float32
BF16 = jnp.bfloat16

D_MODEL = 1024
N_DIFF_HEADS = 4
DIFF_QK = 64
DIFF_V = 128
DIFF_W = 512
N_NSA_HEADS = 8
NSA_D = 64
N_KV = 2
N_GRP = 4
NSA_W = 512
CMP_BLOCK = 32
CMP_STRIDE = 16
CMP_HIDDEN = 256
SEL_BLOCK = 64
N_SEL_BLOCKS = 32
SEL_TOPK = 8
N_LOCAL = 2
FORCE_BONUS = 1000.0
WINDOW = 512
EPS = 1e-6
NEG_BIG = -1e30
LAM_INIT = 0.8 - 0.6 * math.exp(-0.3 * 0)
LOG2E = math.log2(math.e)

IN_MAIN = 3840
N_GATE = 24
IN_PAD = 3968
LANES = 128
SUBLANES = 8
BF16_ROWS = 16
SMALL_PARAM_ROWS = 32
SEL_MASK_BIAS = -32768.0

TM_PROJ = 1024
TM_OUT = 1024
TQ = 256
TK = 256
VMEM_LIMIT = 56 * 1024 * 1024
DIFF_ACC_ROWS = DIFF_V + SUBLANES
DIFF_VT_ROWS = DIFF_V + BF16_ROWS
DIFF_HEADS_PER_STEP = 4
NSA_ACC_ROWS = NSA_D + SUBLANES
NSA_VT_ROWS = NSA_D + BF16_ROWS


def _dot(a, b):
    return jnp.dot(a, b, preferred_element_type=F32)


def _sigmoid(y):
    return 1.0 / (1.0 + jnp.exp(-y))


def _seg_mean_sq(y, e):
    n = y.shape[1]
    y2 = (y * y).astype(BF16)
    if n == LANES:
        return _dot(y2, e[:LANES, :LANES])
    return jnp.concatenate([_dot(y2[:, c:c + 256], e) for c in range(0, n, 256)], axis=1)


def _seg_norm(y, e, gain):
    return y * lax.rsqrt(_seg_mean_sq(y, e) + EPS) * gain


def _transpose_bf16(a):
    return a.astype(F32).T.astype(BF16)


def _ones_rows(n_rows, width):
    return jnp.where(lax.broadcasted_iota(jnp.int32, (n_rows, width), 0) == 0, 1.0, 0.0).astype(BF16)


def _online_softmax_step(st, vt_blk, m_ref, acc_ref):
    m_prev = m_ref[...]
    m_new = jnp.maximum(m_prev, jnp.max(st, axis=0, keepdims=True))
    alpha = jnp.exp2(m_prev - m_new)
    p = jnp.exp2(st - m_new).astype(BF16)
    pv = _dot(vt_blk, p)
    acc_ref[...] = alpha * acc_ref[...] + pv[:acc_ref.shape[0]]
    m_ref[...] = m_new


def _pipelined_sweep(n, n_chunks, qk, consume, st_a, st_b, interleave=True, first_scores_done=False, zero=None):
    chunks = range(n_chunks)

    def both(j_next, st_next, j_cur, st_cur):
        if interleave:
            for c in chunks:
                qk(j_next, st_next, c)
                consume(j_cur, st_cur, c)
        else:
            for c in chunks:
                qk(j_next, st_next, c)
            for c in chunks:
                consume(j_cur, st_cur, c)

    if not first_scores_done:
        for c in chunks:
            qk(0, st_a, c)

    def body(jj, carry):
        j = 2 * jj
        both(j + 1, st_b, j, st_a)
        both(j + 2, st_a, j + 1, st_b)
        return carry

    def tail_even():
        for c in chunks:
            consume(n, st_a, c)

    def tail_odd():
        both(n, st_b, n - 1, st_a)
        for c in chunks:
            consume(n, st_b, c)

    if zero is not None:
        lax.fori_loop(0, n // 2 + zero, body, 0)
        (tail_even if n % 2 == 0 else tail_odd)()
    elif isinstance(n, int):
        if n // 2 > 0:
            lax.fori_loop(0, n // 2, body, 0)
        (tail_even if n % 2 == 0 else tail_odd)()
    else:
        lax.fori_loop(0, n // 2, body, 0)
        pl.when(n % 2 == 0)(tail_even)
        pl.when(n % 2 == 1)(tail_odd)


def _proj_kernel(x_ref, g_ref, w_ref, e_ref, gq_ref, gk_ref, gnq_ref, gks_ref, gkw_ref, fs_ref, fw_ref,
                 dq_ref, dk_ref, dv_ref, dz_ref, nq_ref, kc_ref, vc_ref, kk_ref, vs_ref, vw_ref,
                 nz_ref, gate_ref):
    x = x_ref[...]
    ms = jnp.mean(x * x, axis=-1, keepdims=True)
    h = (x * lax.rsqrt(ms + EPS) * g_ref[...]).astype(BF16)
    e = e_ref[...]

    def proj(a, b):
        return _dot(h, w_ref[:, a:b])

    dq_ref[...] = _seg_norm(proj(0, 512), e, gq_ref[...]).astype(BF16)
    dk_ref[...] = _seg_norm(proj(512, 1024), e, gk_ref[...]).astype(BF16)
    dv_ref[...] = proj(1024, 1536).astype(BF16)
    y = proj(1536, 2048)
    dz_ref[...] = (y * _sigmoid(y)).astype(BF16)
    nq_ref[...] = _seg_norm(proj(2048, 2560), e, gnq_ref[...]).astype(BF16)
    y = proj(2560, 2816)
    kc_ref[...] = y[:, :LANES]
    vc_ref[...] = y[:, LANES:]

    low = lax.broadcasted_iota(jnp.int32, (1, LANES), 1) < 64
    y = proj(2816, 3072)
    kp = _seg_norm(y[:, :LANES], e, gks_ref[0:1, :]).astype(BF16)
    fs = fs_ref[...]
    kk_ref[0] = jnp.where(low, kp, fs)
    kk_ref[1] = jnp.where(low, fs, kp)
    vs_ref[...] = y[:, LANES:].astype(BF16)
    y = proj(3072, 3328)
    kp = _seg_norm(y[:, :LANES], e, gkw_ref[0:1, :]).astype(BF16)
    fw = fw_ref[...]
    kk_ref[2] = jnp.where(low, kp, fw)
    kk_ref[3] = jnp.where(low, fw, kp)
    vw_ref[...] = y[:, LANES:].astype(BF16)
    y = proj(3328, 3840)
    nz_ref[...] = (y * _sigmoid(y)).astype(BF16)
    gate_ref[...] = _sigmoid(proj(3840, 3968))


def _cmp_kernel(hk_ref, hv_ref, pa_ref, pb_ref, w1a_ref, w1b_ref, b1_ref, w2_ref, e_ref, gkc_ref,
                kc_ref, vct_ref):
    n_rows = hk_ref.shape[0] // CMP_STRIDE
    row_ok = lax.broadcasted_iota(jnp.int32, (n_rows, 1), 0) < (n_rows - 1)
    low = lax.broadcasted_iota(jnp.int32, (1, LANES), 1) < 64

    def mlp(h_ref, idx):
        hf = jnp.concatenate([h_ref[pl.ds(t, n_rows, stride=CMP_STRIDE), :] for t in range(CMP_STRIDE)], axis=1)
        ha = (hf + pa_ref[idx]).astype(BF16)
        hb = (hf + pb_ref[idx]).astype(BF16)
        a = _dot(ha, w1a_ref[idx])
        b = _dot(hb, w1b_ref[idx])
        hid = a + pltpu.roll(b, n_rows - 1, 0) + b1_ref[idx]
        hid = hid * _sigmoid(hid)
        return _dot(hid.astype(BF16), w2_ref[idx])

    yk = mlp(hk_ref, 0)
    yk = _seg_norm(yk, e_ref[...], gkc_ref[0:1, :])
    yk = jnp.where(row_ok, yk, 0.0).astype(BF16)
    zero = jnp.zeros_like(yk)
    kc_ref[0, 0] = jnp.where(low, yk, zero)
    kc_ref[0, 1] = jnp.where(low, zero, yk)
    yvt = jnp.where(row_ok, mlp(hv_ref, 1), 0.0).T.astype(BF16)
    vct_ref[0, 0] = yvt[:NSA_D]
    vct_ref[0, 1] = yvt[NSA_D:]


def _diff_kernel(zero_ref, q_ref, k_ref, v_ref, z_ref, fk_ref, qf_ref, lam_ref, gsub_ref, o_ref,
                 vt_ref, qq_ref, mask_ref, st_a, st_b, m_ref, acc_ref):
    seq = k_ref.shape[0]
    heads = range(DIFF_HEADS_PER_STEP)

    for hh in heads:
        vt_ref[hh, 0:DIFF_V, :] = _transpose_bf16(v_ref[:, hh * DIFF_V:(hh + 1) * DIFF_V])
        vt_ref[hh, DIFF_V:DIFF_VT_ROWS, :] = _ones_rows(BF16_ROWS, seq)
    causal = lax.broadcasted_iota(jnp.int32, (TK, TQ), 0) <= lax.broadcasted_iota(jnp.int32, (TK, TQ), 1)
    mask_ref[0] = jnp.zeros((TK, TQ), F32)
    mask_ref[1] = jnp.where(causal, 0.0, -jnp.inf)
    rows = lax.broadcasted_iota(jnp.int32, (LANES, TQ), 0)
    lam_v = lam_ref[0:4, 0:DIFF_QK]
    lam = (jnp.exp(jnp.sum(lam_v[0:1] * lam_v[1:2], axis=-1, keepdims=True))
           - jnp.exp(jnp.sum(lam_v[2:3] * lam_v[3:4], axis=-1, keepdims=True)) + LAM_INIT)

    def qk(i, j, st_ref, hh):
        k0 = j * TK if isinstance(j, int) else pl.multiple_of(j * TK, TK)
        diag = int(j == i) if isinstance(j, int) else jnp.where(j == i, 1, 0)
        mask = mask_ref[diag]
        keys = jnp.concatenate([k_ref[pl.ds(k0, TK), hh * LANES:(hh + 1) * LANES], fk_ref[pl.ds(k0, TK), :]],
                               axis=1)
        st = _dot(keys, qq_ref[i % 2, hh])
        st_ref[hh] = jnp.concatenate([st[:, a:a + TQ] + mask for a in range(0, 2 * TQ, TQ)], axis=1)

    def consume(j, st_ref, hh):
        k0 = j * TK if isinstance(j, int) else pl.multiple_of(j * TK, TK)
        _online_softmax_step(st_ref[hh], vt_ref[hh, :, pl.ds(k0, TK)], m_ref.at[hh], acc_ref.at[hh])

    def prepare(i, hh):
        qt = _transpose_bf16(q_ref[i * TQ:(i + 1) * TQ, hh * LANES:(hh + 1) * LANES])
        zero = jnp.zeros_like(qt)
        qq_ref[i % 2, hh, 0:LANES, :] = jnp.concatenate(
            [jnp.where(rows < 64, qt, zero), jnp.where(rows >= 64, qt, zero)], axis=1)
        qq_ref[i % 2, hh, LANES:2 * LANES, :] = jnp.concatenate([qf_ref[hh]] * (2 * TQ // LANES), axis=1)
        qk(i, 0, st_a, hh)

    def finalize(i, hh):
        acc = acc_ref[hh]
        o2 = acc[0:DIFF_V] / acc[DIFF_V:DIFF_V + 1]
        o = (o2[:, :TQ] - lam * o2[:, TQ:]).T
        ms = jnp.mean(o * o, axis=-1, keepdims=True)
        o = o * lax.rsqrt(ms + EPS) * gsub_ref[0:1, :] * (1.0 - LAM_INIT)
        sl = slice(hh * DIFF_V, (hh + 1) * DIFF_V)
        rs = slice(i * TQ, (i + 1) * TQ)
        o_ref[rs, sl] = (o * z_ref[rs, sl].astype(F32)).astype(BF16)

    def reset():
        m_ref[...] = jnp.full(m_ref.shape, NEG_BIG, F32)
        acc_ref[...] = jnp.zeros(acc_ref.shape, F32)

    reset()
    for hh in heads:
        prepare(0, hh)
    n_q = seq // TQ
    for i in range(n_q):
        _pipelined_sweep(i, DIFF_HEADS_PER_STEP, lambda j, st, hh: qk(i, j, st, hh), consume, st_a, st_b,
                         interleave=True, first_scores_done=True, zero=zero_ref[0])
        for hh in heads:
            if i + 1 < n_q:
                prepare(i + 1, hh)
            finalize(i, hh)
        reset()


def _nsa_kernel(q_ref, kc_ref, vct_ref, kk_ref, vs_ref, vw_ref, gate_ref, nz_ref, qf_ref, ovt_ref,
                o_ref, vt_ref, qq_ref, ocmp_ref, mask_ref, st_a, st_b, m_ref, acc_ref):
    seq = vs_ref.shape[0]
    n_q = seq // TQ
    SEL, WIN = 0, 1
    NO_MASK, CAUSAL, UPPER = 0, 1, 2
    kv_heads = range(N_KV)
    n_chunks = N_KV * N_GRP
    krow = lax.broadcasted_iota(jnp.int32, (TK, TQ), 0)
    qcol = lax.broadcasted_iota(jnp.int32, (TK, TQ), 1)
    rows = lax.broadcasted_iota(jnp.int32, (LANES, TQ), 0)
    bidx = lax.broadcasted_iota(jnp.int32, (N_SEL_BLOCKS, TQ), 0)
    row_f = bidx.astype(F32)

    for br, v_ref in ((SEL, vs_ref), (WIN, vw_ref)):
        vt = _transpose_bf16(v_ref[...])
        for kvh in kv_heads:
            vt_ref[br, kvh, 0:NSA_D, :] = vt[kvh * NSA_D:(kvh + 1) * NSA_D]
            vt_ref[br, kvh, NSA_D:NSA_VT_ROWS, :] = _ones_rows(BF16_ROWS, seq)
    mask_ref[NO_MASK] = jnp.zeros((TK, TQ), F32)
    mask_ref[CAUSAL] = jnp.where(krow <= qcol, 0.0, -jnp.inf)
    mask_ref[UPPER] = jnp.where(krow > qcol, 0.0, -jnp.inf)

    def reset():
        m_ref[...] = jnp.full(m_ref.shape, NEG_BIG, F32)
        acc_ref[...] = jnp.zeros(acc_ref.shape, F32)

    def first_virtual(i):
        return 2 - min(i, 2)

    def source(i, u):
        v = u + first_virtual(i)
        if isinstance(u, int):
            if v < 3:
                return WIN, (i - 2 + v) * TK, (UPPER if v == 0 else CAUSAL if v == 2 else NO_MASK)
            return SEL, (v - 3) * TK, (CAUSAL if v - 3 == i else NO_MASK)
        is_win = v < 3
        src = jnp.where(is_win, WIN, SEL)
        kb = jnp.where(is_win, i - 2 + v, v - 3)
        mt = jnp.where(is_win, jnp.where(v == 0, UPPER, jnp.where(v == 2, CAUSAL, NO_MASK)),
                       jnp.where(kb == i, CAUSAL, NO_MASK))
        return src, pl.multiple_of(kb * TK, TK), mt

    def qk(i, u, st_ref, c):
        kvh, g = divmod(c, N_GRP)
        src, k0, mt = source(i, u)
        cs = slice(g * TQ, (g + 1) * TQ)
        st_ref[kvh, :, cs] = (_dot(kk_ref[2 * src + kvh, pl.ds(k0, TK), :], qq_ref[i % 2, src, kvh, :, cs])
                              + mask_ref[mt])

    def consume(i, u, st_ref, c):
        kvh, g = divmod(c, N_GRP)
        src, k0, _ = source(i, u)
        cs = slice(g * TQ, (g + 1) * TQ)
        _online_softmax_step(st_ref[kvh, :, cs], vt_ref[src, kvh, :, pl.ds(k0, TK)],
                             m_ref.at[src, kvh, :, cs], acc_ref.at[src, kvh, :, cs])

    def prepare(i, interleaved):
        rs = slice(i * TQ, (i + 1) * TQ)
        slot = i % 2
        qts = [_transpose_bf16(q_ref[rs, g * LANES:(g + 1) * LANES]) for g in range(N_GRP)]
        qbs = []
        for kvh in kv_heads:
            half = (rows < 64) if kvh == 0 else (rows >= 64)
            qb = jnp.concatenate(
                [jnp.where(half, qts[g], jnp.concatenate([qf_ref[kvh, g]] * (TQ // LANES), axis=1))
                 for g in range(N_GRP)], axis=1)
            qq_ref[slot, WIN, kvh] = qb
            qq_ref[slot, SEL, kvh] = qb
            qbs.append(qb)
        qpos = i * TQ + lax.broadcasted_iota(jnp.int32, (1, TQ), 1)

        valid = (rows * CMP_STRIDE + (CMP_BLOCK - 1)) <= qpos
        blk_t = qpos >> 6
        valid_b = bidx <= blk_t
        forced = valid_b & ((bidx == 0) | (bidx >= blk_t - (N_LOCAL - 1)))
        scores = []
        for kvh in kv_heads:
            st = _dot(kc_ref[0, kvh], qbs[kvh])
            p_parts = []
            for g in range(N_GRP):
                sg = jnp.where(valid, st[:, g * TQ:(g + 1) * TQ], NEG_BIG)
                mx = jnp.max(sg, axis=0, keepdims=True)
                ex = jnp.where(valid, jnp.exp2(sg - mx), 0.0)
                den = jnp.sum(ex, axis=0, keepdims=True)
                p_parts.append(ex * jnp.where(den > 0.0, 1.0 / den, 0.0))
            pt = jnp.concatenate(p_parts, axis=1)
            ocmp_ref[slot, kvh] = _dot(vct_ref[0, kvh], pt.astype(BF16))
            psum = p_parts[0] + p_parts[1] + p_parts[2] + p_parts[3]
            ovt = ovt_ref[kvh]
            p_hi = psum.astype(BF16)
            p_lo = (psum - p_hi.astype(F32)).astype(BF16)
            seg0 = 96 - 64 * kvh
            imp = (_dot(ovt, p_hi) + _dot(ovt, p_lo))[seg0:seg0 + N_SEL_BLOCKS]
            scores.append(jnp.where(valid_b, imp, -1.0) + jnp.where(forced, FORCE_BONUS, 0.0))

        chosen = [jnp.zeros((N_SEL_BLOCKS, TQ), F32) for _ in kv_heads]
        for r in range(max(SEL_TOPK, n_chunks, len(interleaved))):
            if r < SEL_TOPK:
                for kvh in kv_heads:
                    best = jnp.max(scores[kvh], axis=0, keepdims=True)
                    first = jnp.min(jnp.where(scores[kvh] == best, row_f, 1e9), axis=0, keepdims=True)
                    pick = row_f == first
                    chosen[kvh] = jnp.where(pick, 1.0, chosen[kvh])
                    scores[kvh] = jnp.where(pick, -jnp.inf, scores[kvh])
            if r < n_chunks:
                qk(i, 0, st_a, r)
            if r < len(interleaved):
                interleaved[r]()
        for kvh in kv_heads:
            sel_bias = jnp.where(chosen[kvh] == 0.0, SEL_MASK_BIAS, 0.0).astype(BF16)
            seg0 = 96 - 64 * kvh
            for g in range(N_GRP):
                qq_ref[slot, SEL, kvh, seg0:seg0 + N_SEL_BLOCKS, g * TQ:(g + 1) * TQ] = sel_bias

    gate_rows = {}

    def output_piece(i, g):
        rs = slice(i * TQ, (i + 1) * TQ)
        if i not in gate_rows:
            gate_rows[i] = gate_ref[rs, :].T
        gt = gate_rows[i]
        cs = slice(g * TQ, (g + 1) * TQ)
        ots = []
        for kvh in kv_heads:
            gr = [gt[br * 8 + kvh * 4 + g:br * 8 + kvh * 4 + g + 1] for br in range(3)]
            a_s = acc_ref[SEL, kvh, :, cs]
            a_w = acc_ref[WIN, kvh, :, cs]
            ots.append(gr[0] * ocmp_ref[i % 2, kvh, :, cs]
                       + (gr[1] / a_s[NSA_D:NSA_D + 1]) * a_s[:NSA_D]
                       + (gr[2] / a_w[NSA_D:NSA_D + 1]) * a_w[:NSA_D])
        og = jnp.concatenate(ots, axis=0).T
        sl = slice(g * LANES, (g + 1) * LANES)
        o_ref[rs, sl] = (og * nz_ref[rs, sl].astype(F32)).astype(BF16)

    reset()
    prepare(0, [])
    for i in range(n_q):
        _pipelined_sweep(3 + i - first_virtual(i), n_chunks,
                         lambda u, st, c: qk(i, u, st, c), lambda u, st, c: consume(i, u, st, c),
                         st_a, st_b, first_scores_done=True)
        pieces = [(lambda g=g: output_piece(i, g)) for g in range(N_GRP)]
        if i + 1 < n_q:
            prepare(i + 1, pieces)
        else:
            for piece in pieces:
                piece()
        reset()


def _out_kernel(x_ref, od_ref, on_ref, w_ref, o_ref):
    acc = _dot(od_ref[...], w_ref[:DIFF_W, :]) + _dot(on_ref[...], w_ref[DIFF_W:, :])
    o_ref[...] = x_ref[...] + acc


def _params(sem):
    return pltpu.CompilerParams(dimension_semantics=sem, vmem_limit_bytes=VMEM_LIMIT)


def _bf16_pieces(v):
    out = []
    r = np.float32(v)
    for _ in range(3):
        p = np.asarray(r).astype(BF16).astype(np.float32)
        out.append(float(p))
        r = np.float32(r - p)
    return out


def _slope_features(slope):
    f = []
    for piece in _bf16_pieces(slope * LOG2E):
        f += [piece * 256.0, piece]
    return f


def _key_features(seq, with_blocks):
    t = np.arange(seq)
    f = np.zeros((seq, 64), np.float32)
    for r in range(3):
        f[:, 2 * r] = t // 256
        f[:, 2 * r + 1] = t % 256
    if with_blocks:
        f[t, 32 + t // SEL_BLOCK] = 1.0
    return jnp.asarray(np.concatenate([f, f], axis=1), dtype=BF16)


def _nsa_query_features():
    f = np.zeros((N_KV, N_GRP, LANES), np.float32)
    for kvh in range(N_KV):
        base = 64 if kvh == 0 else 0
        for g in range(N_GRP):
            f[kvh, g, base:base + 6] = _slope_features(2.0 ** (-(kvh * N_GRP + g + 1)))
    return jnp.asarray(np.repeat(f[..., None], LANES, axis=-1), dtype=BF16)


def _diff_query_features():
    f = np.zeros((N_DIFF_HEADS, LANES), np.float32)
    for h in range(N_DIFF_HEADS):
        f[h, 0:6] = _slope_features(2.0 ** (-8.0 * (h + 1) / N_DIFF_HEADS))
    return jnp.asarray(np.repeat(f[..., None], LANES, axis=-1), dtype=BF16)


def _overlap_t(seq):
    n_cmp = (seq - CMP_BLOCK) // CMP_STRIDE + 1
    n_sb = seq // SEL_BLOCK
    cs = np.arange(n_cmp) * CMP_STRIDE
    ss = np.arange(n_sb) * SEL_BLOCK
    ovl = ((cs[:, None] < ss[None, :] + SEL_BLOCK) & (cs[:, None] + CMP_BLOCK > ss[None, :])).astype(np.float32)
    out = np.zeros((N_KV, LANES, LANES), np.float32)
    out[0, 96:96 + n_sb, :n_cmp] = ovl.T
    out[1, 32:32 + n_sb, :n_cmp] = ovl.T
    return jnp.asarray(out, dtype=BF16)


def _to_pair_order(a, axis):
    shp = a.shape
    a = a.reshape(shp[:axis] + (N_KV, N_GRP, NSA_D) + shp[axis + 1:])
    a = jnp.swapaxes(a, axis, axis + 1)
    return a.reshape(shp)


def kernel(x, norm_g, w_in, diff_q_norm_g, diff_k_norm_g, diff_lambda_q1, diff_lambda_k1, diff_lambda_q2,
           diff_lambda_k2, diff_subln_g, nsa_q_norm_g, nsa_k_norm_g, cmp_pos, cmp_w1, cmp_b1, cmp_w2, w_out):
    B, T, D = x.shape
    BT = B * T
    assert D == D_MODEL and TQ == TK and T == N_SEL_BLOCKS * SEL_BLOCK and T % TM_PROJ == 0
    assert WINDOW == 2 * TK and N_NSA_HEADS == N_KV * N_GRP
    n_half = T // CMP_STRIDE

    w = w_in[0].astype(BF16)
    w_p = jnp.concatenate([w[:, :2048], _to_pair_order(w[:, 2048:2560], 1), w[:, 2560:3328],
                           _to_pair_order(w[:, 3328:3840], 1), w[:, IN_MAIN:],
                           jnp.zeros((D, IN_PAD - IN_MAIN - N_GATE), BF16)], axis=1)
    w_ob = w_out[0].astype(BF16)
    w_o = jnp.concatenate([w_ob[:DIFF_W], _to_pair_order(w_ob[DIFF_W:], 0)], axis=0)
    seg = np.arange(256) // 64
    e_mat = jnp.asarray((seg[:, None] == seg[None, :]).astype(np.float32) / 64.0, dtype=BF16)

    def small(a):
        a = a.astype(F32)
        return jnp.pad(a, ((0, SMALL_PARAM_ROWS - a.shape[0]), (0, LANES - a.shape[1])))

    gq = (jnp.tile(diff_q_norm_g[0], 8) * (DIFF_QK ** -0.5 * LOG2E))[None]
    gk = jnp.tile(diff_k_norm_g[0], 8)[None]
    gnq = (jnp.tile(nsa_q_norm_g[0], 8) * (NSA_D ** -0.5 * LOG2E))[None]
    gkc = small(jnp.tile(nsa_k_norm_g[0, 0], 2)[None])
    gks = small(jnp.tile(nsa_k_norm_g[0, 1], 2)[None])
    gkw = small(jnp.tile(nsa_k_norm_g[0, 2], 2)[None])
    feat_s = _key_features(T, True)
    feat_w = _key_features(T, False)

    x2 = x.reshape(BT, D)
    n_rt = BT // TM_PROJ
    rt_per_seq = T // TM_PROJ
    row_spec = lambda n: pl.BlockSpec((TM_PROJ, n), lambda i: (i, 0))
    full = lambda shp: pl.BlockSpec(shp, lambda i: (0,) * len(shp))
    feat_spec = pl.BlockSpec((TM_PROJ, LANES), lambda i: (i % rt_per_seq, 0))
    bt = lambda n, dt=BF16: jax.ShapeDtypeStruct((BT, n), dt)
    outs = pl.pallas_call(
        _proj_kernel,
        grid=(n_rt,),
        in_specs=[row_spec(D), full((1, D)), full((D, IN_PAD)), full((256, 256)),
                  full((1, 512)), full((1, 512)), full((1, 512)),
                  full((SMALL_PARAM_ROWS, LANES)), full((SMALL_PARAM_ROWS, LANES)),
                  feat_spec, feat_spec],
        out_specs=[row_spec(512), row_spec(512), row_spec(512), row_spec(512), row_spec(512),
                   row_spec(LANES), row_spec(LANES), pl.BlockSpec((4, TM_PROJ, LANES), lambda i: (0, i, 0)),
                   row_spec(LANES), row_spec(LANES), row_spec(512), row_spec(LANES)],
        out_shape=[bt(512), bt(512), bt(512), bt(512), bt(512), bt(LANES, F32), bt(LANES, F32),
                   jax.ShapeDtypeStruct((4, BT, LANES), BF16), bt(LANES), bt(LANES), bt(512), bt(LANES, F32)],
        compiler_params=_params(("parallel",)),
    )(x2, norm_g[0][None], w_p, e_mat, gq, gk, gnq, gks, gkw, feat_s, feat_w)
    dq, dk, dv, dz, nqp, kcr, vcr, kk, vs, vw, nz, gates = outs

    eye2 = jnp.eye(N_KV, dtype=F32)

    def pair_w1(w1):
        wh = w1.astype(BF16).reshape(2, 16, NSA_D, CMP_HIDDEN)
        z = jnp.zeros_like(wh)
        rows = jnp.stack([jnp.concatenate([wh, z], axis=-1), jnp.concatenate([z, wh], axis=-1)], axis=2)
        return rows.reshape(2, 16 * N_KV * NSA_D, N_KV * CMP_HIDDEN)

    def pair_w2(w2):
        z = w2[None, :, None, :] * eye2[:, None, :, None]
        return z.reshape(N_KV * CMP_HIDDEN, N_KV * NSA_D).astype(BF16)

    def pair_pos(p):
        ph = p.reshape(2, 16, 1, NSA_D)
        return jnp.broadcast_to(ph, (2, 16, N_KV, NSA_D)).reshape(2, 1, 16 * N_KV * NSA_D)

    w1k = pair_w1(cmp_w1[0, 0]); w1v = pair_w1(cmp_w1[0, 1])
    w1a = jnp.stack([w1k[0], w1v[0]]); w1b = jnp.stack([w1k[1], w1v[1]])
    w2p = jnp.stack([pair_w2(cmp_w2[0, 0]), pair_w2(cmp_w2[0, 1])])
    pk = pair_pos(cmp_pos[0, 0]); pv = pair_pos(cmp_pos[0, 1])
    pa = jnp.stack([pk[0], pv[0]]); pb = jnp.stack([pk[1], pv[1]])
    b1p = jnp.stack([jnp.tile(cmp_b1[0, 0], 2)[None], jnp.tile(cmp_b1[0, 1], 2)[None]])
    hspec = pl.BlockSpec((T, LANES), lambda b: (b, 0))
    kc, vct = pl.pallas_call(
        _cmp_kernel,
        grid=(B,),
        in_specs=[hspec, hspec, full((2, 1, 2048)), full((2, 1, 2048)), full((2, 2048, 512)),
                  full((2, 2048, 512)), full((2, 1, 512)), full((2, 512, LANES)), full((256, 256)),
                  full((SMALL_PARAM_ROWS, LANES))],
        out_specs=[pl.BlockSpec((1, 2, n_half, LANES), lambda b: (b, 0, 0, 0)),
                   pl.BlockSpec((1, 2, NSA_D, n_half), lambda b: (b, 0, 0, 0))],
        out_shape=[jax.ShapeDtypeStruct((B, 2, n_half, LANES), BF16),
                   jax.ShapeDtypeStruct((B, 2, NSA_D, n_half), BF16)],
        compiler_params=_params(("parallel",)),
    )(kcr, vcr, pa, pb, w1a, w1b, b1p, w2p, e_mat, gkc)

    lam_v = small(jnp.concatenate([diff_lambda_q1, diff_lambda_k1, diff_lambda_q2, diff_lambda_k2], axis=0))
    hps = DIFF_HEADS_PER_STEP
    seq_spec = pl.BlockSpec((T, hps * LANES), lambda b, h: (b, h))
    st_scratch = pltpu.VMEM((hps, TK, 2 * TQ), F32)
    o_diff = pl.pallas_call(
        _diff_kernel,
        grid=(B, N_DIFF_HEADS // hps),
        in_specs=[pl.BlockSpec(memory_space=pltpu.SMEM), seq_spec, seq_spec, seq_spec, seq_spec,
                  pl.BlockSpec((T, LANES), lambda b, h: (0, 0)),
                  pl.BlockSpec((hps, LANES, LANES), lambda b, h: (h, 0, 0)),
                  pl.BlockSpec((SMALL_PARAM_ROWS, LANES), lambda b, h: (0, 0)),
                  pl.BlockSpec((SMALL_PARAM_ROWS, LANES), lambda b, h: (0, 0))],
        out_specs=seq_spec,
        out_shape=bt(DIFF_W),
        scratch_shapes=[pltpu.VMEM((hps, DIFF_VT_ROWS, T), BF16),
                        pltpu.VMEM((2, hps, 2 * LANES, 2 * TQ), BF16),
                        pltpu.VMEM((2, TK, TQ), F32),
                        st_scratch, st_scratch,
                        pltpu.VMEM((hps, 1, 2 * TQ), F32), pltpu.VMEM((hps, DIFF_ACC_ROWS, 2 * TQ), F32)],
        compiler_params=_params(("arbitrary", "arbitrary")),
    )(jnp.zeros((1,), jnp.int32), dq, dk, dv, dz, feat_w, _diff_query_features(), lam_v, small(diff_subln_g[0][None]))

    n_all = N_GRP * TQ
    seq512 = pl.BlockSpec((T, 512), lambda b: (b, 0))
    seq128 = pl.BlockSpec((T, LANES), lambda b: (b, 0))
    st_nsa = pltpu.VMEM((N_KV, TK, n_all), F32)
    o_nsa = pl.pallas_call(
        _nsa_kernel,
        grid=(B,),
        in_specs=[seq512,
                  pl.BlockSpec((1, N_KV, n_half, LANES), lambda b: (b, 0, 0, 0)),
                  pl.BlockSpec((1, N_KV, NSA_D, n_half), lambda b: (b, 0, 0, 0)),
                  pl.BlockSpec((4, T, LANES), lambda b: (0, b, 0)),
                  seq128, seq128, seq128, seq512,
                  pl.BlockSpec((N_KV, N_GRP, LANES, LANES), lambda b: (0, 0, 0, 0)),
                  pl.BlockSpec((N_KV, LANES, LANES), lambda b: (0, 0, 0))],
        out_specs=seq512,
        out_shape=bt(NSA_W),
        scratch_shapes=[pltpu.VMEM((2, N_KV, NSA_VT_ROWS, T), BF16),
                        pltpu.VMEM((2, 2, N_KV, LANES, n_all), BF16),
                        pltpu.VMEM((2, N_KV, NSA_D, n_all), F32),
                        pltpu.VMEM((3, TK, TQ), F32),
                        st_nsa, st_nsa,
                        pltpu.VMEM((2, N_KV, 1, n_all), F32), pltpu.VMEM((2, N_KV, NSA_ACC_ROWS, n_all), F32)],
        compiler_params=_params(("arbitrary",)),
    )(nqp, kc, vct, kk, vs, vw, gates, nz, _nsa_query_features(), _overlap_t(T))

    out_rows = lambda n: pl.BlockSpec((TM_OUT, n), lambda i: (i, 0))
    out = pl.pallas_call(
        _out_kernel,
        grid=(BT // TM_OUT,),
        in_specs=[out_rows(D), out_rows(DIFF_W), out_rows(NSA_W), full((D, D))],
        out_specs=out_rows(D),
        out_shape=jax.ShapeDtypeStruct((BT, D), x.dtype),
        compiler_params=_params(("parallel",)),
    )(x2, o_diff, o_nsa, w_o)
    return out.reshape(B, T, D)
```
